```python
import math
import jax
import jax.numpy as jnp
from jax import lax
import numpy as np

D_MODEL = 1024
BATCH = 2
SEQ = 8192
DEPTH = 1

N_META = 16
Q_BLOCK = 128
ROPE_THETA = 500000.0
NORM_EPS = 1e-6
DA_HEADS = 4
DA_QK_DIM = 64
DA_V_DIM = 2 * DA_QK_DIM
DA_WIDTH = DA_HEADS * DA_V_DIM
DA_QK_WIDTH = DA_HEADS * 2 * DA_QK_DIM
ROPE_DIM = DA_QK_DIM // 4
RW_HEAD = 64
RW_WIDTH = D_MODEL - DA_WIDTH
RW_HEADS = RW_WIDTH // RW_HEAD
D_DECAY_LORA = 64
D_AAA_LORA = 64
D_GATE_LORA = 160
GN_EPS = 64e-5
MIX_WIDTH = DA_WIDTH + RW_WIDTH
DA_IN = 2 * DA_QK_WIDTH + DA_WIDTH
RW_IN = 3 * RW_WIDTH + D_DECAY_LORA + D_AAA_LORA + D_GATE_LORA
N_IN = DA_IN + RW_IN
D_FF = 2816

kernel_name = "hymba_diffattn_rwkv7_macaron"


def rms_norm(x, g, eps=NORM_EPS):
    xf = x.astype(jnp.float32)
    y = xf * lax.rsqrt(jnp.mean(xf * xf, axis=-1, keepdims=True) + eps)
    return (y * g.astype(jnp.float32)).astype(x.dtype)


def swiglu(x, w_gate, w_up, w_down):
    return (jax.nn.silu(x @ w_gate) * (x @ w_up)) @ w_down


def rope_tables(length):
    pos = jnp.arange(length, dtype=jnp.float32)
    inv_freq = ROPE_THETA ** (-jnp.arange(0, ROPE_DIM, 2, dtype=jnp.float32) / ROPE_DIM)
    ang = pos[:, None] * inv_freq[None, :]
    return jnp.cos(ang), jnp.sin(ang)


def apply_partial_rope(x, cos, sin):
    half = ROPE_DIM // 2
    c = cos[None, :, None, None, :].astype(x.dtype)
    s = sin[None, :, None, None, :].astype(x.dtype)
    x1, x2, rest = x[..., :half], x[..., half:ROPE_DIM], x[..., ROPE_DIM:]
    return jnp.concatenate([x1 * c - x2 * s, x2 * c + x1 * s, rest], axis=-1)


def token_shift(u, mix):
    prev = jnp.pad(u, ((0, 0), (1, 0), (0, 0)))[:, :-1]
    return u + (prev - u) * mix


def diff_attention(q, k, v, q_norm, k_norm, lq1, lk1, lq2, lk2, subln, cos, sin, lam_init):
    B, L, _ = q.shape
    q = q.reshape(B, L, DA_HEADS, 2, DA_QK_DIM)
    k = k.reshape(B, L, DA_HEADS, 2, DA_QK_DIM)
    v = v.reshape(B, L, DA_HEADS, DA_V_DIM)
    q = apply_partial_rope(rms_norm(q, q_norm), cos, sin)
    k = apply_partial_rope(rms_norm(k, k_norm), cos, sin)
    f32 = jnp.float32
    lam = (jnp.exp(jnp.sum(lq1.astype(f32) * lk1.astype(f32)))
           - jnp.exp(jnp.sum(lq2.astype(f32) * lk2.astype(f32))) + lam_init)
    kh = jnp.transpose(k, (0, 2, 3, 1, 4))
    vh = jnp.transpose(v, (0, 2, 1, 3))
    nb = L // Q_BLOCK
    qb = jnp.transpose(q, (0, 2, 3, 1, 4)).reshape(B, DA_HEADS, 2, nb, Q_BLOCK, DA_QK_DIM)
    qb = jnp.transpose(qb, (3, 0, 1, 2, 4, 5))
    kpos = jnp.arange(L)
    scale = DA_QK_DIM ** -0.5

    def one_block(args):
        q_blk, bi = args
        s = jnp.einsum('bhcqd,bhckd->bhcqk', q_blk, kh, preferred_element_type=f32) * scale
        qpos = bi * Q_BLOCK + jnp.arange(Q_BLOCK)
        s = jnp.where(kpos[None, :] <= qpos[:, None], s, -jnp.inf)
        p = jax.nn.softmax(s, axis=-1)
        attn = p[:, :, 0] - lam * p[:, :, 1]
        return jnp.einsum('bhqk,bhkd->bhqd', attn.astype(vh.dtype), vh)

    o = lax.map(one_block, (qb, jnp.arange(nb)))
    o = jnp.transpose(o, (1, 0, 3, 2, 4)).reshape(B, L, DA_HEADS, DA_V_DIM)
    o = rms_norm(o, subln) * (1.0 - lam_init)
    return o.reshape(B, L, DA_WIDTH)


def rwkv7_time_mix(r, k, v, w_lo, a_lo, g_lo, w0, w2, a0, a2, g2, k_k, k_a, r_k, ln_w, ln_b):
    B, L, C = r.shape
    f32 = jnp.float32

    def heads(t):
        return t.astype(f32).reshape(B, L, RW_HEADS, RW_HEAD)

    w_log = -jax.nn.softplus(-(w0 + jnp.tanh(w_lo) @ w2).astype(f32)) - 0.5
    decay = jnp.exp(-jnp.exp(w_log))
    a = jax.nn.sigmoid((a0 + a_lo @ a2).astype(f32))
    g = jax.nn.sigmoid(g_lo) @ g2
    kk = heads(k * k_k)
    kk = kk / jnp.maximum(jnp.linalg.norm(kk, axis=-1, keepdims=True), 1e-12)
    k = k.astype(f32) * (1.0 + (a - 1.0) * k_a.astype(f32))
    rh, kh, vh, wh, ah = heads(r), heads(k), heads(v), heads(decay), heads(a)
    a_neg = -kk
    b_vec = kk * ah

    def step(S, inp):
        r_t, w_t, k_t, v_t, an_t, b_t = inp
        sa = jnp.einsum('bhvk,bhk->bhv', S, an_t)
        S = S * w_t[:, :, None, :] + sa[..., None] * b_t[:, :, None, :] + v_t[..., None] * k_t[:, :, None, :]
        return S, jnp.einsum('bhvk,bhk->bhv', S, r_t)

    def seq_major(t):
        return jnp.swapaxes(t, 0, 1)

    S0 = jnp.zeros((B, RW_HEADS, RW_HEAD, RW_HEAD), f32)
    _, y = lax.scan(step, S0, (seq_major(rh), seq_major(wh), seq_major(kh),
                                seq_major(vh), seq_major(a_neg), seq_major(b_vec)))
    y = seq_major(y)
    mu = jnp.mean(y, axis=-1, keepdims=True)
    var = jnp.mean(jnp.square(y - mu), axis=-1, keepdims=True)
    y = ((y - mu) * lax.rsqrt(var + GN_EPS)).reshape(B, L, C) * ln_w.astype(f32) + ln_b.astype(f32)
    bonus = jnp.sum(rh * kh * r_k.astype(f32), axis=-1, keepdims=True) * vh
    out = (y + bonus.reshape(B, L, C)) * g.astype(f32)
    return out.astype(r.dtype)


def setup_inputs(seed: int = 0) -> dict:
    key = jax.random.key(seed)
    ks = jax.random.split(key, 32)
    f = jnp.float32

    def nrm(k, shape, scale):
        return jax.random.normal(k, shape, f) * scale

    def gain(k, shape):
        return 1.0 + 0.02 * jax.random.normal(k, shape, f)

    Dp = DEPTH
    ratio = jnp.arange(RW_WIDTH, dtype=f) / (RW_WIDTH - 1)
    w0_base = -7.0 + 5.0 * ratio ** 0.85 + 0.5
    return {
        "x": nrm(ks[0], (BATCH, SEQ, D_MODEL), 1.0),
        "meta_tokens": nrm(ks[1], (N_META, D_MODEL), 1.0),
        "ffn1_norm": gain(ks[2], (Dp, D_MODEL)),
        "ffn1_w_gate": nrm(ks[3], (Dp, D_MODEL, D_FF), D_MODEL ** -0.5),
        "ffn1_w_up": nrm(ks[4], (Dp, D_MODEL, D_FF), D_MODEL ** -0.5),
        "ffn1_w_down": nrm(ks[5], (Dp, D_FF, D_MODEL), D_FF ** -0.5),
        "mix_norm": gain(ks[6], (Dp, D_MODEL)),
        "w_in": nrm(ks[7], (Dp, D_MODEL, N_IN), D_MODEL ** -0.5),
        "da_q_norm": gain(ks[8], (Dp, DA_QK_DIM)),
        "da_k_norm": gain(ks[9], (Dp, DA_QK_DIM)),
        "da_lambda_q1": nrm(ks[10], (Dp, DA_QK_DIM), 0.1),
        "da_lambda_k1": nrm(ks[11], (Dp, DA_QK_DIM), 0.1),
        "da_lambda_q2": nrm(ks[12], (Dp, DA_QK_DIM), 0.1),
        "da_lambda_k2": nrm(ks[13], (Dp, DA_QK_DIM), 0.1),
        "da_subln": gain(ks[14], (Dp, DA_V_DIM)),
        "rw_shift_mix": jax.random.uniform(ks[15], (Dp, RW_IN), f),
        "rw_w0": w0_base[None, :] + nrm(ks[16], (Dp, RW_WIDTH), 0.1),
        "rw_w2": nrm(ks[17], (Dp, D_DECAY_LORA, RW_WIDTH), 0.1 * D_DECAY_LORA ** -0.5),
        "rw_a0": nrm(ks[18], (Dp, RW_WIDTH), 0.1),
        "rw_a2": nrm(ks[19], (Dp, D_AAA_LORA, RW_WIDTH), 0.1 * D_AAA_LORA ** -0.5),
        "rw_g2": nrm(ks[20], (Dp, D_GATE_LORA, RW_WIDTH), D_GATE_LORA ** -0.5),
        "rw_k_k": 0.85 + nrm(ks[21], (Dp, RW_WIDTH), 0.02),
        "rw_k_a": 1.0 + nrm(ks[22], (Dp, RW_WIDTH), 0.02),
        "rw_r_k": nrm(ks[23], (Dp, RW_HEADS, RW_HEAD), 0.1),
        "rw_ln_w": gain(ks[24], (Dp, RW_WIDTH)),
        "rw_ln_b": nrm(ks[25], (Dp, RW_WIDTH), 0.02),
        "w_out": nrm(ks[26], (Dp, MIX_WIDTH, D_MODEL), MIX_WIDTH ** -0.5),
        "ffn2_norm": gain(ks[27], (Dp, D_MODEL)),
        "ffn2_w_gate": nrm(ks[28], (Dp, D_MODEL, D_FF), D_MODEL ** -0.5),
        "ffn2_w_up": nrm(ks[29], (Dp, D_MODEL, D_FF), D_MODEL ** -0.5),
        "ffn2_w_down": nrm(ks[30], (Dp, D_FF, D_MODEL), D_FF ** -0.5),
    }


def reference(x, meta_tokens, ffn1_norm, ffn1_w_gate, ffn1_w_up, ffn1_w_down, mix_norm, w_in,
              da_q_norm, da_k_norm, da_lambda_q1, da_lambda_k1, da_lambda_q2, da_lambda_k2, da_subln,
              rw_shift_mix, rw_w0, rw_w2, rw_a0, rw_a2, rw_g2, rw_k_k, rw_k_a, rw_r_k, rw_ln_w, rw_ln_b,
              w_out, ffn2_norm, ffn2_w_gate, ffn2_w_up, ffn2_w_down):
    B, T, _ = x.shape
    L = N_META + T
    L_pad = -(-L // Q_BLOCK) * Q_BLOCK
    meta = jnp.broadcast_to(meta_tokens.astype(x.dtype)[None], (B, N_META, D_MODEL))
    h = jnp.concatenate([meta, x], axis=1)
    h = jnp.pad(h, ((0, 0), (0, L_pad - L), (0, 0)))
    cos, sin = rope_tables(L_pad)
    rw_splits = [RW_WIDTH, 2 * RW_WIDTH, 3 * RW_WIDTH,
                 3 * RW_WIDTH + D_DECAY_LORA, 3 * RW_WIDTH + D_DECAY_LORA + D_AAA_LORA]
    for l in range(DEPTH):
        lam_init = 0.8 - 0.6 * math.exp(-0.3 * l)
        h = h + 0.5 * swiglu(rms_norm(h, ffn1_norm[l]), ffn1_w_gate[l], ffn1_w_up[l], ffn1_w_down[l])
        n = rms_norm(h, mix_norm[l])
        u = n @ w_in[l]
        u_da, u_rw = u[..., :DA_IN], u[..., DA_IN:]
        q, k, v = jnp.split(u_da, [DA_QK_WIDTH, 2 * DA_QK_WIDTH], axis=-1)
        u_rw = token_shift(u_rw, rw_shift_mix[l])
        r_rw, k_rw, v_rw, w_lo, a_lo, g_lo = jnp.split(u_rw, rw_splits, axis=-1)
        o_da = diff_attention(q, k, v, da_q_norm[l], da_k_norm[l], da_lambda_q1[l], da_lambda_k1[l],
                              da_lambda_q2[l], da_lambda_k2[l], da_subln[l], cos, sin, lam_init)
        o_rw = rwkv7_time_mix(r_rw, k_rw, v_rw, w_lo, a_lo, g_lo, rw_w0[l], rw_w2[l], rw_a0[l],
                              rw_a2[l], rw_g2[l], rw_k_k[l], rw_k_a[l], rw_r_k[l], rw_ln_w[l], rw_ln_b[l])
        h = h + jnp.concatenate([o_da, o_rw], axis=-1) @ w_out[l]
        h = h + 0.5 * swiglu(rms_norm(h, ffn2_norm[l]), ffn2_w_gate[l], ffn2_w_up[l], ffn2_w_down[l])
    return h[:, N_META:L]
```

```python
import functools
import math

import jax
import jax.numpy as jnp
from jax import lax
from jax.experimental import pallas as pl
from jax.experimental.pallas import tpu as pltpu

F32 = jnp.float32
BF16 = jnp.bfloat16

D_MODEL = 1024
N_META = 16
Q_BLOCK = 128
ROPE_THETA = 500000.0
NORM_EPS = 1e-6
DA_HEADS = 4
DA_QK_DIM = 64
DA_V_DIM = 2 * DA_QK_DIM
DA_WIDTH = DA_HEADS * DA_V_DIM
DA_QK_WIDTH = DA_HEADS * 2 * DA_QK_DIM
ROPE_DIM = DA_QK_DIM // 4
RW_HEAD = 64
RW_WIDTH = D_MODEL - DA_WIDTH
RW_HEADS = RW_WIDTH // RW_HEAD
D_DECAY_LORA = 64
D_AAA_LORA = 64
D_GATE_LORA = 160
GN_EPS = 64e-5
DA_IN = 2 * DA_QK_WIDTH + DA_WIDTH
RW_IN = 3 * RW_WIDTH + D_DECAY_LORA + D_AAA_LORA + D_GATE_LORA
D_FF = 2816

LANES = 128
VMEM_LIMIT_BYTES = 56 * 1024 * 1024

ROW_TILE = 640
FF_CHUNK = 256
ATTN_TILE = 640
RW_CHUNK = 64
RW_ROWS = 2 * RW_CHUNK
RW_LORA_PAD = 2 * LANES
RW_IN_PAD = 3 * RW_WIDTH + LANES + RW_LORA_PAD
N_IN_PAD = DA_IN + RW_IN_PAD

_NT = (((1,), (1,)), ((), ()))
_TN = (((0,), (0,)), ((), ()))


def _dot(a, b):
    return jnp.dot(a, b, preferred_element_type=F32)


def _dot_nt(a, b):
    return lax.dot_general(a, b, _NT, preferred_element_type=F32)


def _dot_tn(a, b):
    return lax.dot_general(a, b, _TN, preferred_element_type=F32)


def _split2(x):
    hi = x.astype(BF16)
    lo = (x - hi.astype(F32)).astype(BF16)
    return hi, lo


def _dot_split(a, b):
    a_hi, a_lo = _split2(a)
    b_hi, b_lo = _split2(b)
    return _dot(a_hi, b_hi) + (_dot(a_hi, b_lo) + _dot(a_lo, b_hi))


def _seg_sum(x, ones_bd):
    hi, lo = _split2(x)
    return _dot(hi, ones_bd) + _dot(lo, ones_bd)


def _rms_rows(h, gain):
    ms = jnp.mean(h * h, axis=-1, keepdims=True)
    return h * lax.rsqrt(ms + NORM_EPS) * gain


def _ffn_kernel(*refs, has_mix):
    if has_mix:
        (h_ref, oda_ref, orw_ref, wout_ref, g_ref, wgu_ref, wd_ref, o_ref, xn_ref, acc_ref) = refs
        h = (h_ref[...] + _dot(oda_ref[...], wout_ref[0:DA_WIDTH, :])
             + _dot(orw_ref[...], wout_ref[DA_WIDTH:DA_WIDTH + RW_WIDTH, :]))
    else:
        (h_ref, g_ref, wgu_ref, wd_ref, o_ref, xn_ref, acc_ref) = refs
        h = h_ref[...]
    o_ref[...] = h
    xn_ref[...] = _rms_rows(h, g_ref[...]).astype(BF16)
    acc_ref[...] = jnp.zeros_like(acc_ref)

    def body(c, carry):
        gu = _dot(xn_ref[...], wgu_ref[c])
        g = gu[:, :FF_CHUNK]
        u = gu[:, FF_CHUNK:]
        act = (g * jax.nn.sigmoid(g)) * u
        acc_ref[...] += _dot(act.astype(BF16), wd_ref[c])
        return carry

    lax.fori_loop(0, wgu_ref.shape[0], body, 0)
    o_ref[...] = o_ref[...] + 0.5 * acc_ref[...]


def _ffn(h, norm_g, wgu, wd, mix=None):
    m, d = h.shape
    tm = ROW_TILE
    n_chunks = wgu.shape[0]
    row = lambda i: (i, 0)
    const2 = lambda i: (0, 0)
    const3 = lambda i: (0, 0, 0)
    in_specs = [pl.BlockSpec((tm, d), row)]
    args = [h]
    if mix is not None:
        o_da, o_rw, w_out = mix
        in_specs += [pl.BlockSpec((tm, DA_WIDTH), row), pl.BlockSpec((tm, RW_WIDTH), row),
                     pl.BlockSpec(w_out.shape, const2)]
        args += [o_da, o_rw, w_out]
    in_specs += [pl.BlockSpec((1, d), const2),
                 pl.BlockSpec(wgu.shape, const3, pipeline_mode=pl.Buffered(1)),
                 pl.BlockSpec(wd.shape, const3, pipeline_mode=pl.Buffered(1))]
    args += [norm_g.reshape(1, d), wgu, wd]
    return pl.pallas_call(
        functools.partial(_ffn_kernel, has_mix=mix is not None),
        grid=(m // tm,),
        in_specs=in_specs,
        out_specs=pl.BlockSpec((tm, d), row),
        out_shape=jax.ShapeDtypeStruct((m, d), F32),
        scratch_shapes=[pltpu.VMEM((tm, d), BF16), pltpu.VMEM((tm, d), F32)],
        compiler_params=pltpu.CompilerParams(dimension_semantics=("arbitrary",),
                                             vmem_limit_bytes=VMEM_LIMIT_BYTES),
        name="ffn_mix" if mix is not None else "ffn",
    )(*args)


def _qk_prep(u, gain_ref, c, s1, s2, ones_bd, scale, out_ref):
    for j in range(DA_QK_WIDTH // LANES):
        sl = slice(LANES * j, LANES * (j + 1))
        x = u[:, sl]
        ss = _seg_sum(x * x, ones_bd)
        xn = x * lax.rsqrt(ss * (1.0 / DA_QK_DIM) + NORM_EPS) * gain_ref[:, sl]
        half = ROPE_DIM // 2
        xr = xn * c + pltpu.roll(xn, LANES - half, 1) * s1 + pltpu.roll(xn, half, 1) * s2
        out_ref[:, sl] = (xr * scale).astype(BF16)


def _mixin_kernel(h_ref, g_ref, win_ref, qg_ref, kg_ref, c_ref, s1_ref, s2_ref, ones_ref,
                  q_ref, k_ref, v_ref, urw_ref, *, q_scale):
    xn = _rms_rows(h_ref[...], g_ref[...]).astype(BF16)
    qw = DA_QK_WIDTH
    uq = _dot(xn, win_ref[:, 0:qw])
    uk = _dot(xn, win_ref[:, qw:2 * qw])
    v_ref[...] = _dot(xn, win_ref[:, 2 * qw:DA_IN]).astype(BF16)
    urw_ref[...] = _dot(xn, win_ref[:, DA_IN:N_IN_PAD])
    c, s1, s2, ones_bd = c_ref[...], s1_ref[...], s2_ref[...], ones_ref[...]
    _qk_prep(uq, qg_ref, c, s1, s2, ones_bd, q_scale, q_ref)
    _qk_prep(uk, kg_ref, c, s1, s2, ones_bd, 1.0, k_ref)


def _mix_in(h, norm_g, w_in_p, q_gain, k_gain, rope_c, rope_s1, rope_s2, ones_bd, l_pad):
    m, d = h.shape
    tm = ROW_TILE
    tiles_per_seq = l_pad // tm
    row = lambda i: (i, 0)
    pos = lambda i: (i % tiles_per_seq, 0)
    const2 = lambda i: (0, 0)
    q_scale = DA_QK_DIM ** -0.5 * math.log2(math.e)
    return pl.pallas_call(
        functools.partial(_mixin_kernel, q_scale=q_scale),
        grid=(m // tm,),
        in_specs=[pl.BlockSpec((tm, d), row), pl.BlockSpec((1, d), const2),
                  pl.BlockSpec(w_in_p.shape, const2, pipeline_mode=pl.Buffered(1)),
                  pl.BlockSpec((1, DA_QK_WIDTH), const2), pl.BlockSpec((1, DA_QK_WIDTH), const2),
                  pl.BlockSpec((tm, LANES), pos), pl.BlockSpec((tm, LANES), pos),
                  pl.BlockSpec((tm, LANES), pos), pl.BlockSpec((LANES, LANES), const2)],
        out_specs=[pl.BlockSpec((tm, DA_QK_WIDTH), row), pl.BlockSpec((tm, DA_QK_WIDTH), row),
                   pl.BlockSpec((tm, DA_WIDTH), row), pl.BlockSpec((tm, RW_IN_PAD), row)],
        out_shape=[jax.ShapeDtypeStruct((m, DA_QK_WIDTH), BF16), jax.ShapeDtypeStruct((m, DA_QK_WIDTH), BF16),
                   jax.ShapeDtypeStruct((m, DA_WIDTH), BF16), jax.ShapeDtypeStruct((m, RW_IN_PAD), F32)],
        compiler_params=pltpu.CompilerParams(dimension_semantics=("arbitrary",),
                                             vmem_limit_bytes=VMEM_LIMIT_BYTES),
        name="mix_in",
    )(h, norm_g.reshape(1, d), w_in_p, q_gain, k_gain, rope_c, rope_s1, rope_s2, ones_bd)


def _attn_kernel(lam_ref, q_ref, k_ref, v_ref, sub_ref, o_ref, q2_s, vx_s, m_s, l_s, acc_s, *, tq, lam_init):
    qi = pl.program_id(2)

    @pl.when(qi == 0)
    def _():
        vx_s[:, 0:LANES] = v_ref[...]
        vx_s[:, LANES:2 * LANES] = jnp.ones((vx_s.shape[0], LANES), BF16)

    q = q_ref[...]
    lane = lax.broadcasted_iota(jnp.int32, q.shape, 1)
    zero = jnp.zeros_like(q)
    q2_s[0:tq, :] = jnp.where(lane < DA_QK_DIM, q, zero)
    q2_s[tq:2 * tq, :] = jnp.where(lane < DA_QK_DIM, zero, q)
    m_s[...] = jnp.full(m_s.shape, -1e30, F32)
    l_s[...] = jnp.zeros_like(l_s)
    acc_s[...] = jnp.zeros_like(acc_s)

    def step(j, masked):
        start = pl.multiple_of(j * tq, tq)
        s = _dot_nt(q2_s[...], k_ref[pl.ds(start, tq), :])
        if masked:
            row = lax.broadcasted_iota(jnp.int32, s.shape, 0)
            col = lax.broadcasted_iota(jnp.int32, s.shape, 1)
            row = jnp.where(row >= tq, row - tq, row)
            s = jnp.where(col <= row, s, -jnp.inf)
        m_prev = m_s[...]
        m_new = jnp.maximum(m_prev, jnp.max(s, axis=1, keepdims=True))
        alpha = jnp.exp2(m_prev - m_new)
        p = jnp.exp2(s - jnp.concatenate([m_new] * (tq // LANES), axis=1))
        pv = _dot(p.astype(BF16), vx_s[pl.ds(start, tq), :])
        acc_s[...] = acc_s[...] * alpha + pv[:, 0:LANES]
        l_s[...] = l_s[...] * alpha + pv[:, LANES:2 * LANES]
        m_s[...] = m_new

    def loop_body(j, carry):
        step(j, False)
        return carry

    lax.fori_loop(0, qi, loop_body, 0)
    step(qi, True)

    lam1 = jnp.exp(jnp.sum(lam_ref[0:1, :] * lam_ref[1:2, :], axis=1, keepdims=True))
    lam2 = jnp.exp(jnp.sum(lam_ref[2:3, :] * lam_ref[3:4, :], axis=1, keepdims=True))
    lam = lam1 - lam2 + lam_init
    o = acc_s[0:tq, :] / l_s[0:tq, :] - lam * (acc_s[tq:2 * tq, :] / l_s[tq:2 * tq, :])
    o = _rms_rows(o, sub_ref[...]) * (1.0 - lam_init)
    o_ref[...] = o.astype(o_ref.dtype)


def _attention(q, k, v, lam_vecs, subln, lam_init):
    b, l_pad, _ = q.shape
    tq = ATTN_TILE
    nq = l_pad // tq
    return pl.pallas_call(
        functools.partial(_attn_kernel, tq=tq, lam_init=lam_init),
        grid=(b, DA_HEADS, nq),
        in_specs=[pl.BlockSpec(lam_vecs.shape, lambda bi, h, qi: (0, 0)),
                  pl.BlockSpec((None, tq, LANES), lambda bi, h, qi: (bi, qi, h)),
                  pl.BlockSpec((None, l_pad, LANES), lambda bi, h, qi: (bi, 0, h)),
                  pl.BlockSpec((None, l_pad, LANES), lambda bi, h, qi: (bi, 0, h)),
                  pl.BlockSpec((1, DA_V_DIM), lambda bi, h, qi: (0, 0))],
        out_specs=pl.BlockSpec((None, tq, LANES), lambda bi, h, qi: (bi, qi, h)),
        out_shape=jax.ShapeDtypeStruct((b, l_pad, DA_WIDTH), BF16),
        scratch_shapes=[pltpu.VMEM((2 * tq, LANES), BF16), pltpu.VMEM((l_pad, 2 * LANES), BF16),
                        pltpu.VMEM((2 * tq, LANES), F32), pltpu.VMEM((2 * tq, LANES), F32),
                        pltpu.VMEM((2 * tq, LANES), F32)],
        compiler_params=pltpu.CompilerParams(dimension_semantics=("arbitrary", "arbitrary", "arbitrary"),
                                             vmem_limit_bytes=VMEM_LIMIT_BYTES),
        name="diff_attention",
    )(lam_vecs, q, k, v, subln.reshape(1, DA_V_DIM))


def _expand_heads(x, lo_mask):
    zero = jnp.zeros_like(x)
    return jnp.concatenate([jnp.where(lo_mask, x, zero), jnp.where(lo_mask, zero, x)], axis=0)


def _unit_lower_inverse(a):
    n = a.shape[0]
    eye = (lax.broadcasted_iota(jnp.int32, (n, n), 0) == lax.broadcasted_iota(jnp.int32, (n, n), 1)).astype(F32)
    t = eye + a
    p = _dot_split(a, a)
    levels = int(math.log2(RW_CHUNK))
    for _ in range(levels - 2):
        pt = _dot_split(p, jnp.concatenate([p, t], axis=1))
        t = t + pt[:, n:]
        p = pt[:, :n]
    return t + _dot_split(p, t)


def _rwkv_kernel(u_ref, halo_ref, mix_ref, w2a2_ref, w0_ref, a0_ref, g2_ref, kk_ref, ka_ref, rk_ref,
                 lnw_ref, lnb_ref, ones_ref, tri_ref, o_ref, s_ref):
    i = pl.program_id(1)
    rows = u_ref.shape[0]
    c_len = RW_CHUNK
    w = RW_WIDTH

    @pl.when(i == 0)
    def _():
        s_ref[...] = jnp.zeros_like(s_ref)

    ones_bd = ones_ref[...]

    def shifted(lo, hi):
        u = u_ref[:, lo:hi]
        last_prev = jnp.where(i == 0, 0.0, halo_ref[7:8, lo:hi])
        rolled = pltpu.roll(u, 1, 0)
        row = lax.broadcasted_iota(jnp.int32, u.shape, 0)
        prev = jnp.where(row == 0, last_prev, rolled)
        return u + (prev - u) * mix_ref[:, lo:hi]

    r = shifted(0, w)
    k = shifted(w, 2 * w)
    v = shifted(2 * w, 3 * w)
    lora_in = shifted(3 * w, 3 * w + LANES)
    g_in = shifted(3 * w + LANES, RW_IN_PAD)

    lane = lax.broadcasted_iota(jnp.int32, lora_in.shape, 1)
    lora_act = jnp.where(lane < D_DECAY_LORA, jnp.tanh(lora_in), lora_in)
    wa = _dot_split(lora_act, w2a2_ref[...])
    w_pre = w0_ref[...] + wa[:, 0:w]
    lw = (-math.exp(-0.5)) * jax.nn.sigmoid(w_pre)
    a = jax.nn.sigmoid(a0_ref[...] + wa[:, w:2 * w])
    g = _dot(jax.nn.sigmoid(g_in).astype(BF16), g2_ref[...])

    kk_raw = k * kk_ref[...]
    kk_parts, bonus_parts = [], []
    k_mod = k * (1.0 + (a - 1.0) * ka_ref[...])
    rkr = r * k_mod * rk_ref[...]
    for p in range(w // LANES):
        sl = slice(LANES * p, LANES * (p + 1))
        x = kk_raw[:, sl]
        ss = _seg_sum(x * x, ones_bd)
        kk_parts.append(x * lax.rsqrt(jnp.maximum(ss, 1e-24)))
        bonus_parts.append(_seg_sum(rkr[:, sl], ones_bd) * v[:, sl])
    kk = jnp.concatenate(kk_parts, axis=1)
    bonus = jnp.concatenate(bonus_parts, axis=1)
    a_neg = -kk
    b_vec = kk * a

    n2 = 2 * c_len
    r_i = lax.broadcasted_iota(jnp.int32, (n2, n2), 0)
    c_i = lax.broadcasted_iota(jnp.int32, (n2, n2), 1)
    t_row = jnp.where(r_i >= c_len, r_i - c_len, r_i)
    t_col = jnp.where(c_i >= c_len, c_i - c_len, c_i)
    strict = t_col < t_row
    incl = t_col <= t_row
    lo_mask = lax.broadcasted_iota(jnp.int32, (c_len, LANES), 1) < RW_HEAD
    tri = tri_ref[...]

    y_chunks = []
    for c in range(rows // c_len):
        rs = slice(c * c_len, (c + 1) * c_len)
        lw_c = lw[rs]
        hi = lw_c.astype(BF16)
        mid_f = lw_c - hi.astype(F32)
        mid = mid_f.astype(BF16)
        lo = (mid_f - mid.astype(F32)).astype(BF16)
        cum = _dot(tri, hi) + (_dot(tri, mid) + _dot(tri, lo))
        cum_end = cum[c_len - 1:c_len, :]
        e_in = jnp.exp(cum)
        e_out = jnp.exp(-cum)
        e_end = jnp.exp(cum_end - cum)
        r_t = r[rs] * e_in
        a_t = a_neg[rs] * jnp.exp(cum - lw_c)
        k_t = k_mod[rs] * e_out
        b_t = b_vec[rs] * e_out
        k_e = k_mod[rs] * e_end
        b_e = b_vec[rs] * e_end
        g_end = jnp.exp(cum_end)
        v_c = v[rs]
        y_pairs = []
        for p in range(w // LANES):
            sl = slice(LANES * p, LANES * (p + 1))
            ar = jnp.concatenate([_expand_heads(a_t[:, sl], lo_mask), _expand_heads(r_t[:, sl], lo_mask)],
                                 axis=0).astype(BF16)
            bk = jnp.concatenate([_expand_heads(b_t[:, sl], lo_mask), _expand_heads(k_t[:, sl], lo_mask)],
                                 axis=0).astype(BF16)
            aa = _dot_nt(ar, bk)
            a_ab = jnp.where(strict, aa[0:n2, 0:n2], 0.0)
            a_ak = jnp.where(strict, aa[0:n2, n2:2 * n2], 0.0)
            a_rb = jnp.where(incl, aa[n2:2 * n2, 0:n2], 0.0)
            a_rk = jnp.where(incl, aa[n2:2 * n2, n2:2 * n2], 0.0)
            t_inv = _unit_lower_inverse(a_ab)
            v2 = _expand_heads(v_c[:, sl], lo_mask)
            state = s_ref[p]
            arh = _dot_nt(ar, state.astype(BF16))
            z = arh[0:n2] + _dot(a_ak.astype(BF16), v2.astype(BF16))
            u = _dot(t_inv.astype(BF16), z.astype(BF16))
            uv = jnp.concatenate([u, v2], axis=0).astype(BF16)
            y2 = arh[n2:2 * n2] + _dot(jnp.concatenate([a_rb, a_rk], axis=1).astype(BF16), uv)
            y_pairs.append(y2[0:c_len] + y2[c_len:n2])
            bke = jnp.concatenate([_expand_heads(b_e[:, sl], lo_mask), _expand_heads(k_e[:, sl], lo_mask)],
                                  axis=0).astype(BF16)
            s_ref[p] = state * g_end[:, sl] + _dot_tn(uv, bke)
        y_chunks.append(jnp.concatenate(y_pairs, axis=1))
    y = jnp.concatenate(y_chunks, axis=0)

    out_parts = []
    for p in range(w // LANES):
        sl = slice(LANES * p, LANES * (p + 1))
        yp = y[:, sl]
        mu = _seg_sum(yp, ones_bd) * (1.0 / RW_HEAD)
        d = yp - mu
        var = _seg_sum(d * d, ones_bd) * (1.0 / RW_HEAD)
        out_parts.append(d * lax.rsqrt(var + GN_EPS))
    yn = jnp.concatenate(out_parts, axis=1) * lnw_ref[...] + lnb_ref[...]
    o_ref[...] = ((yn + bonus) * g).astype(o_ref.dtype)


def _rwkv(u_rw, shift_mix_p, w2a2, w0, a0, g2_p, k_k, k_a, r_k, ln_w, ln_b, ones_bd, tri):
    b, l_pad, _ = u_rw.shape
    rows = RW_ROWS
    halo_per_tile = rows // 8
    vec = lambda x: x.reshape(1, -1).astype(F32)
    const = lambda bi, i: (0, 0)
    vec_spec = pl.BlockSpec((1, RW_WIDTH), const)
    return pl.pallas_call(
        _rwkv_kernel,
        grid=(b, l_pad // rows),
        in_specs=[pl.BlockSpec((None, rows, RW_IN_PAD), lambda bi, i: (bi, i, 0)),
                  pl.BlockSpec((None, 8, RW_IN_PAD), lambda bi, i: (bi, jnp.maximum(i * halo_per_tile - 1, 0), 0)),
                  pl.BlockSpec((1, RW_IN_PAD), const),
                  pl.BlockSpec(w2a2.shape, const),
                  vec_spec, vec_spec,
                  pl.BlockSpec(g2_p.shape, const),
                  vec_spec, vec_spec, vec_spec, vec_spec, vec_spec,
                  pl.BlockSpec((LANES, LANES), const),
                  pl.BlockSpec((RW_CHUNK, RW_CHUNK), const)],
        out_specs=pl.BlockSpec((None, rows, RW_WIDTH), lambda bi, i: (bi, i, 0)),
        out_shape=jax.ShapeDtypeStruct((b, l_pad, RW_WIDTH), BF16),
        scratch_shapes=[pltpu.VMEM((RW_WIDTH // LANES, LANES, LANES), F32)],
        compiler_params=pltpu.CompilerParams(dimension_semantics=("arbitrary", "arbitrary"),
                                             vmem_limit_bytes=VMEM_LIMIT_BYTES),
        name="rwkv7",
    )(u_rw, u_rw, vec(shift_mix_p), w2a2, vec(w0), vec(a0), g2_p, vec(k_k), vec(k_a), vec(r_k),
      vec(ln_w), vec(ln_b), ones_bd, tri)


def _pack_ffn(w_gate, w_up, w_down):
    d, ff = w_gate.shape
    n = ff // FF_CHUNK
    wg = w_gate.astype(BF16).reshape(d, n, FF_CHUNK).transpose(1, 0, 2)
    wu = w_up.astype(BF16).reshape(d, n, FF_CHUNK).transpose(1, 0, 2)
    wgu = jnp.concatenate([wg, wu], axis=-1)
    wd = w_down.astype(BF16).reshape(n, FF_CHUNK, d)
    return wgu, wd


def _rope_lane_tables(length):
    pos = jnp.arange(length, dtype=F32)
    inv_freq = ROPE_THETA ** (-jnp.arange(0, ROPE_DIM, 2, dtype=F32) / ROPE_DIM)
    ang = pos[:, None] * inv_freq[None, :]
    cos, sin = jnp.cos(ang), jnp.sin(ang)
    half = ROPE_DIM // 2
    rest = DA_QK_DIM - ROPE_DIM
    ones = jnp.ones((length, rest), F32)
    zeros = lambda n: jnp.zeros((length, n), F32)
    c = jnp.concatenate([cos, cos, ones], axis=1)
    s1 = jnp.concatenate([-sin, zeros(half + rest)], axis=1)
    s2 = jnp.concatenate([zeros(half), sin, zeros(rest)], axis=1)
    rep = LANES // DA_QK_DIM
    return jnp.tile(c, (1, rep)), jnp.tile(s1, (1, rep)), jnp.tile(s2, (1, rep))


def kernel(x, meta_tokens, ffn1_norm, ffn1_w_gate, ffn1_w_up, ffn1_w_down, mix_norm, w_in, da_q_norm, da_k_norm, da_lambda_q1, da_lambda_k1, da_lambda_q2, da_lambda_k2, da_subln, rw_shift_mix, rw_w0, rw_w2, rw_a0, rw_a2, rw_g2, rw_k_k, rw_k_a, rw_r_k, rw_ln_w, rw_ln_b, w_out, ffn2_norm, ffn2_w_gate, ffn2_w_up, ffn2_w_down):
    bsz, t, d = x.shape
    depth = w_in.shape[0]
    l = N_META + t
    l_pad = -(-l // Q_BLOCK) * Q_BLOCK
    assert d == D_MODEL and l_pad % ROW_TILE == 0 and l_pad % ATTN_TILE == 0 and l_pad % RW_ROWS == 0
    meta = jnp.broadcast_to(meta_tokens.astype(x.dtype)[None], (bsz, N_META, d))
    h = jnp.concatenate([meta, x], axis=1)
    h = jnp.pad(h, ((0, 0), (0, l_pad - l), (0, 0))).reshape(bsz * l_pad, d)

    rope_c, rope_s1, rope_s2 = _rope_lane_tables(l_pad)
    lane_head = jnp.arange(LANES) // RW_HEAD
    ones_bd = (lane_head[:, None] == lane_head[None, :]).astype(BF16)
    tri = (jnp.arange(RW_CHUNK)[:, None] >= jnp.arange(RW_CHUNK)[None, :]).astype(BF16)

    for layer in range(depth):
        lam_init = 0.8 - 0.6 * math.exp(-0.3 * layer)
        wgu1, wd1 = _pack_ffn(ffn1_w_gate[layer], ffn1_w_up[layer], ffn1_w_down[layer])
        wgu2, wd2 = _pack_ffn(ffn2_w_gate[layer], ffn2_w_up[layer], ffn2_w_down[layer])
        w_in_p = jnp.pad(w_in[layer], ((0, 0), (0, N_IN_PAD - w_in.shape[2]))).astype(BF16)
        shift_mix_p = jnp.pad(rw_shift_mix[layer], (0, RW_IN_PAD - RW_IN))
        zeros_lora = jnp.zeros((D_DECAY_LORA, RW_WIDTH), F32)
        w2a2 = jnp.concatenate([jnp.concatenate([rw_w2[layer], zeros_lora], axis=1),
                                jnp.concatenate([zeros_lora, rw_a2[layer]], axis=1)], axis=0)
        g2_p = jnp.pad(rw_g2[layer], ((0, RW_LORA_PAD - D_GATE_LORA), (0, 0))).astype(BF16)
        q_gain = jnp.tile(da_q_norm[layer], DA_QK_WIDTH // DA_QK_DIM).reshape(1, DA_QK_WIDTH)
        k_gain = jnp.tile(da_k_norm[layer], DA_QK_WIDTH // DA_QK_DIM).reshape(1, DA_QK_WIDTH)
        lam_vecs = jnp.stack([da_lambda_q1[layer], da_lambda_k1[layer],
                              da_lambda_q2[layer], da_lambda_k2[layer]]).astype(F32)

        h = _ffn(h, ffn1_norm[layer], wgu1, wd1)
        q, k, v, u_rw = _mix_in(h, mix_norm[layer], w_in_p, q_gain, k_gain,
                                rope_c, rope_s1, rope_s2, ones_bd, l_pad)
        shape3 = lambda a: a.reshape(bsz, l_pad, a.shape[-1])
        o_da = _attention(shape3(q), shape3(k), shape3(v), lam_vecs, da_subln[layer], lam_init)
        o_rw = _rwkv(shape3(u_rw), shift_mix_p, w2a2, rw_w0[layer], rw_a0[layer], g2_p,
                     rw_k_k[layer], rw_k_a[layer], rw_r_k[layer], rw_ln_w[layer], rw_ln_b[layer],
                     ones_bd, tri)
        h = _ffn(h, ffn2_norm[layer], wgu2, wd2,
                 mix=(o_da.reshape(bsz * l_pad, DA_WIDTH), o_rw.reshape(bsz * l_pad, RW_WIDTH),
                      w_out[layer].astype(BF16)))
    return h.reshape(bsz, l_pad, d)[:, N_META:l]
```

```python
import functools
import math

import jax
import jax.numpy as jnp
from jax import lax
from jax.experimental import pallas as pl
from jax.experimental.pallas import tpu as pltpu

F32 = jnp.float32
BF16 = jnp.bfloat16

D_MODEL = 1024
N_META = 16
Q_BLOCK = 128
ROPE_THETA = 500000.0
NORM_EPS = 1e-6
DA_HEADS = 4
DA_QK_DIM = 64
DA_V_DIM = 2 * DA_QK_DIM
DA_WIDTH = DA_HEADS * DA_V_DIM
DA_QK_WIDTH = DA_HEADS * 2 * DA_QK_DIM
ROPE_DIM = DA_QK_DIM // 4
RW_HEAD = 64
RW_WIDTH = D_MODEL - DA_WIDTH
RW_HEADS = RW_WIDTH // RW_HEAD
D_DECAY_LORA = 64
D_AAA_LORA = 64
D_GATE_LORA = 160
GN_EPS = 64e-5
DA_IN = 2 * DA_QK_WIDTH + DA_WIDTH
RW_IN = 3 * RW_WIDTH + D_DECAY_LORA + D_AAA_LORA + D_GATE_LORA
D_FF = 2816

LANES = 128
VMEM_LIMIT_BYTES = 56 * 1024 * 1024

ROW_TILE = 640
FF_CHUNK = 256
ATTN_TILE = 640
RW_CHUNK = 64
RW_ROWS = 2 * RW_CHUNK
RW_LORA_PAD = 2 * LANES
RW_IN_PAD = 3 * RW_WIDTH + LANES + RW_LORA_PAD
N_IN_PAD = DA_IN + RW_IN_PAD

_NT = (((1,), (1,)), ((), ()))
_TN = (((0,), (0,)), ((), ()))


def _dot(a, b):
    return jnp.dot(a, b, preferred_element_type=F32)


def _dot_nt(a, b):
    return lax.dot_general(a, b, _NT, preferred_element_type=F32)


def _dot_tn(a, b):
    return lax.dot_general(a, b, _TN, preferred_element_type=F32)


def _split2(x):
    hi = x.astype(BF16)
    lo = (x - hi.astype(F32)).astype(BF16)
    return hi, lo


def _dot_split(a, b):
    a_hi, a_lo = _split2(a)
    b_hi, b_lo = _split2(b)
    return _dot(a_hi, b_hi) + (_dot(a_hi, b_lo) + _dot(a_lo, b_hi))


def _seg_sum(x, ones_bd):
    hi, lo = _split2(x)
    return _dot(hi, ones_bd) + _dot(lo, ones_bd)


def _rms_rows(h, gain):
    ms = jnp.mean(h * h, axis=-1, keepdims=True)
    return h * lax.rsqrt(ms + NORM_EPS) * gain


def _ffn_kernel(*refs, has_mix):
    if has_mix:
        (h_ref, oda_ref, orw_ref, wout_ref, g_ref, wgu_ref, wd_ref, o_ref, xn_ref, acc_ref) = refs
        h = (h_ref[...] + _dot(oda_ref[...], wout_ref[0:DA_WIDTH, :])
             + _dot(orw_ref[...], wout_ref[DA_WIDTH:DA_WIDTH + RW_WIDTH, :]))
    else:
        (h_ref, g_ref, wgu_ref, wd_ref, o_ref, xn_ref, acc_ref) = refs
        h = h_ref[...]
    o_ref[...] = h
    xn_ref[...] = _rms_rows(h, g_ref[...]).astype(BF16)
    acc_ref[...] = jnp.zeros_like(acc_ref)

    def body(c, carry):
        gu = _dot(xn_ref[...], wgu_ref[c])
        g = gu[:, :FF_CHUNK]
        u = gu[:, FF_CHUNK:]
        act = (g * jax.nn.sigmoid(g)) * u
        acc_ref[...] += _dot(act.astype(BF16), wd_ref[c])
        return carry

    lax.fori_loop(0, wgu_ref.shape[0], body, 0)
    o_ref[...] = o_ref[...] + 0.5 * acc_ref[...]


def _ffn(h, norm_g, wgu, wd, mix=None):
    m, d = h.shape
    tm = ROW_TILE
    n_chunks = wgu.shape[0]
    row = lambda i: (i, 0)
    const2 = lambda i: (0, 0)
    const3 = lambda i: (0, 0, 0)
    in_specs = [pl.BlockSpec((tm, d), row)]
    args = [h]
    if mix is not None:
        o_da, o_rw, w_out = mix
        in_specs += [pl.BlockSpec((tm, DA_WIDTH), row), pl.BlockSpec((tm, RW_WIDTH), row),
                     pl.BlockSpec(w_out.shape, const2)]
        args += [o_da, o_rw, w_out]
    in_specs += [pl.BlockSpec((1, d), const2),
                 pl.BlockSpec(wgu.shape, const3, pipeline_mode=pl.Buffered(1)),
                 pl.BlockSpec(wd.shape, const3, pipeline_mode=pl.Buffered(1))]
    args += [norm_g.reshape(1, d), wgu, wd]
    return pl.pallas_call(
        functools.partial(_ffn_kernel, has_mix=mix is not None),
        grid=(m // tm,),
        in_specs=in_specs,
        out_specs=pl.BlockSpec((tm, d), row),
        out_shape=jax.ShapeDtypeStruct((m, d), F32),
        scratch_shapes=[pltpu.VMEM((tm, d), BF16), pltpu.VMEM((tm, d), F32)],
        compiler_params=pltpu.CompilerParams(dimension_semantics=("arbitrary",),
                                             vmem_limit_bytes=VMEM_LIMIT_BYTES),
        name="ffn_mix" if mix is not None else "ffn",
    )(*args)


def _qk_prep(u, gain_ref, c, s1, s2, ones_bd, scale, out_ref):
    for j in range(DA_QK_WIDTH // LANES):
        sl = slice(LANES * j, LANES * (j + 1))
        x = u[:, sl]
        ss = _seg_sum(x * x, ones_bd)
        xn = x * lax.rsqrt(ss * (1.0 / DA_QK_DIM) + NORM_EPS) * gain_ref[:, sl]
        half = ROPE_DIM // 2
        xr = xn * c + pltpu.roll(xn, LANES - half, 1) * s1 + pltpu.roll(xn, half, 1) * s2
        out_ref[:, sl] = (xr * scale).astype(BF16)


def _mixin_kernel(h_ref, g_ref, win_ref, qg_ref, kg_ref, c_ref, s1_ref, s2_ref, ones_ref,
                  q_ref, k_ref, v_ref, urw_ref, *, q_scale):
    xn = _rms_rows(h_ref[...], g_ref[...]).astype(BF16)
    qw = DA_QK_WIDTH
    uq = _dot(xn, win_ref[:, 0:qw])
    uk = _dot(xn, win_ref[:, qw:2 * qw])
    v_ref[...] = _dot(xn, win_ref[:, 2 * qw:DA_IN]).astype(BF16)
    urw_ref[...] = _dot(xn, win_ref[:, DA_IN:N_IN_PAD])
    c, s1, s2, ones_bd = c_ref[...], s1_ref[...], s2_ref[...], ones_ref[...]
    _qk_prep(uq, qg_ref, c, s1, s2, ones_bd, q_scale, q_ref)
    _qk_prep(uk, kg_ref, c, s1, s2, ones_bd, 1.0, k_ref)


def _mix_in(h, norm_g, w_in_p, q_gain, k_gain, rope_c, rope_s1, rope_s2, ones_bd, l_pad):
    m, d = h.shape
    tm = ROW_TILE
    tiles_per_seq = l_pad // tm
    row = lambda i: (i, 0)
    pos = lambda i: (i % tiles_per_seq, 0)
    const2 = lambda i: (0, 0)
    q_scale = DA_QK_DIM ** -0.5 * math.log2(math.e)
    return pl.pallas_call(
        functools.partial(_mixin_kernel, q_scale=q_scale),
        grid=(m // tm,),
        in_specs=[pl.BlockSpec((tm, d), row), pl.BlockSpec((1, d), const2),
                  pl.BlockSpec(w_in_p.shape, const2, pipeline_mode=pl.Buffered(1)),
                  pl.BlockSpec((1, DA_QK_WIDTH), const2), pl.BlockSpec((1, DA_QK_WIDTH), const2),
                  pl.BlockSpec((tm, LANES), pos), pl.BlockSpec((tm, LANES), pos),
                  pl.BlockSpec((tm, LANES), pos), pl.BlockSpec((LANES, LANES), const2)],
        out_specs=[pl.BlockSpec((tm, DA_QK_WIDTH), row), pl.BlockSpec((tm, DA_QK_WIDTH), row),
                   pl.BlockSpec((tm, DA_WIDTH), row), pl.BlockSpec((tm, RW_IN_PAD), row)],
        out_shape=[jax.ShapeDtypeStruct((m, DA_QK_WIDTH), BF16), jax.ShapeDtypeStruct((m, DA_QK_WIDTH), BF16),
                   jax.ShapeDtypeStruct((m, DA_WIDTH), BF16), jax.ShapeDtypeStruct((m, RW_IN_PAD), F32)],
        compiler_params=pltpu.CompilerParams(dimension_semantics=("arbitrary",),
                                             vmem_limit_bytes=VMEM_LIMIT_BYTES),
        name="mix_in",
    )(h, norm_g.reshape(1, d), w_in_p, q_gain, k_gain, rope_c, rope_s1, rope_s2, ones_bd)


def _attn_kernel(lam_ref, q_ref, k_ref, v_ref, sub_ref, o_ref, q2_s, vx_s, m_s, l_s, acc_s, sa_s, sb_s,
                 *, tq, lam_init):
    qi = pl.program_id(2)

    @pl.when(qi == 0)
    def _():
        vx_s[:, 0:LANES] = v_ref[...]
        vx_s[:, LANES:2 * LANES] = jnp.ones((vx_s.shape[0], LANES), BF16)

    q = q_ref[...]
    lane = lax.broadcasted_iota(jnp.int32, q.shape, 1)
    zero = jnp.zeros_like(q)
    q2_s[0:tq, :] = jnp.where(lane < DA_QK_DIM, q, zero)
    q2_s[tq:2 * tq, :] = jnp.where(lane < DA_QK_DIM, zero, q)
    m_s[...] = jnp.full(m_s.shape, -1e30, F32)
    l_s[...] = jnp.zeros_like(l_s)
    acc_s[...] = jnp.zeros_like(acc_s)

    def scores(j, buf):
        start = pl.multiple_of(j * tq, tq)
        buf[...] = _dot_nt(q2_s[...], k_ref[pl.ds(start, tq), :])

    def softmax_pv(j, buf, masked):
        start = pl.multiple_of(j * tq, tq)
        for half in range(2):
            rows = slice(half * tq, (half + 1) * tq)
            s = buf[rows, :]
            if masked:
                row = lax.broadcasted_iota(jnp.int32, s.shape, 0)
                col = lax.broadcasted_iota(jnp.int32, s.shape, 1)
                s = jnp.where(col <= row, s, -jnp.inf)
            m_prev = m_s[rows, :]
            m_new = jnp.maximum(m_prev, jnp.max(s, axis=1, keepdims=True))
            alpha = jnp.exp2(m_prev - m_new)
            p = jnp.exp2(s - jnp.concatenate([m_new] * (tq // LANES), axis=1))
            pv = _dot(p.astype(BF16), vx_s[pl.ds(start, tq), :])
            acc_s[rows, :] = acc_s[rows, :] * alpha + pv[:, 0:LANES]
            l_s[rows, :] = l_s[rows, :] * alpha + pv[:, LANES:2 * LANES]
            m_s[rows, :] = m_new

    scores(0, sa_s)

    def pair_body(t, carry):
        j = 2 * t
        scores(j + 1, sb_s)
        softmax_pv(j, sa_s, False)
        scores(j + 2, sa_s)
        softmax_pv(j + 1, sb_s, False)
        return carry

    lax.fori_loop(0, qi // 2, pair_body, 0)

    @pl.when(qi % 2 == 0)
    def _():
        softmax_pv(qi, sa_s, True)

    @pl.when(qi % 2 == 1)
    def _():
        scores(qi, sb_s)
        softmax_pv(qi - 1, sa_s, False)
        softmax_pv(qi, sb_s, True)

    lam1 = jnp.exp(jnp.sum(lam_ref[0:1, :] * lam_ref[1:2, :], axis=1, keepdims=True))
    lam2 = jnp.exp(jnp.sum(lam_ref[2:3, :] * lam_ref[3:4, :], axis=1, keepdims=True))
    lam = lam1 - lam2 + lam_init
    o = acc_s[0:tq, :] / l_s[0:tq, :] - lam * (acc_s[tq:2 * tq, :] / l_s[tq:2 * tq, :])
    o = _rms_rows(o, sub_ref[...]) * (1.0 - lam_init)
    o_ref[...] = o.astype(o_ref.dtype)


def _attention(q, k, v, lam_vecs, subln, lam_init):
    b, l_pad, _ = q.shape
    tq = ATTN_TILE
    nq = l_pad // tq
    return pl.pallas_call(
        functools.partial(_attn_kernel, tq=tq, lam_init=lam_init),
        grid=(b, DA_HEADS, nq),
        in_specs=[pl.BlockSpec(lam_vecs.shape, lambda bi, h, qi: (0, 0)),
                  pl.BlockSpec((None, tq, LANES), lambda bi, h, qi: (bi, qi, h)),
                  pl.BlockSpec((None, l_pad, LANES), lambda bi, h, qi: (bi, 0, h)),
                  pl.BlockSpec((None, l_pad, LANES), lambda bi, h, qi: (bi, 0, h)),
                  pl.BlockSpec((1, DA_V_DIM), lambda bi, h, qi: (0, 0))],
        out_specs=pl.BlockSpec((None, tq, LANES), lambda bi, h, qi: (bi, qi, h)),
        out_shape=jax.ShapeDtypeStruct((b, l_pad, DA_WIDTH), BF16),
        scratch_shapes=[pltpu.VMEM((2 * tq, LANES), BF16), pltpu.VMEM((l_pad, 2 * LANES), BF16),
                        pltpu.VMEM((2 * tq, LANES), F32), pltpu.VMEM((2 * tq, LANES), F32),
                        pltpu.VMEM((2 * tq, LANES), F32),
                        pltpu.VMEM((2 * tq, tq), F32), pltpu.VMEM((2 * tq, tq), F32)],
        compiler_params=pltpu.CompilerParams(dimension_semantics=("arbitrary", "arbitrary", "arbitrary"),
                                             vmem_limit_bytes=VMEM_LIMIT_BYTES),
        name="diff_attention",
    )(lam_vecs, q, k, v, subln.reshape(1, DA_V_DIM))


def _expand_heads(x, lo_mask):
    zero = jnp.zeros_like(x)
    return jnp.concatenate([jnp.where(lo_mask, x, zero), jnp.where(lo_mask, zero, x)], axis=0)


def _unit_lower_inverse(a_list):
    n = a_list[0].shape[0]
    eye = (lax.broadcasted_iota(jnp.int32, (n, n), 0) == lax.broadcasted_iota(jnp.int32, (n, n), 1)).astype(F32)
    t_list = [eye + a for a in a_list]
    p_list = [_dot_split(a, a) for a in a_list]
    levels = int(math.log2(RW_CHUNK))
    for _ in range(levels - 2):
        pt_list = [_dot_split(p, jnp.concatenate([p, t], axis=1)) for p, t in zip(p_list, t_list)]
        t_list = [t + pt[:, n:] for t, pt in zip(t_list, pt_list)]
        p_list = [pt[:, :n] for pt in pt_list]
    return [t + _dot_split(p, t) for p, t in zip(p_list, t_list)]


def _rwkv_kernel(u_ref, halo_ref, mix_ref, w2a2_ref, w0_ref, a0_ref, g2_ref, kk_ref, ka_ref, rk_ref,
                 lnw_ref, lnb_ref, ones_ref, tri_ref, o_ref, s_ref):
    i = pl.program_id(1)
    rows = u_ref.shape[0]
    c_len = RW_CHUNK
    w = RW_WIDTH

    @pl.when(i == 0)
    def _():
        s_ref[...] = jnp.zeros_like(s_ref)

    ones_bd = ones_ref[...]

    def shifted(lo, hi):
        u = u_ref[:, lo:hi]
        last_prev = jnp.where(i == 0, 0.0, halo_ref[7:8, lo:hi])
        rolled = pltpu.roll(u, 1, 0)
        row = lax.broadcasted_iota(jnp.int32, u.shape, 0)
        prev = jnp.where(row == 0, last_prev, rolled)
        return u + (prev - u) * mix_ref[:, lo:hi]

    r = shifted(0, w)
    k = shifted(w, 2 * w)
    v = shifted(2 * w, 3 * w)
    lora_in = shifted(3 * w, 3 * w + LANES)
    g_in = shifted(3 * w + LANES, RW_IN_PAD)

    lane = lax.broadcasted_iota(jnp.int32, lora_in.shape, 1)
    lora_act = jnp.where(lane < D_DECAY_LORA, jnp.tanh(lora_in), lora_in)
    wa = _dot_split(lora_act, w2a2_ref[...])
    w_pre = w0_ref[...] + wa[:, 0:w]
    lw = (-math.exp(-0.5)) * jax.nn.sigmoid(w_pre)
    a = jax.nn.sigmoid(a0_ref[...] + wa[:, w:2 * w])
    g = _dot(jax.nn.sigmoid(g_in).astype(BF16), g2_ref[...])

    kk_raw = k * kk_ref[...]
    kk_parts, bonus_parts = [], []
    k_mod = k * (1.0 + (a - 1.0) * ka_ref[...])
    rkr = r * k_mod * rk_ref[...]
    for p in range(w // LANES):
        sl = slice(LANES * p, LANES * (p + 1))
        x = kk_raw[:, sl]
        ss = _seg_sum(x * x, ones_bd)
        kk_parts.append(x * lax.rsqrt(jnp.maximum(ss, 1e-24)))
        bonus_parts.append(_seg_sum(rkr[:, sl], ones_bd) * v[:, sl])
    kk = jnp.concatenate(kk_parts, axis=1)
    bonus = jnp.concatenate(bonus_parts, axis=1)
    a_neg = -kk
    b_vec = kk * a

    n2 = 2 * c_len
    r_i = lax.broadcasted_iota(jnp.int32, (n2, n2), 0)
    c_i = lax.broadcasted_iota(jnp.int32, (n2, n2), 1)
    t_row = jnp.where(r_i >= c_len, r_i - c_len, r_i)
    t_col = jnp.where(c_i >= c_len, c_i - c_len, c_i)
    strict = t_col < t_row
    incl = t_col <= t_row
    lo_mask = lax.broadcasted_iota(jnp.int32, (c_len, LANES), 1) < RW_HEAD
    tri = tri_ref[...]

    n_chunks = rows // c_len
    n_pairs = w // LANES
    ar_l, bk_l, bke_l, v2_l, g_end_l = [], [], [], [], []
    for c in range(n_chunks):
        rs = slice(c * c_len, (c + 1) * c_len)
        lw_c = lw[rs]
        hi = lw_c.astype(BF16)
        mid_f = lw_c - hi.astype(F32)
        mid = mid_f.astype(BF16)
        lo = (mid_f - mid.astype(F32)).astype(BF16)
        cum = _dot(tri, hi) + (_dot(tri, mid) + _dot(tri, lo))
        cum_end = cum[c_len - 1:c_len, :]
        e_in = jnp.exp(cum)
        e_out = jnp.exp(-cum)
        e_end = jnp.exp(cum_end - cum)
        r_t = r[rs] * e_in
        a_t = a_neg[rs] * jnp.exp(cum - lw_c)
        k_t = k_mod[rs] * e_out
        b_t = b_vec[rs] * e_out
        k_e = k_mod[rs] * e_end
        b_e = b_vec[rs] * e_end
        g_end_l.append(jnp.exp(cum_end))
        v_c = v[rs]
        for p in range(n_pairs):
            sl = slice(LANES * p, LANES * (p + 1))
            ar_l.append(jnp.concatenate([_expand_heads(a_t[:, sl], lo_mask), _expand_heads(r_t[:, sl], lo_mask)],
                                        axis=0).astype(BF16))
            bk_l.append(jnp.concatenate([_expand_heads(b_t[:, sl], lo_mask), _expand_heads(k_t[:, sl], lo_mask)],
                                        axis=0).astype(BF16))
            bke_l.append(jnp.concatenate([_expand_heads(b_e[:, sl], lo_mask), _expand_heads(k_e[:, sl], lo_mask)],
                                         axis=0).astype(BF16))
            v2_l.append(_expand_heads(v_c[:, sl], lo_mask).astype(BF16))
    aa_l = [_dot_nt(ar, bk) for ar, bk in zip(ar_l, bk_l)]
    t_inv_l = _unit_lower_inverse([jnp.where(strict, aa[0:n2, 0:n2], 0.0) for aa in aa_l])
    akv_l = [_dot(jnp.where(strict, aa[0:n2, n2:2 * n2], 0.0).astype(BF16), v2) for aa, v2 in zip(aa_l, v2_l)]
    arbk_l = [jnp.concatenate([jnp.where(incl, aa[n2:2 * n2, 0:n2], 0.0),
                               jnp.where(incl, aa[n2:2 * n2, n2:2 * n2], 0.0)], axis=1).astype(BF16)
              for aa in aa_l]
    t_inv_l = [t.astype(BF16) for t in t_inv_l]

    y_chunks = []
    for c in range(n_chunks):
        idx = [c * n_pairs + p for p in range(n_pairs)]
        states = [s_ref[p] for p in range(n_pairs)]
        arh = [_dot_nt(ar_l[j], s.astype(BF16)) for j, s in zip(idx, states)]
        u_l = [_dot(t_inv_l[j], (h[0:n2] + akv_l[j]).astype(BF16)) for j, h in zip(idx, arh)]
        uv = [jnp.concatenate([u.astype(BF16), v2_l[j]], axis=0) for j, u in zip(idx, u_l)]
        y2 = [h[n2:2 * n2] + _dot(arbk_l[j], x) for j, h, x in zip(idx, arh, uv)]
        for p in range(n_pairs):
            sl = slice(LANES * p, LANES * (p + 1))
            s_ref[p] = states[p] * g_end_l[c][:, sl] + _dot_tn(uv[p], bke_l[idx[p]])
        y_chunks.append(jnp.concatenate([y[0:c_len] + y[c_len:n2] for y in y2], axis=1))
    y = jnp.concatenate(y_chunks, axis=0)

    out_parts = []
    for p in range(w // LANES):
        sl = slice(LANES * p, LANES * (p + 1))
        yp = y[:, sl]
        mu = _seg_sum(yp, ones_bd) * (1.0 / RW_HEAD)
        d = yp - mu
        var = _seg_sum(d * d, ones_bd) * (1.0 / RW_HEAD)
        out_parts.append(d * lax.rsqrt(var + GN_EPS))
    yn = jnp.concatenate(out_parts, axis=1) * lnw_ref[...] + lnb_ref[...]
    o_ref[...] = ((yn + bonus) * g).astype(o_ref.dtype)


def _rwkv(u_rw, shift_mix_p, w2a2, w0, a0, g2_p, k_k, k_a, r_k, ln_w, ln_b, ones_bd, tri):
    b, l_pad, _ = u_rw.shape
    rows = RW_ROWS
    halo_per_tile = rows // 8
    vec = lambda x: x.reshape(1, -1).astype(F32)
    const = lambda bi, i: (0, 0)
    vec_spec = pl.BlockSpec((1, RW_WIDTH), const)
    return pl.pallas_call(
        _rwkv_kernel,
        grid=(b, l_pad // rows),
        in_specs=[pl.BlockSpec((None, rows, RW_IN_PAD), lambda bi, i: (bi, i, 0)),
                  pl.BlockSpec((None, 8, RW_IN_PAD), lambda bi, i: (bi, jnp.maximum(i * halo_per_tile - 1, 0), 0)),
                  pl.BlockSpec((1, RW_IN_PAD), const),
                  pl.BlockSpec(w2a2.shape, const),
                  vec_spec, vec_spec,
                  pl.BlockSpec(g2_p.shape, const),
                  vec_spec, vec_spec, vec_spec, vec_spec, vec_spec,
                  pl.BlockSpec((LANES, LANES), const),
                  pl.BlockSpec((RW_CHUNK, RW_CHUNK), const)],
        out_specs=pl.BlockSpec((None, rows, RW_WIDTH), lambda bi, i: (bi, i, 0)),
        out_shape=jax.ShapeDtypeStruct((b, l_pad, RW_WIDTH), BF16),
        scratch_shapes=[pltpu.VMEM((RW_WIDTH // LANES, LANES, LANES), F32)],
        compiler_params=pltpu.CompilerParams(dimension_semantics=("arbitrary", "arbitrary"),
                                             vmem_limit_bytes=VMEM_LIMIT_BYTES),
        name="rwkv7",
    )(u_rw, u_rw, vec(shift_mix_p), w2a2, vec(w0), vec(a0), g2_p, vec(k_k), vec(k_a), vec(r_k),
      vec(ln_w), vec(ln_b), ones_bd, tri)


def _pack_ffn(w_gate, w_up, w_down):
    d, ff = w_gate.shape
    n = ff // FF_CHUNK
    wg = w_gate.astype(BF16).reshape(d, n, FF_CHUNK).transpose(1, 0, 2)
    wu = w_up.astype(BF16).reshape(d, n, FF_CHUNK).transpose(1, 0, 2)
    wgu = jnp.concatenate([wg, wu], axis=-1)
    wd = w_down.astype(BF16).reshape(n, FF_CHUNK, d)
    return wgu, wd


def _rope_lane_tables(length):
    pos = jnp.arange(length, dtype=F32)
    inv_freq = ROPE_THETA ** (-jnp.arange(0, ROPE_DIM, 2, dtype=F32) / ROPE_DIM)
    ang = pos[:, None] * inv_freq[None, :]
    cos, sin = jnp.cos(ang), jnp.sin(ang)
    half = ROPE_DIM // 2
    rest = DA_QK_DIM - ROPE_DIM
    ones = jnp.ones((length, rest), F32)
    zeros = lambda n: jnp.zeros((length, n), F32)
    c = jnp.concatenate([cos, cos, ones], axis=1)
    s1 = jnp.concatenate([-sin, zeros(half + rest)], axis=1)
    s2 = jnp.concatenate([zeros(half), sin, zeros(rest)], axis=1)
    rep = LANES // DA_QK_DIM
    return jnp.tile(c, (1, rep)), jnp.tile(s1, (1, rep)), jnp.tile(s2, (1, rep))


def kernel(x, meta_tokens, ffn1_norm, ffn1_w_gate, ffn1_w_up, ffn1_w_down, mix_norm, w_in, da_q_norm, da_k_norm, da_lambda_q1, da_lambda_k1, da_lambda_q2, da_lambda_k2, da_subln, rw_shift_mix, rw_w0, rw_w2, rw_a0, rw_a2, rw_g2, rw_k_k, rw_k_a, rw_r_k, rw_ln_w, rw_ln_b, w_out, ffn2_norm, ffn2_w_gate, ffn2_w_up, ffn2_w_down):
    bsz, t, d = x.shape
    depth = w_in.shape[0]
    l = N_META + t
    l_pad = -(-l // Q_BLOCK) * Q_BLOCK
    assert d == D_MODEL and l_pad % ROW_TILE == 0 and l_pad % ATTN_TILE == 0 and l_pad % RW_ROWS == 0
    meta = jnp.broadcast_to(meta_tokens.astype(x.dtype)[None], (bsz, N_META, d))
    h = jnp.concatenate([meta, x], axis=1)
    h = jnp.pad(h, ((0, 0), (0, l_pad - l), (0, 0))).reshape(bsz * l_pad, d)

    rope_c, rope_s1, rope_s2 = _rope_lane_tables(l_pad)
    lane_head = jnp.arange(LANES) // RW_HEAD
    ones_bd = (lane_head[:, None] == lane_head[None, :]).astype(BF16)
    tri = (jnp.arange(RW_CHUNK)[:, None] >= jnp.arange(RW_CHUNK)[None, :]).astype(BF16)

    for layer in range(depth):
        lam_init = 0.8 - 0.6 * math.exp(-0.3 * layer)
        wgu1, wd1 = _pack_ffn(ffn1_w_gate[layer], ffn1_w_up[layer], ffn1_w_down[layer])
        wgu2, wd2 = _pack_ffn(ffn2_w_gate[layer], ffn2_w_up[layer], ffn2_w_down[layer])
        w_in_p = jnp.pad(w_in[layer], ((0, 0), (0, N_IN_PAD - w_in.shape[2]))).astype(BF16)
        shift_mix_p = jnp.pad(rw_shift_mix[layer], (0, RW_IN_PAD - RW_IN))
        zeros_lora = jnp.zeros((D_DECAY_LORA, RW_WIDTH), F32)
        w2a2 = jnp.concatenate([jnp.concatenate([rw_w2[layer], zeros_lora], axis=1),
                                jnp.concatenate([zeros_lora, rw_a2[layer]], axis=1)], axis=0)
        g2_p = jnp.pad(rw_g2[layer], ((0, RW_LORA_PAD - D_GATE_LORA), (0, 0))).astype(BF16)
        q_gain = jnp.tile(da_q_norm[layer], DA_QK_WIDTH // DA_QK_DIM).reshape(1, DA_QK_WIDTH)
        k_gain = jnp.tile(da_k_norm[layer], DA_QK_WIDTH // DA_QK_DIM).reshape(1, DA_QK_WIDTH)
        lam_vecs = jnp.stack([da_lambda_q1[layer], da_lambda_k1[layer],
                              da_lambda_q2[layer], da_lambda_k2[layer]]).astype(F32)

        h = _ffn(h, ffn1_norm[layer], wgu1, wd1)
        q, k, v, u_rw = _mix_in(h, mix_norm[layer], w_in_p, q_gain, k_gain,
                                rope_c, rope_s1, rope_s2, ones_bd, l_pad)
        shape3 = lambda a: a.reshape(bsz, l_pad, a.shape[-1])
        o_da = _attention(shape3(q), shape3(k), shape3(v), lam_vecs, da_subln[layer], lam_init)
        o_rw = _rwkv(shape3(u_rw), shift_mix_p, w2a2, rw_w0[layer], rw_a0[layer], g2_p,
                     rw_k_k[layer], rw_k_a[layer], rw_r_k[layer], rw_ln_w[layer], rw_ln_b[layer],
                     ones_bd, tri)
        h = _ffn(h, ffn2_norm[layer], wgu2, wd2,
                 mix=(o_da.reshape(bsz * l_pad, DA_WIDTH), o_rw.reshape(bsz * l_pad, RW_WIDTH),
                      w_out[layer].astype(BF16)))
    return h.reshape(bsz, l_pad, d)[:, N_META:l]
```

```python
import functools
import math

import jax
import jax.numpy as jnp
from jax import lax
from jax.experimental import pallas as pl
from jax.experimental.pallas import tpu as pltpu

F32 = jnp.float32
BF16 = jnp.bfloat16

D_MODEL = 1024
N_META = 16
Q_BLOCK = 128
ROPE_THETA = 500000.0
NORM_EPS = 1e-6
DA_HEADS = 4
DA_QK_DIM = 64
DA_V_DIM = 2 * DA_QK_DIM
DA_WIDTH = DA_HEADS * DA_V_DIM
DA_QK_WIDTH = DA_HEADS * 2 * DA_QK_DIM
ROPE_DIM = DA_QK_DIM // 4
RW_HEAD = 64
RW_WIDTH = D_MODEL - DA_WIDTH
RW_HEADS = RW_WIDTH // RW_HEAD
D_DECAY_LORA = 64
D_AAA_LORA = 64
D_GATE_LORA = 160
GN_EPS = 64e-5
DA_IN = 2 * DA_QK_WIDTH + DA_WIDTH
RW_IN = 3 * RW_WIDTH + D_DECAY_LORA + D_AAA_LORA + D_GATE_LORA
D_FF = 2816

LANES = 128
VMEM_LIMIT_BYTES = 56 * 1024 * 1024

ROW_TILE = 640
FF_CHUNK = 256
ATTN_TILE = 640
ATTN_ROW_BLOCKS = 2
RW_CHUNK = 64
RW_ROWS = 2 * RW_CHUNK
RW_LORA_PAD = 2 * LANES
RW_IN_PAD = 3 * RW_WIDTH + LANES + RW_LORA_PAD
N_IN_PAD = DA_IN + RW_IN_PAD

_NT = (((1,), (1,)), ((), ()))
_TN = (((0,), (0,)), ((), ()))


def _dot(a, b):
    return jnp.dot(a, b, preferred_element_type=F32)


def _dot_nt(a, b):
    return lax.dot_general(a, b, _NT, preferred_element_type=F32)


def _dot_tn(a, b):
    return lax.dot_general(a, b, _TN, preferred_element_type=F32)


def _split2(x):
    hi = x.astype(BF16)
    lo = (x - hi.astype(F32)).astype(BF16)
    return hi, lo


def _dot_split(a, b):
    a_hi, a_lo = _split2(a)
    b_hi, b_lo = _split2(b)
    return _dot(a_hi, b_hi) + (_dot(a_hi, b_lo) + _dot(a_lo, b_hi))


def _seg_sum(x, ones_bd):
    hi, lo = _split2(x)
    return _dot(hi, ones_bd) + _dot(lo, ones_bd)


def _rms_rows(h, gain):
    ms = jnp.mean(h * h, axis=-1, keepdims=True)
    return h * lax.rsqrt(ms + NORM_EPS) * gain


def _ffn_kernel(*refs, has_mix):
    if has_mix:
        (h_ref, oda_ref, orw_ref, wout_ref, g_ref, wg_ref, wu_ref, wd_ref, o_ref, xn_ref, act_ref) = refs
        h = (h_ref[...] + _dot(oda_ref[...], wout_ref[0:DA_WIDTH, :])
             + _dot(orw_ref[...], wout_ref[DA_WIDTH:DA_WIDTH + RW_WIDTH, :]))
        o_ref[...] = h
        res_ref = o_ref
    else:
        (h_ref, g_ref, wg_ref, wu_ref, wd_ref, o_ref, xn_ref, act_ref) = refs
        h = h_ref[...]
        res_ref = h_ref
    xn_ref[...] = _rms_rows(h, g_ref[...]).astype(BF16)
    d_ff = wg_ref.shape[1]
    for c in range(d_ff // FF_CHUNK):
        cols = slice(c * FF_CHUNK, (c + 1) * FF_CHUNK)
        g = _dot(xn_ref[...], wg_ref[:, cols])
        u = _dot(xn_ref[...], wu_ref[:, cols])
        act_ref[:, cols] = ((g * jax.nn.sigmoid(g)) * u).astype(BF16)
    o_ref[...] = res_ref[...] + 0.5 * _dot(act_ref[...], wd_ref[...])


def _ffn(h, norm_g, wg, wu, wd, mix=None):
    m, d = h.shape
    tm = ROW_TILE
    d_ff = wg.shape[1]
    row = lambda i: (i, 0)
    const2 = lambda i: (0, 0)
    in_specs = [pl.BlockSpec((tm, d), row)]
    args = [h]
    if mix is not None:
        o_da, o_rw, w_out = mix
        in_specs += [pl.BlockSpec((tm, DA_WIDTH), row), pl.BlockSpec((tm, RW_WIDTH), row),
                     pl.BlockSpec(w_out.shape, const2)]
        args += [o_da, o_rw, w_out]
    in_specs += [pl.BlockSpec((1, d), const2),
                 pl.BlockSpec(wg.shape, const2, pipeline_mode=pl.Buffered(1)),
                 pl.BlockSpec(wu.shape, const2, pipeline_mode=pl.Buffered(1)),
                 pl.BlockSpec(wd.shape, const2, pipeline_mode=pl.Buffered(1))]
    args += [norm_g.reshape(1, d), wg, wu, wd]
    return pl.pallas_call(
        functools.partial(_ffn_kernel, has_mix=mix is not None),
        grid=(m // tm,),
        in_specs=in_specs,
        out_specs=pl.BlockSpec((tm, d), row),
        out_shape=jax.ShapeDtypeStruct((m, d), F32),
        scratch_shapes=[pltpu.VMEM((tm, d), BF16), pltpu.VMEM((tm, d_ff), BF16)],
        compiler_params=pltpu.CompilerParams(dimension_semantics=("arbitrary",),
                                             vmem_limit_bytes=VMEM_LIMIT_BYTES),
        name="ffn_mix" if mix is not None else "ffn",
    )(*args)


def _qk_prep(u, gain_ref, c, s1, s2, ones_bd, scale, out_ref):
    for j in range(DA_QK_WIDTH // LANES):
        sl = slice(LANES * j, LANES * (j + 1))
        x = u[:, sl]
        ss = _seg_sum(x * x, ones_bd)
        xn = x * lax.rsqrt(ss * (1.0 / DA_QK_DIM) + NORM_EPS) * gain_ref[:, sl]
        half = ROPE_DIM // 2
        xr = xn * c + pltpu.roll(xn, LANES - half, 1) * s1 + pltpu.roll(xn, half, 1) * s2
        out_ref[:, sl] = (xr * scale).astype(BF16)


def _mixin_kernel(h_ref, g_ref, win_ref, qg_ref, kg_ref, c_ref, s1_ref, s2_ref, ones_ref,
                  q_ref, k_ref, v_ref, urw_ref, *, q_scale):
    xn = _rms_rows(h_ref[...], g_ref[...]).astype(BF16)
    qw = DA_QK_WIDTH
    uq = _dot(xn, win_ref[:, 0:qw])
    uk = _dot(xn, win_ref[:, qw:2 * qw])
    v_ref[...] = _dot(xn, win_ref[:, 2 * qw:DA_IN]).astype(BF16)
    urw_ref[...] = _dot(xn, win_ref[:, DA_IN:N_IN_PAD])
    c, s1, s2, ones_bd = c_ref[...], s1_ref[...], s2_ref[...], ones_ref[...]
    _qk_prep(uq, qg_ref, c, s1, s2, ones_bd, q_scale, q_ref)
    _qk_prep(uk, kg_ref, c, s1, s2, ones_bd, 1.0, k_ref)


def _mix_in(h, norm_g, w_in_p, q_gain, k_gain, rope_c, rope_s1, rope_s2, ones_bd, l_pad):
    m, d = h.shape
    tm = ROW_TILE
    tiles_per_seq = l_pad // tm
    row = lambda i: (i, 0)
    pos = lambda i: (i % tiles_per_seq, 0)
    const2 = lambda i: (0, 0)
    q_scale = DA_QK_DIM ** -0.5 * math.log2(math.e)
    return pl.pallas_call(
        functools.partial(_mixin_kernel, q_scale=q_scale),
        grid=(m // tm,),
        in_specs=[pl.BlockSpec((tm, d), row), pl.BlockSpec((1, d), const2),
                  pl.BlockSpec(w_in_p.shape, const2, pipeline_mode=pl.Buffered(1)),
                  pl.BlockSpec((1, DA_QK_WIDTH), const2), pl.BlockSpec((1, DA_QK_WIDTH), const2),
                  pl.BlockSpec((tm, LANES), pos), pl.BlockSpec((tm, LANES), pos),
                  pl.BlockSpec((tm, LANES), pos), pl.BlockSpec((LANES, LANES), const2)],
        out_specs=[pl.BlockSpec((tm, DA_QK_WIDTH), row), pl.BlockSpec((tm, DA_QK_WIDTH), row),
                   pl.BlockSpec((tm, DA_WIDTH), row), pl.BlockSpec((tm, RW_IN_PAD), row)],
        out_shape=[jax.ShapeDtypeStruct((m, DA_QK_WIDTH), BF16), jax.ShapeDtypeStruct((m, DA_QK_WIDTH), BF16),
                   jax.ShapeDtypeStruct((m, DA_WIDTH), BF16), jax.ShapeDtypeStruct((m, RW_IN_PAD), F32)],
        compiler_params=pltpu.CompilerParams(dimension_semantics=("arbitrary",),
                                             vmem_limit_bytes=VMEM_LIMIT_BYTES),
        name="mix_in",
    )(h, norm_g.reshape(1, d), w_in_p, q_gain, k_gain, rope_c, rope_s1, rope_s2, ones_bd)


def _attn_kernel(lam_ref, q_ref, k_ref, v_ref, sub_ref, o_ref, q2_s, vx_s, m_s, l_s, acc_s, s_s, *, tq, lam_init):
    qi = pl.program_id(2)

    @pl.when(qi == 0)
    def _():
        vx_s[:, 0:LANES] = v_ref[...]
        vx_s[:, LANES:2 * LANES] = jnp.ones((vx_s.shape[0], LANES), BF16)

    q = q_ref[...]
    lane = lax.broadcasted_iota(jnp.int32, q.shape, 1)
    zero = jnp.zeros_like(q)
    q2_s[0:tq, :] = jnp.where(lane < DA_QK_DIM, q, zero)
    q2_s[tq:2 * tq, :] = jnp.where(lane < DA_QK_DIM, zero, q)
    m_s[...] = jnp.full(m_s.shape, -1e30, F32)
    l_s[...] = jnp.zeros_like(l_s)
    acc_s[...] = jnp.zeros_like(acc_s)

    def process(start, width, diag_offset):
        rb = (2 * tq) // ATTN_ROW_BLOCKS
        blocks = [slice(i * rb, (i + 1) * rb) for i in range(ATTN_ROW_BLOCKS)]
        for rows in blocks:
            s_s[rows, 0:width] = _dot_nt(q2_s[rows, :], k_ref[pl.ds(start, width), :])
        for i, rows in enumerate(blocks):
            s = s_s[rows, 0:width]
            if diag_offset is not None:
                row = lax.broadcasted_iota(jnp.int32, s.shape, 0) + (i * rb) % tq
                col = lax.broadcasted_iota(jnp.int32, s.shape, 1)
                s = jnp.where(col <= row + diag_offset, s, -jnp.inf)
            m_prev = m_s[rows, :]
            m_new = jnp.maximum(m_prev, jnp.max(s, axis=1, keepdims=True))
            alpha = jnp.exp2(m_prev - m_new)
            p = jnp.exp2(s - jnp.concatenate([m_new] * (width // LANES), axis=1))
            pv = _dot(p.astype(BF16), vx_s[pl.ds(start, width), :])
            acc_s[rows, :] = acc_s[rows, :] * alpha + pv[:, 0:LANES]
            l_s[rows, :] = l_s[rows, :] * alpha + pv[:, LANES:2 * LANES]
            m_s[rows, :] = m_new

    def wide_body(t, carry):
        process(pl.multiple_of(t * (2 * tq), 2 * tq), 2 * tq, None)
        return carry

    lax.fori_loop(0, qi // 2, wide_body, 0)

    @pl.when(qi % 2 == 0)
    def _():
        process(pl.multiple_of(qi * tq, tq), tq, 0)

    @pl.when(qi % 2 == 1)
    def _():
        process(pl.multiple_of((qi - 1) * tq, tq), 2 * tq, tq)

    lam1 = jnp.exp(jnp.sum(lam_ref[0:1, :] * lam_ref[1:2, :], axis=1, keepdims=True))
    lam2 = jnp.exp(jnp.sum(lam_ref[2:3, :] * lam_ref[3:4, :], axis=1, keepdims=True))
    lam = lam1 - lam2 + lam_init
    o = acc_s[0:tq, :] / l_s[0:tq, :] - lam * (acc_s[tq:2 * tq, :] / l_s[tq:2 * tq, :])
    o = _rms_rows(o, sub_ref[...]) * (1.0 - lam_init)
    o_ref[...] = o.astype(o_ref.dtype)


def _attention(q, k, v, lam_vecs, subln, lam_init):
    b, l_pad, _ = q.shape
    tq = ATTN_TILE
    nq = l_pad // tq
    return pl.pallas_call(
        functools.partial(_attn_kernel, tq=tq, lam_init=lam_init),
        grid=(b, DA_HEADS, nq),
        in_specs=[pl.BlockSpec(lam_vecs.shape, lambda bi, h, qi: (0, 0)),
                  pl.BlockSpec((None, tq, LANES), lambda bi, h, qi: (bi, qi, h)),
                  pl.BlockSpec((None, l_pad, LANES), lambda bi, h, qi: (bi, 0, h)),
                  pl.BlockSpec((None, l_pad, LANES), lambda bi, h, qi: (bi, 0, h)),
                  pl.BlockSpec((1, DA_V_DIM), lambda bi, h, qi: (0, 0))],
        out_specs=pl.BlockSpec((None, tq, LANES), lambda bi, h, qi: (bi, qi, h)),
        out_shape=jax.ShapeDtypeStruct((b, l_pad, DA_WIDTH), BF16),
        scratch_shapes=[pltpu.VMEM((2 * tq, LANES), BF16), pltpu.VMEM((l_pad, 2 * LANES), BF16),
                        pltpu.VMEM((2 * tq, LANES), F32), pltpu.VMEM((2 * tq, LANES), F32),
                        pltpu.VMEM((2 * tq, LANES), F32),
                        pltpu.VMEM((2 * tq, 2 * tq), F32)],
        compiler_params=pltpu.CompilerParams(dimension_semantics=("arbitrary", "arbitrary", "arbitrary"),
                                             vmem_limit_bytes=VMEM_LIMIT_BYTES),
        name="diff_attention",
    )(lam_vecs, q, k, v, subln.reshape(1, DA_V_DIM))


def _expand_heads(x, lo_mask):
    zero = jnp.zeros_like(x)
    return jnp.concatenate([jnp.where(lo_mask, x, zero), jnp.where(lo_mask, zero, x)], axis=0)


def _unit_lower_inverse(a_list):
    n = a_list[0].shape[0]
    eye = (lax.broadcasted_iota(jnp.int32, (n, n), 0) == lax.broadcasted_iota(jnp.int32, (n, n), 1)).astype(F32)
    mm = lambda x, y: _dot(x.astype(BF16), y.astype(BF16))
    t_list = [eye + a for a in a_list]
    p_list = [mm(a, a) for a in a_list]
    levels = int(math.log2(RW_CHUNK))
    for _ in range(levels - 2):
        pt_list = [mm(p, jnp.concatenate([p, t], axis=1)) for p, t in zip(p_list, t_list)]
        t_list = [t + pt[:, n:] for t, pt in zip(t_list, pt_list)]
        p_list = [pt[:, :n] for pt in pt_list]
    return [t + mm(p, t) for p, t in zip(p_list, t_list)]


def _rwkv_kernel(u_ref, halo_ref, mix_ref, w2a2_ref, w0_ref, a0_ref, g2_ref, kk_ref, ka_ref, rk_ref,
                 lnw_ref, lnb_ref, ones_ref, tri_ref, o_ref, s_ref):
    i = pl.program_id(1)
    rows = u_ref.shape[0]
    c_len = RW_CHUNK
    w = RW_WIDTH

    @pl.when(i == 0)
    def _():
        s_ref[...] = jnp.zeros_like(s_ref)

    ones_bd = ones_ref[...]

    def shifted(lo, hi):
        u = u_ref[:, lo:hi]
        last_prev = jnp.where(i == 0, 0.0, halo_ref[7:8, lo:hi])
        rolled = pltpu.roll(u, 1, 0)
        row = lax.broadcasted_iota(jnp.int32, u.shape, 0)
        prev = jnp.where(row == 0, last_prev, rolled)
        return u + (prev - u) * mix_ref[:, lo:hi]

    r = shifted(0, w)
    k = shifted(w, 2 * w)
    v = shifted(2 * w, 3 * w)
    lora_in = shifted(3 * w, 3 * w + LANES)
    g_in = shifted(3 * w + LANES, RW_IN_PAD)

    lane = lax.broadcasted_iota(jnp.int32, lora_in.shape, 1)
    lora_act = jnp.where(lane < D_DECAY_LORA, jnp.tanh(lora_in), lora_in)
    wa = _dot_split(lora_act, w2a2_ref[...])
    w_pre = w0_ref[...] + wa[:, 0:w]
    lw = (-math.exp(-0.5)) * jax.nn.sigmoid(w_pre)
    a = jax.nn.sigmoid(a0_ref[...] + wa[:, w:2 * w])
    g = _dot(jax.nn.sigmoid(g_in).astype(BF16), g2_ref[...])

    kk_raw = k * kk_ref[...]
    kk_parts, bonus_parts = [], []
    k_mod = k * (1.0 + (a - 1.0) * ka_ref[...])
    rkr = r * k_mod * rk_ref[...]
    for p in range(w // LANES):
        sl = slice(LANES * p, LANES * (p + 1))
        x = kk_raw[:, sl]
        ss = _seg_sum(x * x, ones_bd)
        kk_parts.append(x * lax.rsqrt(jnp.maximum(ss, 1e-24)))
        bonus_parts.append(_seg_sum(rkr[:, sl], ones_bd) * v[:, sl])
    kk = jnp.concatenate(kk_parts, axis=1)
    bonus = jnp.concatenate(bonus_parts, axis=1)
    a_neg = -kk
    b_vec = kk * a

    n2 = 2 * c_len
    r_i = lax.broadcasted_iota(jnp.int32, (n2, n2), 0)
    c_i = lax.broadcasted_iota(jnp.int32, (n2, n2), 1)
    t_row = jnp.where(r_i >= c_len, r_i - c_len, r_i)
    t_col = jnp.where(c_i >= c_len, c_i - c_len, c_i)
    strict = t_col < t_row
    incl = t_col <= t_row
    lo_mask = lax.broadcasted_iota(jnp.int32, (c_len, LANES), 1) < RW_HEAD
    tri = tri_ref[...]

    n_chunks = rows // c_len
    n_pairs = w // LANES
    ar_l, bk_l, bke_l, v2_l, g_end_l = [], [], [], [], []
    for c in range(n_chunks):
        rs = slice(c * c_len, (c + 1) * c_len)
        lw_c = lw[rs]
        hi = lw_c.astype(BF16)
        mid_f = lw_c - hi.astype(F32)
        mid = mid_f.astype(BF16)
        lo = (mid_f - mid.astype(F32)).astype(BF16)
        cum = _dot(tri, hi) + (_dot(tri, mid) + _dot(tri, lo))
        cum_end = cum[c_len - 1:c_len, :]
        e_in = jnp.exp(cum)
        e_out = jnp.exp(-cum)
        e_end = jnp.exp(cum_end - cum)
        r_t = r[rs] * e_in
        a_t = a_neg[rs] * jnp.exp(cum - lw_c)
        k_t = k_mod[rs] * e_out
        b_t = b_vec[rs] * e_out
        k_e = k_mod[rs] * e_end
        b_e = b_vec[rs] * e_end
        g_end_l.append(jnp.exp(cum_end))
        v_c = v[rs]
        for p in range(n_pairs):
            sl = slice(LANES * p, LANES * (p + 1))
            ar_l.append(jnp.concatenate([_expand_heads(a_t[:, sl], lo_mask), _expand_heads(r_t[:, sl], lo_mask)],
                                        axis=0).astype(BF16))
            bk_l.append(jnp.concatenate([_expand_heads(b_t[:, sl], lo_mask), _expand_heads(k_t[:, sl], lo_mask)],
                                        axis=0).astype(BF16))
            bke_l.append(jnp.concatenate([_expand_heads(b_e[:, sl], lo_mask), _expand_heads(k_e[:, sl], lo_mask)],
                                         axis=0).astype(BF16))
            v2_l.append(_expand_heads(v_c[:, sl], lo_mask).astype(BF16))
    aa_l = [_dot_nt(ar, bk) for ar, bk in zip(ar_l, bk_l)]
    t_inv_l = _unit_lower_inverse([jnp.where(strict, aa[0:n2, 0:n2], 0.0) for aa in aa_l])
    akv_l = [_dot(jnp.where(strict, aa[0:n2, n2:2 * n2], 0.0).astype(BF16), v2) for aa, v2 in zip(aa_l, v2_l)]
    arbk_l = [jnp.concatenate([jnp.where(incl, aa[n2:2 * n2, 0:n2], 0.0),
                               jnp.where(incl, aa[n2:2 * n2, n2:2 * n2], 0.0)], axis=1).astype(BF16)
              for aa in aa_l]
    t_inv_l = [t.astype(BF16) for t in t_inv_l]

    y_chunks = []
    for c in range(n_chunks):
        idx = [c * n_pairs + p for p in range(n_pairs)]
        states = [s_ref[p] for p in range(n_pairs)]
        arh = [_dot_nt(ar_l[j], s.astype(BF16)) for j, s in zip(idx, states)]
        u_l = [_dot(t_inv_l[j], (h[0:n2] + akv_l[j]).astype(BF16)) for j, h in zip(idx, arh)]
        uv = [jnp.concatenate([u.astype(BF16), v2_l[j]], axis=0) for j, u in zip(idx, u_l)]
        y2 = [h[n2:2 * n2] + _dot(arbk_l[j], x) for j, h, x in zip(idx, arh, uv)]
        for p in range(n_pairs):
            sl = slice(LANES * p, LANES * (p + 1))
            s_ref[p] = states[p] * g_end_l[c][:, sl] + _dot_tn(uv[p], bke_l[idx[p]])
        y_chunks.append(jnp.concatenate([y[0:c_len] + y[c_len:n2] for y in y2], axis=1))
    y = jnp.concatenate(y_chunks, axis=0)

    out_parts = []
    for p in range(w // LANES):
        sl = slice(LANES * p, LANES * (p + 1))
        yp = y[:, sl]
        mu = _seg_sum(yp, ones_bd) * (1.0 / RW_HEAD)
        d = yp - mu
        var = _seg_sum(d * d, ones_bd) * (1.0 / RW_HEAD)
        out_parts.append(d * lax.rsqrt(var + GN_EPS))
    yn = jnp.concatenate(out_parts, axis=1) * lnw_ref[...] + lnb_ref[...]
    o_ref[...] = ((yn + bonus) * g).astype(o_ref.dtype)


def _rwkv(u_rw, shift_mix_p, w2a2, w0, a0, g2_p, k_k, k_a, r_k, ln_w, ln_b, ones_bd, tri):
    b, l_pad, _ = u_rw.shape
    rows = RW_ROWS
    halo_per_tile = rows // 8
    vec = lambda x: x.reshape(1, -1).astype(F32)
    const = lambda bi, i: (0, 0)
    vec_spec = pl.BlockSpec((1, RW_WIDTH), const)
    return pl.pallas_call(
        _rwkv_kernel,
        grid=(b, l_pad // rows),
        in_specs=[pl.BlockSpec((None, rows, RW_IN_PAD), lambda bi, i: (bi, i, 0)),
                  pl.BlockSpec((None, 8, RW_IN_PAD), lambda bi, i: (bi, jnp.maximum(i * halo_per_tile - 1, 0), 0)),
                  pl.BlockSpec((1, RW_IN_PAD), const),
                  pl.BlockSpec(w2a2.shape, const),
                  vec_spec, vec_spec,
                  pl.BlockSpec(g2_p.shape, const),
                  vec_spec, vec_spec, vec_spec, vec_spec, vec_spec,
                  pl.BlockSpec((LANES, LANES), const),
                  pl.BlockSpec((RW_CHUNK, RW_CHUNK), const)],
        out_specs=pl.BlockSpec((None, rows, RW_WIDTH), lambda bi, i: (bi, i, 0)),
        out_shape=jax.ShapeDtypeStruct((b, l_pad, RW_WIDTH), BF16),
        scratch_shapes=[pltpu.VMEM((RW_WIDTH // LANES, LANES, LANES), F32)],
        compiler_params=pltpu.CompilerParams(dimension_semantics=("arbitrary", "arbitrary"),
                                             vmem_limit_bytes=VMEM_LIMIT_BYTES),
        name="rwkv7",
    )(u_rw, u_rw, vec(shift_mix_p), w2a2, vec(w0), vec(a0), g2_p, vec(k_k), vec(k_a), vec(r_k),
      vec(ln_w), vec(ln_b), ones_bd, tri)


def _rope_lane_tables(length):
    pos = jnp.arange(length, dtype=F32)
    inv_freq = ROPE_THETA ** (-jnp.arange(0, ROPE_DIM, 2, dtype=F32) / ROPE_DIM)
    ang = pos[:, None] * inv_freq[None, :]
    cos, sin = jnp.cos(ang), jnp.sin(ang)
    half = ROPE_DIM // 2
    rest = DA_QK_DIM - ROPE_DIM
    ones = jnp.ones((length, rest), F32)
    zeros = lambda n: jnp.zeros((length, n), F32)
    c = jnp.concatenate([cos, cos, ones], axis=1)
    s1 = jnp.concatenate([-sin, zeros(half + rest)], axis=1)
    s2 = jnp.concatenate([zeros(half), sin, zeros(rest)], axis=1)
    rep = LANES // DA_QK_DIM
    return jnp.tile(c, (1, rep)), jnp.tile(s1, (1, rep)), jnp.tile(s2, (1, rep))


def kernel(x, meta_tokens, ffn1_norm, ffn1_w_gate, ffn1_w_up, ffn1_w_down, mix_norm, w_in, da_q_norm, da_k_norm, da_lambda_q1, da_lambda_k1, da_lambda_q2, da_lambda_k2, da_subln, rw_shift_mix, rw_w0, rw_w2, rw_a0, rw_a2, rw_g2, rw_k_k, rw_k_a, rw_r_k, rw_ln_w, rw_ln_b, w_out, ffn2_norm, ffn2_w_gate, ffn2_w_up, ffn2_w_down):
    bsz, t, d = x.shape
    depth = w_in.shape[0]
    l = N_META + t
    l_pad = -(-l // Q_BLOCK) * Q_BLOCK
    assert d == D_MODEL and l_pad % ROW_TILE == 0 and l_pad % ATTN_TILE == 0 and l_pad % RW_ROWS == 0
    meta = jnp.broadcast_to(meta_tokens.astype(x.dtype)[None], (bsz, N_META, d))
    h = jnp.concatenate([meta, x], axis=1)
    h = jnp.pad(h, ((0, 0), (0, l_pad - l), (0, 0))).reshape(bsz * l_pad, d)

    rope_c, rope_s1, rope_s2 = _rope_lane_tables(l_pad)
    lane_head = jnp.arange(LANES) // RW_HEAD
    ones_bd = (lane_head[:, None] == lane_head[None, :]).astype(BF16)
    tri = (jnp.arange(RW_CHUNK)[:, None] >= jnp.arange(RW_CHUNK)[None, :]).astype(BF16)

    for layer in range(depth):
        lam_init = 0.8 - 0.6 * math.exp(-0.3 * layer)
        bf = lambda a: a[layer].astype(BF16)
        w_in_p = jnp.pad(w_in[layer], ((0, 0), (0, N_IN_PAD - w_in.shape[2]))).astype(BF16)
        shift_mix_p = jnp.pad(rw_shift_mix[layer], (0, RW_IN_PAD - RW_IN))
        zeros_lora = jnp.zeros((D_DECAY_LORA, RW_WIDTH), F32)
        w2a2 = jnp.concatenate([jnp.concatenate([rw_w2[layer], zeros_lora], axis=1),
                                jnp.concatenate([zeros_lora, rw_a2[layer]], axis=1)], axis=0)
        g2_p = jnp.pad(rw_g2[layer], ((0, RW_LORA_PAD - D_GATE_LORA), (0, 0))).astype(BF16)
        q_gain = jnp.tile(da_q_norm[layer], DA_QK_WIDTH // DA_QK_DIM).reshape(1, DA_QK_WIDTH)
        k_gain = jnp.tile(da_k_norm[layer], DA_QK_WIDTH // DA_QK_DIM).reshape(1, DA_QK_WIDTH)
        lam_vecs = jnp.stack([da_lambda_q1[layer], da_lambda_k1[layer],
                              da_lambda_q2[layer], da_lambda_k2[layer]]).astype(F32)

        h = _ffn(h, ffn1_norm[layer], bf(ffn1_w_gate), bf(ffn1_w_up), bf(ffn1_w_down))
        q, k, v, u_rw = _mix_in(h, mix_norm[layer], w_in_p, q_gain, k_gain,
                                rope_c, rope_s1, rope_s2, ones_bd, l_pad)
        shape3 = lambda a: a.reshape(bsz, l_pad, a.shape[-1])
        o_da = _attention(shape3(q), shape3(k), shape3(v), lam_vecs, da_subln[layer], lam_init)
        o_rw = _rwkv(shape3(u_rw), shift_mix_p, w2a2, rw_w0[layer], rw_a0[layer], g2_p,
                     rw_k_k[layer], rw_k_a[layer], rw_r_k[layer], rw_ln_w[layer], rw_ln_b[layer],
                     ones_bd, tri)
        h = _ffn(h, ffn2_norm[layer], bf(ffn2_w_gate), bf(ffn2_w_up), bf(ffn2_w_down),
                 mix=(o_da.reshape(bsz * l_pad, DA_WIDTH), o_rw.reshape(bsz * l_pad, RW_WIDTH), bf(w_out)))
    return h.reshape(bsz, l_pad, d)[:, N_META:l]
```

```python
import functools
import math

import jax
import jax.numpy as jnp
from jax import lax
from jax.experimental import pallas as pl
from jax.experimental.pallas import tpu as pltpu

F32 = jnp.float32
BF16 = jnp.bfloat16

D_MODEL = 1024
N_META = 16
Q_BLOCK = 128
ROPE_THETA = 500000.0
NORM_EPS = 1e-6
DA_HEADS = 4
DA_QK_DIM = 64
DA_V_DIM = 2 * DA_QK_DIM
DA_WIDTH = DA_HEADS * DA_V_DIM
DA_QK_WIDTH = DA_HEADS * 2 * DA_QK_DIM
ROPE_DIM = DA_QK_DIM // 4
RW_HEAD = 64
RW_WIDTH = D_MODEL - DA_WIDTH
RW_HEADS = RW_WIDTH // RW_HEAD
D_DECAY_LORA = 64
D_AAA_LORA = 64
D_GATE_LORA = 160
GN_EPS = 64e-5
DA_IN = 2 * DA_QK_WIDTH + DA_WIDTH
RW_IN = 3 * RW_WIDTH + D_DECAY_LORA + D_AAA_LORA + D_GATE_LORA
D_FF = 2816

LANES = 128
SUBLANES = 8
VMEM_LIMIT_BYTES = 56 * 1024 * 1024

ROW_TILE = 640
FF_CHUNK = 256
ATTN_TILE = 640
ATTN_ROW_BLOCKS = 2
RW_CHUNK = 64
RW_ROWS = 2 * RW_CHUNK
RW_BLOCK_ROWS = 5 * RW_ROWS
RW_LORA_PAD = 2 * LANES
RW_IN_PAD = 3 * RW_WIDTH + LANES + RW_LORA_PAD
N_IN_PAD = DA_IN + RW_IN_PAD

_NT = (((1,), (1,)), ((), ()))
_TN = (((0,), (0,)), ((), ()))


def _dot(a, b):
    return jnp.dot(a, b, preferred_element_type=F32)


def _dot_nt(a, b):
    return lax.dot_general(a, b, _NT, preferred_element_type=F32)


def _dot_tn(a, b):
    return lax.dot_general(a, b, _TN, preferred_element_type=F32)


def _split2(x):
    hi = x.astype(BF16)
    lo = (x - hi.astype(F32)).astype(BF16)
    return hi, lo


def _dot_split(a, b):
    a_hi, a_lo = _split2(a)
    b_hi, b_lo = _split2(b)
    return _dot(a_hi, b_hi) + (_dot(a_hi, b_lo) + _dot(a_lo, b_hi))


def _seg_sum(x, ones_bd):
    hi, lo = _split2(x)
    return _dot(hi, ones_bd) + _dot(lo, ones_bd)


def _rms_rows(h, gain):
    ms = jnp.mean(h * h, axis=-1, keepdims=True)
    return h * lax.rsqrt(ms + NORM_EPS) * gain


def _ffn_kernel(*refs, has_mix):
    if has_mix:
        (h_ref, oda_ref, orw_ref, wout_ref, g_ref, wg_ref, wu_ref, wd_ref, o_ref, xn_ref, act_ref) = refs
        h = (h_ref[...] + _dot(oda_ref[...], wout_ref[0:DA_WIDTH, :])
             + _dot(orw_ref[...], wout_ref[DA_WIDTH:DA_WIDTH + RW_WIDTH, :]))
        o_ref[...] = h
        res_ref = o_ref
    else:
        (h_ref, g_ref, wg_ref, wu_ref, wd_ref, o_ref, xn_ref, act_ref) = refs
        h = h_ref[...]
        res_ref = h_ref
    xn_ref[...] = _rms_rows(h, g_ref[...]).astype(BF16)
    d_ff = wg_ref.shape[1]
    for c in range(d_ff // FF_CHUNK):
        cols = slice(c * FF_CHUNK, (c + 1) * FF_CHUNK)
        g = _dot(xn_ref[...], wg_ref[:, cols])
        u = _dot(xn_ref[...], wu_ref[:, cols])
        act_ref[:, cols] = ((g * jax.nn.sigmoid(g)) * u).astype(BF16)
    o_ref[...] = res_ref[...] + 0.5 * _dot(act_ref[...], wd_ref[...])


def _ffn(h, norm_g, wg, wu, wd, mix=None):
    m, d = h.shape
    tm = ROW_TILE
    d_ff = wg.shape[1]
    row = lambda i: (i, 0)
    const2 = lambda i: (0, 0)
    in_specs = [pl.BlockSpec((tm, d), row)]
    args = [h]
    if mix is not None:
        o_da, o_rw, w_out = mix
        in_specs += [pl.BlockSpec((tm, DA_WIDTH), row), pl.BlockSpec((tm, RW_WIDTH), row),
                     pl.BlockSpec(w_out.shape, const2)]
        args += [o_da, o_rw, w_out]
    in_specs += [pl.BlockSpec((1, d), const2),
                 pl.BlockSpec(wg.shape, const2, pipeline_mode=pl.Buffered(1)),
                 pl.BlockSpec(wu.shape, const2, pipeline_mode=pl.Buffered(1)),
                 pl.BlockSpec(wd.shape, const2, pipeline_mode=pl.Buffered(1))]
    args += [norm_g.reshape(1, d), wg, wu, wd]
    return pl.pallas_call(
        functools.partial(_ffn_kernel, has_mix=mix is not None),
        grid=(m // tm,),
        in_specs=in_specs,
        out_specs=pl.BlockSpec((tm, d), row),
        out_shape=jax.ShapeDtypeStruct((m, d), F32),
        scratch_shapes=[pltpu.VMEM((tm, d), BF16), pltpu.VMEM((tm, d_ff), BF16)],
        compiler_params=pltpu.CompilerParams(dimension_semantics=("arbitrary",),
                                             vmem_limit_bytes=VMEM_LIMIT_BYTES),
        name="ffn_mix" if mix is not None else "ffn",
    )(*args)


def _qk_prep(u, gain_ref, c, s1, s2, ones_bd, scale, out_ref):
    for j in range(DA_QK_WIDTH // LANES):
        sl = slice(LANES * j, LANES * (j + 1))
        x = u[:, sl]
        ss = _seg_sum(x * x, ones_bd)
        xn = x * lax.rsqrt(ss * (1.0 / DA_QK_DIM) + NORM_EPS) * gain_ref[:, sl]
        half = ROPE_DIM // 2
        xr = xn * c + pltpu.roll(xn, LANES - half, 1) * s1 + pltpu.roll(xn, half, 1) * s2
        out_ref[:, sl] = (xr * scale).astype(BF16)


def _mixin_kernel(h_ref, g_ref, win_ref, qg_ref, kg_ref, c_ref, s1_ref, s2_ref, ones_ref,
                  q_ref, k_ref, v_ref, urw_ref, *, q_scale):
    xn = _rms_rows(h_ref[...], g_ref[...]).astype(BF16)
    qw = DA_QK_WIDTH
    c, s1, s2, ones_bd = c_ref[...], s1_ref[...], s2_ref[...], ones_ref[...]
    uq = _dot(xn, win_ref[:, 0:qw])
    uk = _dot(xn, win_ref[:, qw:2 * qw])
    _qk_prep(uq, qg_ref, c, s1, s2, ones_bd, q_scale, q_ref)
    _qk_prep(uk, kg_ref, c, s1, s2, ones_bd, 1.0, k_ref)
    v_ref[...] = _dot(xn, win_ref[:, 2 * qw:DA_IN]).astype(BF16)
    urw_ref[...] = _dot(xn, win_ref[:, DA_IN:N_IN_PAD])


def _mix_in(h, norm_g, w_in_p, q_gain, k_gain, rope_c, rope_s1, rope_s2, ones_bd, l_pad):
    m, d = h.shape
    tm = ROW_TILE
    tiles_per_seq = l_pad // tm
    row = lambda i: (i, 0)
    pos = lambda i: (i % tiles_per_seq, 0)
    const2 = lambda i: (0, 0)
    q_scale = DA_QK_DIM ** -0.5 * math.log2(math.e)
    return pl.pallas_call(
        functools.partial(_mixin_kernel, q_scale=q_scale),
        grid=(m // tm,),
        in_specs=[pl.BlockSpec((tm, d), row), pl.BlockSpec((1, d), const2),
                  pl.BlockSpec(w_in_p.shape, const2, pipeline_mode=pl.Buffered(1)),
                  pl.BlockSpec((1, DA_QK_WIDTH), const2), pl.BlockSpec((1, DA_QK_WIDTH), const2),
                  pl.BlockSpec((tm, LANES), pos), pl.BlockSpec((tm, LANES), pos),
                  pl.BlockSpec((tm, LANES), pos), pl.BlockSpec((LANES, LANES), const2)],
        out_specs=[pl.BlockSpec((tm, DA_QK_WIDTH), row), pl.BlockSpec((tm, DA_QK_WIDTH), row),
                   pl.BlockSpec((tm, DA_WIDTH), row), pl.BlockSpec((tm, RW_IN_PAD), row)],
        out_shape=[jax.ShapeDtypeStruct((m, DA_QK_WIDTH), BF16), jax.ShapeDtypeStruct((m, DA_QK_WIDTH), BF16),
                   jax.ShapeDtypeStruct((m, DA_WIDTH), BF16), jax.ShapeDtypeStruct((m, RW_IN_PAD), F32)],
        compiler_params=pltpu.CompilerParams(dimension_semantics=("arbitrary",),
                                             vmem_limit_bytes=VMEM_LIMIT_BYTES),
        name="mix_in",
    )(h, norm_g.reshape(1, d), w_in_p, q_gain, k_gain, rope_c, rope_s1, rope_s2, ones_bd)


def _attn_kernel(lam_ref, q_ref, k_ref, v_ref, sub_ref, o_ref, q2_s, vx_s, m_s, l_s, acc_s, s_s, *, tq, lam_init):
    qi = pl.program_id(2)

    @pl.when(qi == 0)
    def _():
        vx_s[:, 0:LANES] = v_ref[...]
        vx_s[:, LANES:2 * LANES] = jnp.ones((vx_s.shape[0], LANES), BF16)

    q = q_ref[...]
    lane = lax.broadcasted_iota(jnp.int32, q.shape, 1)
    zero = jnp.zeros_like(q)
    q2_s[0:tq, :] = jnp.where(lane < DA_QK_DIM, q, zero)
    q2_s[tq:2 * tq, :] = jnp.where(lane < DA_QK_DIM, zero, q)
    m_s[...] = jnp.full(m_s.shape, -1e30, F32)
    l_s[...] = jnp.zeros_like(l_s)
    acc_s[...] = jnp.zeros_like(acc_s)

    def process(start, width, diag_offset):
        rb = (2 * tq) // ATTN_ROW_BLOCKS
        blocks = [slice(i * rb, (i + 1) * rb) for i in range(ATTN_ROW_BLOCKS)]
        for rows in blocks:
            s_s[rows, 0:width] = _dot_nt(q2_s[rows, :], k_ref[pl.ds(start, width), :])
        for i, rows in enumerate(blocks):
            s = s_s[rows, 0:width]
            if diag_offset is not None:
                row = lax.broadcasted_iota(jnp.int32, s.shape, 0) + (i * rb) % tq
                col = lax.broadcasted_iota(jnp.int32, s.shape, 1)
                s = jnp.where(col <= row + diag_offset, s, -jnp.inf)
            m_prev = m_s[rows, :]
            m_new = jnp.maximum(m_prev, jnp.max(s, axis=1, keepdims=True))
            alpha = jnp.exp2(m_prev - m_new)
            p = jnp.exp2(s - jnp.concatenate([m_new] * (width // LANES), axis=1))
            pv = _dot(p.astype(BF16), vx_s[pl.ds(start, width), :])
            acc_s[rows, :] = acc_s[rows, :] * alpha + pv[:, 0:LANES]
            l_s[rows, :] = l_s[rows, :] * alpha + pv[:, LANES:2 * LANES]
            m_s[rows, :] = m_new

    def wide_body(t, carry):
        process(pl.multiple_of(t * (2 * tq), 2 * tq), 2 * tq, None)
        return carry

    lax.fori_loop(0, qi // 2, wide_body, 0)

    @pl.when(qi % 2 == 0)
    def _():
        process(pl.multiple_of(qi * tq, tq), tq, 0)

    @pl.when(qi % 2 == 1)
    def _():
        process(pl.multiple_of((qi - 1) * tq, tq), 2 * tq, tq)

    lam1 = jnp.exp(jnp.sum(lam_ref[0:1, :] * lam_ref[1:2, :], axis=1, keepdims=True))
    lam2 = jnp.exp(jnp.sum(lam_ref[2:3, :] * lam_ref[3:4, :], axis=1, keepdims=True))
    lam = lam1 - lam2 + lam_init
    o = acc_s[0:tq, :] / l_s[0:tq, :] - lam * (acc_s[tq:2 * tq, :] / l_s[tq:2 * tq, :])
    o = _rms_rows(o, sub_ref[...]) * (1.0 - lam_init)
    o_ref[...] = o.astype(o_ref.dtype)


def _attention(q, k, v, lam_vecs, subln, lam_init):
    b, l_pad, _ = q.shape
    tq = ATTN_TILE
    nq = l_pad // tq
    return pl.pallas_call(
        functools.partial(_attn_kernel, tq=tq, lam_init=lam_init),
        grid=(b, DA_HEADS, nq),
        in_specs=[pl.BlockSpec(lam_vecs.shape, lambda bi, h, qi: (0, 0)),
                  pl.BlockSpec((None, tq, LANES), lambda bi, h, qi: (bi, qi, h)),
                  pl.BlockSpec((None, l_pad, LANES), lambda bi, h, qi: (bi, 0, h)),
                  pl.BlockSpec((None, l_pad, LANES), lambda bi, h, qi: (bi, 0, h)),
                  pl.BlockSpec((1, DA_V_DIM), lambda bi, h, qi: (0, 0))],
        out_specs=pl.BlockSpec((None, tq, LANES), lambda bi, h, qi: (bi, qi, h)),
        out_shape=jax.ShapeDtypeStruct((b, l_pad, DA_WIDTH), BF16),
        scratch_shapes=[pltpu.VMEM((2 * tq, LANES), BF16), pltpu.VMEM((l_pad, 2 * LANES), BF16),
                        pltpu.VMEM((2 * tq, LANES), F32), pltpu.VMEM((2 * tq, LANES), F32),
                        pltpu.VMEM((2 * tq, LANES), F32),
                        pltpu.VMEM((2 * tq, 2 * tq), F32)],
        compiler_params=pltpu.CompilerParams(dimension_semantics=("arbitrary", "arbitrary", "arbitrary"),
                                             vmem_limit_bytes=VMEM_LIMIT_BYTES),
        name="diff_attention",
    )(lam_vecs, q, k, v, subln.reshape(1, DA_V_DIM))


def _expand_heads(x, lo_mask):
    zero = jnp.zeros_like(x)
    return jnp.concatenate([jnp.where(lo_mask, x, zero), jnp.where(lo_mask, zero, x)], axis=0)


def _run(stages):
    try:
        while True:
            next(stages)
    except StopIteration as stop:
        return stop.value


def _run_interleaved(stages_a, stages_b):
    live = [stages_a, stages_b]
    results = [None, None]
    while live[0] is not None or live[1] is not None:
        for n in range(2):
            if live[n] is None:
                continue
            try:
                next(live[n])
            except StopIteration as stop:
                results[n] = stop.value
                live[n] = None
    return results


def _unit_lower_inverse(a_list):
    n = a_list[0].shape[0]
    eye = (lax.broadcasted_iota(jnp.int32, (n, n), 0) == lax.broadcasted_iota(jnp.int32, (n, n), 1)).astype(F32)
    mm = lambda x, y: _dot(x.astype(BF16), y.astype(BF16))
    t_list = [eye + a for a in a_list]
    p_list = [mm(a, a) for a in a_list]
    yield
    levels = int(math.log2(RW_CHUNK))
    for _ in range(levels - 2):
        pt_list = [mm(p, jnp.concatenate([p, t], axis=1)) for p, t in zip(p_list, t_list)]
        t_list = [t + pt[:, n:] for t, pt in zip(t_list, pt_list)]
        p_list = [pt[:, :n] for pt in pt_list]
        yield
    return [t + mm(p, t) for p, t in zip(p_list, t_list)]


def _rwkv_kernel(u_ref, halo_ref, mix_ref, w2a2_ref, w0_ref, a0_ref, g2_ref, kk_ref, ka_ref, rk_ref,
                 lnw_ref, lnb_ref, ones_ref, tri_ref, o_ref, s_ref):
    i = pl.program_id(1)
    c_len = RW_CHUNK
    w = RW_WIDTH
    n2 = 2 * c_len
    n_sub = u_ref.shape[0] // RW_ROWS
    n_chunks = RW_ROWS // c_len
    n_pairs = w // LANES
    pair_lanes = [slice(LANES * p, LANES * (p + 1)) for p in range(n_pairs)]

    @pl.when(i == 0)
    def _():
        s_ref[...] = jnp.zeros_like(s_ref)

    ones_bd = ones_ref[...]
    tri = tri_ref[...]
    r_i = lax.broadcasted_iota(jnp.int32, (n2, n2), 0)
    c_i = lax.broadcasted_iota(jnp.int32, (n2, n2), 1)
    t_row = jnp.where(r_i >= c_len, r_i - c_len, r_i)
    t_col = jnp.where(c_i >= c_len, c_i - c_len, c_i)
    strict = t_col < t_row
    incl = t_col <= t_row
    lo_mask = lax.broadcasted_iota(jnp.int32, (c_len, LANES), 1) < RW_HEAD

    def prepare(j):
        base = j * RW_ROWS

        def shifted(lo, hi):
            u = u_ref[base:base + RW_ROWS, lo:hi]
            if j == 0:
                last_prev = jnp.where(i == 0, 0.0, halo_ref[SUBLANES - 1:SUBLANES, lo:hi])
            else:
                last_prev = u_ref[base - 1:base, lo:hi]
            rolled = pltpu.roll(u, 1, 0)
            row = lax.broadcasted_iota(jnp.int32, u.shape, 0)
            prev = jnp.where(row == 0, last_prev, rolled)
            return u + (prev - u) * mix_ref[:, lo:hi]

        r = shifted(0, w)
        k = shifted(w, 2 * w)
        v = shifted(2 * w, 3 * w)
        lora_in = shifted(3 * w, 3 * w + LANES)
        g_in = shifted(3 * w + LANES, RW_IN_PAD)

        lane = lax.broadcasted_iota(jnp.int32, lora_in.shape, 1)
        lora_act = jnp.where(lane < D_DECAY_LORA, jnp.tanh(lora_in), lora_in)
        wa = _dot_split(lora_act, w2a2_ref[...])
        g = _dot(jax.nn.sigmoid(g_in).astype(BF16), g2_ref[...])
        yield
        w_pre = w0_ref[...] + wa[:, 0:w]
        lw = (-math.exp(-0.5)) * jax.nn.sigmoid(w_pre)
        a = jax.nn.sigmoid(a0_ref[...] + wa[:, w:2 * w])

        cums = []
        for c in range(n_chunks):
            lw_c = lw[c * c_len:(c + 1) * c_len]
            hi = lw_c.astype(BF16)
            mid_f = lw_c - hi.astype(F32)
            mid = mid_f.astype(BF16)
            lo = (mid_f - mid.astype(F32)).astype(BF16)
            cums.append(_dot(tri, hi) + (_dot(tri, mid) + _dot(tri, lo)))

        kk_raw = k * kk_ref[...]
        k_mod = k * (1.0 + (a - 1.0) * ka_ref[...])
        rkr = r * k_mod * rk_ref[...]
        ss_l = [_seg_sum(kk_raw[:, sl] * kk_raw[:, sl], ones_bd) for sl in pair_lanes]
        rk_l = [_seg_sum(rkr[:, sl], ones_bd) for sl in pair_lanes]
        yield
        kk = jnp.concatenate([kk_raw[:, sl] * lax.rsqrt(jnp.maximum(ss, 1e-24))
                              for sl, ss in zip(pair_lanes, ss_l)], axis=1)
        bonus = jnp.concatenate([s * v[:, sl] for sl, s in zip(pair_lanes, rk_l)], axis=1)
        a_neg = -kk
        b_vec = kk * a

        ar_l, bk_l, bke_l, v2_l, g_end_l = [], [], [], [], []
        for c in range(n_chunks):
            rs = slice(c * c_len, (c + 1) * c_len)
            cum = cums[c]
            cum_end = cum[c_len - 1:c_len, :]
            e_in = jnp.exp(cum)
            e_out = jnp.exp(-cum)
            e_end = jnp.exp(cum_end - cum)
            r_t = r[rs] * e_in
            a_t = a_neg[rs] * jnp.exp(cum - lw[rs])
            k_t = k_mod[rs] * e_out
            b_t = b_vec[rs] * e_out
            k_e = k_mod[rs] * e_end
            b_e = b_vec[rs] * e_end
            g_end_l.append(jnp.exp(cum_end))
            v_c = v[rs]
            for sl in pair_lanes:
                ar_l.append(jnp.concatenate([_expand_heads(a_t[:, sl], lo_mask), _expand_heads(r_t[:, sl], lo_mask)],
                                            axis=0).astype(BF16))
                bk_l.append(jnp.concatenate([_expand_heads(b_t[:, sl], lo_mask), _expand_heads(k_t[:, sl], lo_mask)],
                                            axis=0).astype(BF16))
                bke_l.append(jnp.concatenate([_expand_heads(b_e[:, sl], lo_mask), _expand_heads(k_e[:, sl], lo_mask)],
                                             axis=0).astype(BF16))
                v2_l.append(_expand_heads(v_c[:, sl], lo_mask).astype(BF16))
        aa_l = [_dot_nt(ar, bk) for ar, bk in zip(ar_l, bk_l)]
        yield
        t_inv_l = yield from _unit_lower_inverse([jnp.where(strict, aa[0:n2, 0:n2], 0.0) for aa in aa_l])
        akv_l = [_dot(jnp.where(strict, aa[0:n2, n2:2 * n2], 0.0).astype(BF16), v2) for aa, v2 in zip(aa_l, v2_l)]
        arbk_l = [jnp.concatenate([jnp.where(incl, aa[n2:2 * n2, 0:n2], 0.0),
                                   jnp.where(incl, aa[n2:2 * n2, n2:2 * n2], 0.0)], axis=1).astype(BF16)
                  for aa in aa_l]
        t_inv_l = [t.astype(BF16) for t in t_inv_l]
        yield
        return dict(ar=ar_l, bke=bke_l, v2=v2_l, g_end=g_end_l, t_inv=t_inv_l, akv=akv_l, arbk=arbk_l,
                    bonus=bonus, gate=g)

    def advance(j, pre):
        base = j * RW_ROWS
        y_chunks = []
        for c in range(n_chunks):
            idx = [c * n_pairs + p for p in range(n_pairs)]
            states = [s_ref[p] for p in range(n_pairs)]
            arh = [_dot_nt(pre["ar"][q], s.astype(BF16)) for q, s in zip(idx, states)]
            yield
            u_l = [_dot(pre["t_inv"][q], (h[0:n2] + pre["akv"][q]).astype(BF16)) for q, h in zip(idx, arh)]
            yield
            uv = [jnp.concatenate([u.astype(BF16), pre["v2"][q]], axis=0) for q, u in zip(idx, u_l)]
            y2 = [h[n2:2 * n2] + _dot(pre["arbk"][q], x) for q, h, x in zip(idx, arh, uv)]
            for p in range(n_pairs):
                s_ref[p] = states[p] * pre["g_end"][c][:, pair_lanes[p]] + _dot_tn(uv[p], pre["bke"][idx[p]])
            yield
            y_chunks.append(jnp.concatenate([y[0:c_len] + y[c_len:n2] for y in y2], axis=1))
        y = jnp.concatenate(y_chunks, axis=0)

        mu_l = [_seg_sum(y[:, sl], ones_bd) * (1.0 / RW_HEAD) for sl in pair_lanes]
        yield
        d_l = [y[:, sl] - mu for sl, mu in zip(pair_lanes, mu_l)]
        var_l = [_seg_sum(d * d, ones_bd) * (1.0 / RW_HEAD) for d in d_l]
        yield
        yn = jnp.concatenate([d * lax.rsqrt(var + GN_EPS) for d, var in zip(d_l, var_l)], axis=1)
        yn = yn * lnw_ref[...] + lnb_ref[...]
        o_ref[base:base + RW_ROWS, :] = ((yn + pre["bonus"]) * pre["gate"]).astype(o_ref.dtype)

    pre = _run(prepare(0))
    for j in range(n_sub):
        if j + 1 < n_sub:
            pre, _ = _run_interleaved(prepare(j + 1), advance(j, pre))
        else:
            _run(advance(j, pre))


def _rwkv(u_rw, shift_mix_p, w2a2, w0, a0, g2_p, k_k, k_a, r_k, ln_w, ln_b, ones_bd, tri):
    b, l_pad, _ = u_rw.shape
    rows = RW_BLOCK_ROWS
    halo_per_tile = rows // SUBLANES
    vec = lambda x: x.reshape(1, -1).astype(F32)
    const = lambda bi, i: (0, 0)
    vec_spec = pl.BlockSpec((1, RW_WIDTH), const)
    return pl.pallas_call(
        _rwkv_kernel,
        grid=(b, l_pad // rows),
        in_specs=[pl.BlockSpec((None, rows, RW_IN_PAD), lambda bi, i: (bi, i, 0)),
                  pl.BlockSpec((None, SUBLANES, RW_IN_PAD),
                               lambda bi, i: (bi, jnp.maximum(i * halo_per_tile - 1, 0), 0)),
                  pl.BlockSpec((1, RW_IN_PAD), const),
                  pl.BlockSpec(w2a2.shape, const),
                  vec_spec, vec_spec,
                  pl.BlockSpec(g2_p.shape, const),
                  vec_spec, vec_spec, vec_spec, vec_spec, vec_spec,
                  pl.BlockSpec((LANES, LANES), const),
                  pl.BlockSpec((RW_CHUNK, RW_CHUNK), const)],
        out_specs=pl.BlockSpec((None, rows, RW_WIDTH), lambda bi, i: (bi, i, 0)),
        out_shape=jax.ShapeDtypeStruct((b, l_pad, RW_WIDTH), BF16),
        scratch_shapes=[pltpu.VMEM((RW_WIDTH // LANES, LANES, LANES), F32)],
        compiler_params=pltpu.CompilerParams(dimension_semantics=("arbitrary", "arbitrary"),
                                             vmem_limit_bytes=VMEM_LIMIT_BYTES),
        name="rwkv7",
    )(u_rw, u_rw, vec(shift_mix_p), w2a2, vec(w0), vec(a0), g2_p, vec(k_k), vec(k_a), vec(r_k),
      vec(ln_w), vec(ln_b), ones_bd, tri)


def _rope_lane_tables(length):
    pos = jnp.arange(length, dtype=F32)
    inv_freq = ROPE_THETA ** (-jnp.arange(0, ROPE_DIM, 2, dtype=F32) / ROPE_DIM)
    ang = pos[:, None] * inv_freq[None, :]
    cos, sin = jnp.cos(ang), jnp.sin(ang)
    half = ROPE_DIM // 2
    rest = DA_QK_DIM - ROPE_DIM
    ones = jnp.ones((length, rest), F32)
    zeros = lambda n: jnp.zeros((length, n), F32)
    c = jnp.concatenate([cos, cos, ones], axis=1)
    s1 = jnp.concatenate([-sin, zeros(half + rest)], axis=1)
    s2 = jnp.concatenate([zeros(half), sin, zeros(rest)], axis=1)
    rep = LANES // DA_QK_DIM
    return jnp.tile(c, (1, rep)), jnp.tile(s1, (1, rep)), jnp.tile(s2, (1, rep))


def kernel(x, meta_tokens, ffn1_norm, ffn1_w_gate, ffn1_w_up, ffn1_w_down, mix_norm, w_in, da_q_norm, da_k_norm, da_lambda_q1, da_lambda_k1, da_lambda_q2, da_lambda_k2, da_subln, rw_shift_mix, rw_w0, rw_w2, rw_a0, rw_a2, rw_g2, rw_k_k, rw_k_a, rw_r_k, rw_ln_w, rw_ln_b, w_out, ffn2_norm, ffn2_w_gate, ffn2_w_up, ffn2_w_down):
    bsz, t, d = x.shape
    depth = w_in.shape[0]
    l = N_META + t
    l_pad = -(-l // Q_BLOCK) * Q_BLOCK
    assert d == D_MODEL and l_pad % ROW_TILE == 0 and l_pad % ATTN_TILE == 0 and l_pad % RW_BLOCK_ROWS == 0
    meta = jnp.broadcast_to(meta_tokens.astype(x.dtype)[None], (bsz, N_META, d))
    h = jnp.concatenate([meta, x], axis=1)
    h = jnp.pad(h, ((0, 0), (0, l_pad - l), (0, 0))).reshape(bsz * l_pad, d)

    rope_c, rope_s1, rope_s2 = _rope_lane_tables(l_pad)
    lane_head = jnp.arange(LANES) // RW_HEAD
    ones_bd = (lane_head[:, None] == lane_head[None, :]).astype(BF16)
    tri = (jnp.arange(RW_CHUNK)[:, None] >= jnp.arange(RW_CHUNK)[None, :]).astype(BF16)

    for layer in range(depth):
        lam_init = 0.8 - 0.6 * math.exp(-0.3 * layer)
        bf = lambda a: a[layer].astype(BF16)
        w_in_p = jnp.pad(w_in[layer], ((0, 0), (0, N_IN_PAD - w_in.shape[2]))).astype(BF16)
        shift_mix_p = jnp.pad(rw_shift_mix[layer], (0, RW_IN_PAD - RW_IN))
        zeros_lora = jnp.zeros((D_DECAY_LORA, RW_WIDTH), F32)
        w2a2 = jnp.concatenate([jnp.concatenate([rw_w2[layer], zeros_lora], axis=1),
                                jnp.concatenate([zeros_lora, rw_a2[layer]], axis=1)], axis=0)
        g2_p = jnp.pad(rw_g2[layer], ((0, RW_LORA_PAD - D_GATE_LORA), (0, 0))).astype(BF16)
        q_gain = jnp.tile(da_q_norm[layer], DA_QK_WIDTH // DA_QK_DIM).reshape(1, DA_QK_WIDTH)
        k_gain = jnp.tile(da_k_norm[layer], DA_QK_WIDTH // DA_QK_DIM).reshape(1, DA_QK_WIDTH)
        lam_vecs = jnp.stack([da_lambda_q1[layer], da_lambda_k1[layer],
                              da_lambda_q2[layer], da_lambda_k2[layer]]).astype(F32)

        h = _ffn(h, ffn1_norm[layer], bf(ffn1_w_gate), bf(ffn1_w_up), bf(ffn1_w_down))
        q, k, v, u_rw = _mix_in(h, mix_norm[layer], w_in_p, q_gain, k_gain,
                                rope_c, rope_s1, rope_s2, ones_bd, l_pad)
        shape3 = lambda a: a.reshape(bsz, l_pad, a.shape[-1])
        o_da = _attention(shape3(q), shape3(k), shape3(v), lam_vecs, da_subln[layer], lam_init)
        o_rw = _rwkv(shape3(u_rw), shift_mix_p, w2a2, rw_w0[layer], rw_a0[layer], g2_p,
                     rw_k_k[layer], rw_k_a[layer], rw_r_k[layer], rw_ln_w[layer], rw_ln_b[layer],
                     ones_bd, tri)
        h = _ffn(h, ffn2_norm[layer], bf(ffn2_w_gate), bf(ffn2_w_up), bf(ffn2_w_down),
                 mix=(o_da.reshape(bsz * l_pad, DA_WIDTH), o_rw.reshape(bsz * l_pad, RW_WIDTH), bf(w_out)))
    return h.reshape(bsz, l_pad, d)[:, N_META:l]
```

```python
import functools
import math

import jax
import jax.numpy as jnp
from jax import lax
from jax.experimental import pallas as pl
from jax.experimental.pallas import tpu as pltpu

F32 = jnp.float32
BF16 = jnp.bfloat16

D_MODEL = 1024
N_META = 16
Q_BLOCK = 128
ROPE_THETA = 500000.0
NORM_EPS = 1e-6
DA_HEADS = 4
DA_QK_DIM = 64
DA_V_DIM = 2 * DA_QK_DIM
DA_WIDTH = DA_HEADS * DA_V_DIM
DA_QK_WIDTH = DA_HEADS * 2 * DA_QK_DIM
ROPE_DIM = DA_QK_DIM // 4
RW_HEAD = 64
RW_WIDTH = D_MODEL - DA_WIDTH
RW_HEADS = RW_WIDTH // RW_HEAD
D_DECAY_LORA = 64
D_AAA_LORA = 64
D_GATE_LORA = 160
GN_EPS = 64e-5
DA_IN = 2 * DA_QK_WIDTH + DA_WIDTH
RW_IN = 3 * RW_WIDTH + D_DECAY_LORA + D_AAA_LORA + D_GATE_LORA
D_FF = 2816

LANES = 128
SUBLANES = 8
VMEM_LIMIT_BYTES = 56 * 1024 * 1024

ROW_TILE = 640
COMPACT_ROW_TILE = 512
FF_CHUNK = 256
ATTN_TILE = 640
ATTN_HEADS_PER_STEP = 2
RW_CHUNK = 64
RW_ROWS = 2 * RW_CHUNK
RW_BLOCK_ROWS = 5 * RW_ROWS
RW_LORA_PAD = 2 * LANES
RW_IN_PAD = 3 * RW_WIDTH + LANES + RW_LORA_PAD
N_IN_PAD = DA_IN + RW_IN_PAD

_NT = (((1,), (1,)), ((), ()))
_TN = (((0,), (0,)), ((), ()))


def _dot(a, b):
    return jnp.dot(a, b, preferred_element_type=F32)


def _dot_nt(a, b):
    return lax.dot_general(a, b, _NT, preferred_element_type=F32)


def _dot_tn(a, b):
    return lax.dot_general(a, b, _TN, preferred_element_type=F32)


def _split2(x):
    hi = x.astype(BF16)
    lo = (x - hi.astype(F32)).astype(BF16)
    return hi, lo


def _dot_split(a, b):
    a_hi, a_lo = _split2(a)
    b_hi, b_lo = _split2(b)
    return _dot(a_hi, b_hi) + (_dot(a_hi, b_lo) + _dot(a_lo, b_hi))


def _seg_sum(x, ones_bd):
    hi, lo = _split2(x)
    return _dot(hi, ones_bd) + _dot(lo, ones_bd)


def _rms_rows(h, gain):
    ms = jnp.mean(h * h, axis=-1, keepdims=True)
    return h * lax.rsqrt(ms + NORM_EPS) * gain


def _ffn_kernel(*refs, has_mix):
    if has_mix:
        (h_ref, oda_ref, orw_ref, wout_ref, g_ref, wg_ref, wu_ref, wd_ref, o_ref, xn_ref, act_ref) = refs
        h = (h_ref[...] + _dot(oda_ref[...], wout_ref[0:DA_WIDTH, :])
             + _dot(orw_ref[...], wout_ref[DA_WIDTH:DA_WIDTH + RW_WIDTH, :]))
        o_ref[...] = h
        res_ref = o_ref
    else:
        (h_ref, g_ref, wg_ref, wu_ref, wd_ref, o_ref, xn_ref, act_ref) = refs
        h = h_ref[...]
        res_ref = h_ref
    xn_ref[...] = _rms_rows(h, g_ref[...]).astype(BF16)
    d_ff = wg_ref.shape[1]
    for c in range(d_ff // FF_CHUNK):
        cols = slice(c * FF_CHUNK, (c + 1) * FF_CHUNK)
        g = _dot(xn_ref[...], wg_ref[:, cols])
        u = _dot(xn_ref[...], wu_ref[:, cols])
        act_ref[:, cols] = ((g * jax.nn.sigmoid(g)) * u).astype(BF16)
    o_ref[...] = res_ref[...] + 0.5 * _dot(act_ref[...], wd_ref[...])


def _ffn(h, norm_g, wg, wu, wd, mix=None, compact=None):
    m, d = h.shape
    d_ff = wg.shape[1]
    row = lambda i: (i, 0)
    const2 = lambda i: (0, 0)
    if compact is None:
        tm = ROW_TILE
        m_out = m
        row_spec = lambda width: pl.BlockSpec((tm, width), row)
    else:
        l_pad, first_row, n_rows = compact
        tm = COMPACT_ROW_TILE
        align = 2 * SUBLANES
        assert n_rows % tm == 0 and first_row % align == 0 and l_pad % align == 0 and tm % align == 0
        tiles = n_rows // tm
        m_out = (m // l_pad) * n_rows
        window = lambda i: (pl.multiple_of((i // tiles) * l_pad + first_row + (i % tiles) * tm, align), 0)
        row_spec = lambda width: pl.BlockSpec((pl.Element(tm), pl.Element(width)), window)
    in_specs = [row_spec(d)]
    args = [h]
    if mix is not None:
        o_da, o_rw, w_out = mix
        in_specs += [row_spec(DA_WIDTH), row_spec(RW_WIDTH), pl.BlockSpec(w_out.shape, const2)]
        args += [o_da, o_rw, w_out]
    in_specs += [pl.BlockSpec((1, d), const2),
                 pl.BlockSpec(wg.shape, const2, pipeline_mode=pl.Buffered(1)),
                 pl.BlockSpec(wu.shape, const2, pipeline_mode=pl.Buffered(1)),
                 pl.BlockSpec(wd.shape, const2, pipeline_mode=pl.Buffered(1))]
    args += [norm_g.reshape(1, d), wg, wu, wd]
    return pl.pallas_call(
        functools.partial(_ffn_kernel, has_mix=mix is not None),
        grid=(m_out // tm,),
        in_specs=in_specs,
        out_specs=pl.BlockSpec((tm, d), row),
        out_shape=jax.ShapeDtypeStruct((m_out, d), F32),
        scratch_shapes=[pltpu.VMEM((tm, d), BF16), pltpu.VMEM((tm, d_ff), BF16)],
        compiler_params=pltpu.CompilerParams(dimension_semantics=("arbitrary",),
                                             vmem_limit_bytes=VMEM_LIMIT_BYTES),
        name="ffn_mix" if mix is not None else "ffn",
    )(*args)


def _qk_prep(u, gain_ref, c, s1, s2, ones_bd, scale, out_ref):
    for j in range(DA_QK_WIDTH // LANES):
        sl = slice(LANES * j, LANES * (j + 1))
        x = u[:, sl]
        ss = _seg_sum(x * x, ones_bd)
        xn = x * lax.rsqrt(ss * (1.0 / DA_QK_DIM) + NORM_EPS) * gain_ref[:, sl]
        half = ROPE_DIM // 2
        xr = xn * c + pltpu.roll(xn, LANES - half, 1) * s1 + pltpu.roll(xn, half, 1) * s2
        out_ref[:, sl] = (xr * scale).astype(BF16)


def _mixin_kernel(h_ref, g_ref, win_ref, qg_ref, kg_ref, c_ref, s1_ref, s2_ref, ones_ref,
                  q_ref, k_ref, v_ref, urw_ref, *, q_scale):
    xn = _rms_rows(h_ref[...], g_ref[...]).astype(BF16)
    qw = DA_QK_WIDTH
    c, s1, s2, ones_bd = c_ref[...], s1_ref[...], s2_ref[...], ones_ref[...]
    uq = _dot(xn, win_ref[:, 0:qw])
    uk = _dot(xn, win_ref[:, qw:2 * qw])
    _qk_prep(uq, qg_ref, c, s1, s2, ones_bd, q_scale, q_ref)
    _qk_prep(uk, kg_ref, c, s1, s2, ones_bd, 1.0, k_ref)
    v_ref[...] = _dot(xn, win_ref[:, 2 * qw:DA_IN]).astype(BF16)
    urw_ref[...] = _dot(xn, win_ref[:, DA_IN:N_IN_PAD])


def _mix_in(h, norm_g, w_in_p, q_gain, k_gain, rope_c, rope_s1, rope_s2, ones_bd, l_pad):
    m, d = h.shape
    tm = ROW_TILE
    tiles_per_seq = l_pad // tm
    row = lambda i: (i, 0)
    pos = lambda i: (i % tiles_per_seq, 0)
    const2 = lambda i: (0, 0)
    q_scale = DA_QK_DIM ** -0.5 * math.log2(math.e)
    return pl.pallas_call(
        functools.partial(_mixin_kernel, q_scale=q_scale),
        grid=(m // tm,),
        in_specs=[pl.BlockSpec((tm, d), row), pl.BlockSpec((1, d), const2),
                  pl.BlockSpec(w_in_p.shape, const2, pipeline_mode=pl.Buffered(1)),
                  pl.BlockSpec((1, DA_QK_WIDTH), const2), pl.BlockSpec((1, DA_QK_WIDTH), const2),
                  pl.BlockSpec((tm, LANES), pos), pl.BlockSpec((tm, LANES), pos),
                  pl.BlockSpec((tm, LANES), pos), pl.BlockSpec((LANES, LANES), const2)],
        out_specs=[pl.BlockSpec((tm, DA_QK_WIDTH), row), pl.BlockSpec((tm, DA_QK_WIDTH), row),
                   pl.BlockSpec((tm, DA_WIDTH), row), pl.BlockSpec((tm, RW_IN_PAD), row)],
        out_shape=[jax.ShapeDtypeStruct((m, DA_QK_WIDTH), BF16), jax.ShapeDtypeStruct((m, DA_QK_WIDTH), BF16),
                   jax.ShapeDtypeStruct((m, DA_WIDTH), BF16), jax.ShapeDtypeStruct((m, RW_IN_PAD), F32)],
        compiler_params=pltpu.CompilerParams(dimension_semantics=("arbitrary",),
                                             vmem_limit_bytes=VMEM_LIMIT_BYTES),
        name="mix_in",
    )(h, norm_g.reshape(1, d), w_in_p, q_gain, k_gain, rope_c, rope_s1, rope_s2, ones_bd)


def _attn_kernel(lam_ref, q_ref, k_ref, v_ref, sub_ref, o_ref, q2_s, vx_s, m_s, l_s, acc_s, s_s, *, tq, lam_init):
    qi = pl.program_id(2)
    n_heads = q_ref.shape[1] // LANES
    head_lanes = [slice(LANES * hh, LANES * (hh + 1)) for hh in range(n_heads)]
    vx_lanes = [slice(2 * LANES * hh, 2 * LANES * (hh + 1)) for hh in range(n_heads)]
    streams = [(hh, slice((2 * hh + c) * tq, (2 * hh + c + 1) * tq)) for hh in range(n_heads) for c in range(2)]

    @pl.when(qi == 0)
    def _():
        for hh in range(n_heads):
            vx_s[:, 2 * LANES * hh:2 * LANES * hh + LANES] = v_ref[:, head_lanes[hh]]
            vx_s[:, 2 * LANES * hh + LANES:2 * LANES * (hh + 1)] = jnp.ones((vx_s.shape[0], LANES), BF16)

    for hh in range(n_heads):
        q = q_ref[:, head_lanes[hh]]
        lane = lax.broadcasted_iota(jnp.int32, q.shape, 1)
        zero = jnp.zeros_like(q)
        q2_s[streams[2 * hh][1], :] = jnp.where(lane < DA_QK_DIM, q, zero)
        q2_s[streams[2 * hh + 1][1], :] = jnp.where(lane < DA_QK_DIM, zero, q)
    m_s[...] = jnp.full(m_s.shape, -1e30, F32)
    l_s[...] = jnp.zeros_like(l_s)
    acc_s[...] = jnp.zeros_like(acc_s)

    def process(start, width, diag_offset):
        for hh, rows in streams:
            s_s[rows, 0:width] = _dot_nt(q2_s[rows, :], k_ref[pl.ds(start, width), head_lanes[hh]])
        for hh, rows in streams:
            s = s_s[rows, 0:width]
            if diag_offset is not None:
                row = lax.broadcasted_iota(jnp.int32, s.shape, 0)
                col = lax.broadcasted_iota(jnp.int32, s.shape, 1)
                s = jnp.where(col <= row + diag_offset, s, -jnp.inf)
            m_prev = m_s[rows, :]
            m_new = jnp.maximum(m_prev, jnp.max(s, axis=1, keepdims=True))
            alpha = jnp.exp2(m_prev - m_new)
            p = jnp.exp2(s - jnp.concatenate([m_new] * (width // LANES), axis=1))
            pv = _dot(p.astype(BF16), vx_s[pl.ds(start, width), vx_lanes[hh]])
            acc_s[rows, :] = acc_s[rows, :] * alpha + pv[:, 0:LANES]
            l_s[rows, :] = l_s[rows, :] * alpha + pv[:, LANES:2 * LANES]
            m_s[rows, :] = m_new

    def wide_body(t, carry):
        process(pl.multiple_of(t * (2 * tq), 2 * tq), 2 * tq, None)
        return carry

    lax.fori_loop(0, qi // 2, wide_body, 0)

    @pl.when(qi % 2 == 0)
    def _():
        process(pl.multiple_of(qi * tq, tq), tq, 0)

    @pl.when(qi % 2 == 1)
    def _():
        process(pl.multiple_of((qi - 1) * tq, tq), 2 * tq, tq)

    lam1 = jnp.exp(jnp.sum(lam_ref[0:1, :] * lam_ref[1:2, :], axis=1, keepdims=True))
    lam2 = jnp.exp(jnp.sum(lam_ref[2:3, :] * lam_ref[3:4, :], axis=1, keepdims=True))
    lam = lam1 - lam2 + lam_init
    for hh in range(n_heads):
        r0, r1 = streams[2 * hh][1], streams[2 * hh + 1][1]
        o = acc_s[r0, :] / l_s[r0, :] - lam * (acc_s[r1, :] / l_s[r1, :])
        o = _rms_rows(o, sub_ref[...]) * (1.0 - lam_init)
        o_ref[:, head_lanes[hh]] = o.astype(o_ref.dtype)


def _attention(q, k, v, lam_vecs, subln, lam_init):
    b, l_pad, _ = q.shape
    tq = ATTN_TILE
    nq = l_pad // tq
    nh = ATTN_HEADS_PER_STEP
    hw = nh * LANES
    n_streams = 2 * nh
    kv_spec = pl.BlockSpec((None, l_pad, hw), lambda bi, h, qi: (bi, 0, h), pipeline_mode=pl.Buffered(1))
    return pl.pallas_call(
        functools.partial(_attn_kernel, tq=tq, lam_init=lam_init),
        grid=(b, DA_HEADS // nh, nq),
        in_specs=[pl.BlockSpec(lam_vecs.shape, lambda bi, h, qi: (0, 0)),
                  pl.BlockSpec((None, tq, hw), lambda bi, h, qi: (bi, qi, h)),
                  kv_spec, kv_spec,
                  pl.BlockSpec((1, DA_V_DIM), lambda bi, h, qi: (0, 0))],
        out_specs=pl.BlockSpec((None, tq, hw), lambda bi, h, qi: (bi, qi, h)),
        out_shape=jax.ShapeDtypeStruct((b, l_pad, DA_WIDTH), BF16),
        scratch_shapes=[pltpu.VMEM((n_streams * tq, LANES), BF16), pltpu.VMEM((l_pad, 2 * hw), BF16),
                        pltpu.VMEM((n_streams * tq, LANES), F32), pltpu.VMEM((n_streams * tq, LANES), F32),
                        pltpu.VMEM((n_streams * tq, LANES), F32),
                        pltpu.VMEM((n_streams * tq, 2 * tq), F32)],
        compiler_params=pltpu.CompilerParams(dimension_semantics=("arbitrary", "arbitrary", "arbitrary"),
                                             vmem_limit_bytes=VMEM_LIMIT_BYTES),
        name="diff_attention",
    )(lam_vecs, q, k, v, subln.reshape(1, DA_V_DIM))


def _expand_heads(x, lo_mask):
    zero = jnp.zeros_like(x)
    return jnp.concatenate([jnp.where(lo_mask, x, zero), jnp.where(lo_mask, zero, x)], axis=0)


def _run(stages):
    try:
        while True:
            next(stages)
    except StopIteration as stop:
        return stop.value


def _run_interleaved(stages_a, stages_b):
    live = [stages_a, stages_b]
    results = [None, None]
    while live[0] is not None or live[1] is not None:
        for n in range(2):
            if live[n] is None:
                continue
            try:
                next(live[n])
            except StopIteration as stop:
                results[n] = stop.value
                live[n] = None
    return results


def _unit_lower_inverse(a_list):
    n = a_list[0].shape[0]
    eye = (lax.broadcasted_iota(jnp.int32, (n, n), 0) == lax.broadcasted_iota(jnp.int32, (n, n), 1)).astype(F32)
    mm = lambda x, y: _dot(x.astype(BF16), y.astype(BF16))
    t_list = [eye + a for a in a_list]
    p_list = [mm(a, a) for a in a_list]
    yield
    levels = int(math.log2(RW_CHUNK))
    for _ in range(levels - 2):
        pt_list = [mm(p, jnp.concatenate([p, t], axis=1)) for p, t in zip(p_list, t_list)]
        t_list = [t + pt[:, n:] for t, pt in zip(t_list, pt_list)]
        p_list = [pt[:, :n] for pt in pt_list]
        yield
    return [t + mm(p, t) for p, t in zip(p_list, t_list)]


def _rwkv_kernel(u_ref, halo_ref, mix_ref, w2a2_ref, w0_ref, a0_ref, g2_ref, kk_ref, ka_ref, rk_ref,
                 lnw_ref, lnb_ref, ones_ref, tri_ref, o_ref, s_ref):
    i = pl.program_id(1)
    c_len = RW_CHUNK
    w = RW_WIDTH
    n2 = 2 * c_len
    n_sub = u_ref.shape[0] // RW_ROWS
    n_chunks = RW_ROWS // c_len
    n_pairs = w // LANES
    pair_lanes = [slice(LANES * p, LANES * (p + 1)) for p in range(n_pairs)]

    @pl.when(i == 0)
    def _():
        s_ref[...] = jnp.zeros_like(s_ref)

    ones_bd = ones_ref[...]
    tri = tri_ref[...]
    r_i = lax.broadcasted_iota(jnp.int32, (n2, n2), 0)
    c_i = lax.broadcasted_iota(jnp.int32, (n2, n2), 1)
    t_row = jnp.where(r_i >= c_len, r_i - c_len, r_i)
    t_col = jnp.where(c_i >= c_len, c_i - c_len, c_i)
    strict = t_col < t_row
    incl = t_col <= t_row
    lo_mask = lax.broadcasted_iota(jnp.int32, (c_len, LANES), 1) < RW_HEAD

    def prepare(j):
        base = j * RW_ROWS

        def shifted(lo, hi):
            u = u_ref[base:base + RW_ROWS, lo:hi]
            if j == 0:
                last_prev = jnp.where(i == 0, 0.0, halo_ref[SUBLANES - 1:SUBLANES, lo:hi])
            else:
                last_prev = u_ref[base - 1:base, lo:hi]
            rolled = pltpu.roll(u, 1, 0)
            row = lax.broadcasted_iota(jnp.int32, u.shape, 0)
            prev = jnp.where(row == 0, last_prev, rolled)
            return u + (prev - u) * mix_ref[:, lo:hi]

        r = shifted(0, w)
        k = shifted(w, 2 * w)
        v = shifted(2 * w, 3 * w)
        lora_in = shifted(3 * w, 3 * w + LANES)
        g_in = shifted(3 * w + LANES, RW_IN_PAD)

        lane = lax.broadcasted_iota(jnp.int32, lora_in.shape, 1)
        lora_act = jnp.where(lane < D_DECAY_LORA, jnp.tanh(lora_in), lora_in)
        wa = _dot_split(lora_act, w2a2_ref[...])
        g = _dot(jax.nn.sigmoid(g_in).astype(BF16), g2_ref[...])
        yield
        w_pre = w0_ref[...] + wa[:, 0:w]
        lw = (-math.exp(-0.5)) * jax.nn.sigmoid(w_pre)
        a = jax.nn.sigmoid(a0_ref[...] + wa[:, w:2 * w])

        cums = []
        for c in range(n_chunks):
            lw_c = lw[c * c_len:(c + 1) * c_len]
            hi = lw_c.astype(BF16)
            mid_f = lw_c - hi.astype(F32)
            mid = mid_f.astype(BF16)
            lo = (mid_f - mid.astype(F32)).astype(BF16)
            cums.append(_dot(tri, hi) + (_dot(tri, mid) + _dot(tri, lo)))

        kk_raw = k * kk_ref[...]
        k_mod = k * (1.0 + (a - 1.0) * ka_ref[...])
        rkr = r * k_mod * rk_ref[...]
        ss_l = [_seg_sum(kk_raw[:, sl] * kk_raw[:, sl], ones_bd) for sl in pair_lanes]
        rk_l = [_seg_sum(rkr[:, sl], ones_bd) for sl in pair_lanes]
        yield
        kk = jnp.concatenate([kk_raw[:, sl] * lax.rsqrt(jnp.maximum(ss, 1e-24))
                              for sl, ss in zip(pair_lanes, ss_l)], axis=1)
        bonus = jnp.concatenate([s * v[:, sl] for sl, s in zip(pair_lanes, rk_l)], axis=1)
        a_neg = -kk
        b_vec = kk * a

        ar_l, bk_l, bke_l, v2_l, g_end_l = [], [], [], [], []
        for c in range(n_chunks):
            rs = slice(c * c_len, (c + 1) * c_len)
            cum = cums[c]
            cum_end = cum[c_len - 1:c_len, :]
            e_in = jnp.exp(cum)
            e_out = jnp.exp(-cum)
            e_end = jnp.exp(cum_end - cum)
            r_t = r[rs] * e_in
            a_t = a_neg[rs] * jnp.exp(cum - lw[rs])
            k_t = k_mod[rs] * e_out
            b_t = b_vec[rs] * e_out
            k_e = k_mod[rs] * e_end
            b_e = b_vec[rs] * e_end
            g_end_l.append(jnp.exp(cum_end))
            v_c = v[rs]
            for sl in pair_lanes:
                ar_l.append(jnp.concatenate([_expand_heads(a_t[:, sl], lo_mask), _expand_heads(r_t[:, sl], lo_mask)],
                                            axis=0).astype(BF16))
                bk_l.append(jnp.concatenate([_expand_heads(b_t[:, sl], lo_mask), _expand_heads(k_t[:, sl], lo_mask)],
                                            axis=0).astype(BF16))
                bke_l.append(jnp.concatenate([_expand_heads(b_e[:, sl], lo_mask), _expand_heads(k_e[:, sl], lo_mask)],
                                             axis=0).astype(BF16))
                v2_l.append(_expand_heads(v_c[:, sl], lo_mask).astype(BF16))
        aa_l = [_dot_nt(ar, bk) for ar, bk in zip(ar_l, bk_l)]
        yield
        t_inv_l = yield from _unit_lower_inverse([jnp.where(strict, aa[0:n2, 0:n2], 0.0) for aa in aa_l])
        akv_l = [_dot(jnp.where(strict, aa[0:n2, n2:2 * n2], 0.0).astype(BF16), v2) for aa, v2 in zip(aa_l, v2_l)]
        arbk_l = [jnp.concatenate([jnp.where(incl, aa[n2:2 * n2, 0:n2], 0.0),
                                   jnp.where(incl, aa[n2:2 * n2, n2:2 * n2], 0.0)], axis=1).astype(BF16)
                  for aa in aa_l]
        t_inv_l = [t.astype(BF16) for t in t_inv_l]
        yield
        return dict(ar=ar_l, bke=bke_l, v2=v2_l, g_end=g_end_l, t_inv=t_inv_l, akv=akv_l, arbk=arbk_l,
                    bonus=bonus, gate=g)

    def advance(j, pre):
        base = j * RW_ROWS
        y_chunks = []
        for c in range(n_chunks):
            idx = [c * n_pairs + p for p in range(n_pairs)]
            states = [s_ref[p] for p in range(n_pairs)]
            arh = [_dot_nt(pre["ar"][q], s.astype(BF16)) for q, s in zip(idx, states)]
            yield
            u_l = [_dot(pre["t_inv"][q], (h[0:n2] + pre["akv"][q]).astype(BF16)) for q, h in zip(idx, arh)]
            yield
            uv = [jnp.concatenate([u.astype(BF16), pre["v2"][q]], axis=0) for q, u in zip(idx, u_l)]
            y2 = [h[n2:2 * n2] + _dot(pre["arbk"][q], x) for q, h, x in zip(idx, arh, uv)]
            for p in range(n_pairs):
                s_ref[p] = states[p] * pre["g_end"][c][:, pair_lanes[p]] + _dot_tn(uv[p], pre["bke"][idx[p]])
            yield
            y_chunks.append(jnp.concatenate([y[0:c_len] + y[c_len:n2] for y in y2], axis=1))
        y = jnp.concatenate(y_chunks, axis=0)

        mu_l = [_seg_sum(y[:, sl], ones_bd) * (1.0 / RW_HEAD) for sl in pair_lanes]
        yield
        d_l = [y[:, sl] - mu for sl, mu in zip(pair_lanes, mu_l)]
        var_l = [_seg_sum(d * d, ones_bd) * (1.0 / RW_HEAD) for d in d_l]
        yield
        yn = jnp.concatenate([d * lax.rsqrt(var + GN_EPS) for d, var in zip(d_l, var_l)], axis=1)
        yn = yn * lnw_ref[...] + lnb_ref[...]
        o_ref[base:base + RW_ROWS, :] = ((yn + pre["bonus"]) * pre["gate"]).astype(o_ref.dtype)

    pre = _run(prepare(0))
    for j in range(n_sub):
        if j + 1 < n_sub:
            pre, _ = _run_interleaved(prepare(j + 1), advance(j, pre))
        else:
            _run(advance(j, pre))


def _rwkv(u_rw, shift_mix_p, w2a2, w0, a0, g2_p, k_k, k_a, r_k, ln_w, ln_b, ones_bd, tri):
    b, l_pad, _ = u_rw.shape
    rows = RW_BLOCK_ROWS
    halo_per_tile = rows // SUBLANES
    vec = lambda x: x.reshape(1, -1).astype(F32)
    const = lambda bi, i: (0, 0)
    vec_spec = pl.BlockSpec((1, RW_WIDTH), const)
    return pl.pallas_call(
        _rwkv_kernel,
        grid=(b, l_pad // rows),
        in_specs=[pl.BlockSpec((None, rows, RW_IN_PAD), lambda bi, i: (bi, i, 0)),
                  pl.BlockSpec((None, SUBLANES, RW_IN_PAD),
                               lambda bi, i: (bi, jnp.maximum(i * halo_per_tile - 1, 0), 0)),
                  pl.BlockSpec((1, RW_IN_PAD), const),
                  pl.BlockSpec(w2a2.shape, const),
                  vec_spec, vec_spec,
                  pl.BlockSpec(g2_p.shape, const),
                  vec_spec, vec_spec, vec_spec, vec_spec, vec_spec,
                  pl.BlockSpec((LANES, LANES), const),
                  pl.BlockSpec((RW_CHUNK, RW_CHUNK), const)],
        out_specs=pl.BlockSpec((None, rows, RW_WIDTH), lambda bi, i: (bi, i, 0)),
        out_shape=jax.ShapeDtypeStruct((b, l_pad, RW_WIDTH), BF16),
        scratch_shapes=[pltpu.VMEM((RW_WIDTH // LANES, LANES, LANES), F32)],
        compiler_params=pltpu.CompilerParams(dimension_semantics=("arbitrary", "arbitrary"),
                                             vmem_limit_bytes=VMEM_LIMIT_BYTES),
        name="rwkv7",
    )(u_rw, u_rw, vec(shift_mix_p), w2a2, vec(w0), vec(a0), g2_p, vec(k_k), vec(k_a), vec(r_k),
      vec(ln_w), vec(ln_b), ones_bd, tri)


def _rope_lane_tables(length):
    pos = jnp.arange(length, dtype=F32)
    inv_freq = ROPE_THETA ** (-jnp.arange(0, ROPE_DIM, 2, dtype=F32) / ROPE_DIM)
    ang = pos[:, None] * inv_freq[None, :]
    cos, sin = jnp.cos(ang), jnp.sin(ang)
    half = ROPE_DIM // 2
    rest = DA_QK_DIM - ROPE_DIM
    ones = jnp.ones((length, rest), F32)
    zeros = lambda n: jnp.zeros((length, n), F32)
    c = jnp.concatenate([cos, cos, ones], axis=1)
    s1 = jnp.concatenate([-sin, zeros(half + rest)], axis=1)
    s2 = jnp.concatenate([zeros(half), sin, zeros(rest)], axis=1)
    rep = LANES // DA_QK_DIM
    return jnp.tile(c, (1, rep)), jnp.tile(s1, (1, rep)), jnp.tile(s2, (1, rep))


def kernel(x, meta_tokens, ffn1_norm, ffn1_w_gate, ffn1_w_up, ffn1_w_down, mix_norm, w_in, da_q_norm, da_k_norm, da_lambda_q1, da_lambda_k1, da_lambda_q2, da_lambda_k2, da_subln, rw_shift_mix, rw_w0, rw_w2, rw_a0, rw_a2, rw_g2, rw_k_k, rw_k_a, rw_r_k, rw_ln_w, rw_ln_b, w_out, ffn2_norm, ffn2_w_gate, ffn2_w_up, ffn2_w_down):
    bsz, t, d = x.shape
    depth = w_in.shape[0]
    l = N_META + t
    l_pad = -(-l // Q_BLOCK) * Q_BLOCK
    assert d == D_MODEL and l_pad % ROW_TILE == 0 and l_pad % ATTN_TILE == 0 and l_pad % RW_BLOCK_ROWS == 0
    meta = jnp.broadcast_to(meta_tokens.astype(x.dtype)[None], (bsz, N_META, d))
    h = jnp.concatenate([meta, x], axis=1)
    h = jnp.pad(h, ((0, 0), (0, l_pad - l), (0, 0))).reshape(bsz * l_pad, d)

    rope_c, rope_s1, rope_s2 = _rope_lane_tables(l_pad)
    lane_head = jnp.arange(LANES) // RW_HEAD
    ones_bd = (lane_head[:, None] == lane_head[None, :]).astype(BF16)
    tri = (jnp.arange(RW_CHUNK)[:, None] >= jnp.arange(RW_CHUNK)[None, :]).astype(BF16)

    for layer in range(depth):
        lam_init = 0.8 - 0.6 * math.exp(-0.3 * layer)
        bf = lambda a: a[layer].astype(BF16)
        w_in_p = jnp.pad(w_in[layer], ((0, 0), (0, N_IN_PAD - w_in.shape[2]))).astype(BF16)
        shift_mix_p = jnp.pad(rw_shift_mix[layer], (0, RW_IN_PAD - RW_IN))
        zeros_lora = jnp.zeros((D_DECAY_LORA, RW_WIDTH), F32)
        w2a2 = jnp.concatenate([jnp.concatenate([rw_w2[layer], zeros_lora], axis=1),
                                jnp.concatenate([zeros_lora, rw_a2[layer]], axis=1)], axis=0)
        g2_p = jnp.pad(rw_g2[layer], ((0, RW_LORA_PAD - D_GATE_LORA), (0, 0))).astype(BF16)
        q_gain = jnp.tile(da_q_norm[layer], DA_QK_WIDTH // DA_QK_DIM).reshape(1, DA_QK_WIDTH)
        k_gain = jnp.tile(da_k_norm[layer], DA_QK_WIDTH // DA_QK_DIM).reshape(1, DA_QK_WIDTH)
        lam_vecs = jnp.stack([da_lambda_q1[layer], da_lambda_k1[layer],
                              da_lambda_q2[layer], da_lambda_k2[layer]]).astype(F32)

        h = _ffn(h, ffn1_norm[layer], bf(ffn1_w_gate), bf(ffn1_w_up), bf(ffn1_w_down))
        q, k, v, u_rw = _mix_in(h, mix_norm[layer], w_in_p, q_gain, k_gain,
                                rope_c, rope_s1, rope_s2, ones_bd, l_pad)
        shape3 = lambda a: a.reshape(bsz, l_pad, a.shape[-1])
        o_da = _attention(shape3(q), shape3(k), shape3(v), lam_vecs, da_subln[layer], lam_init)
        o_rw = _rwkv(shape3(u_rw), shift_mix_p, w2a2, rw_w0[layer], rw_a0[layer], g2_p,
                     rw_k_k[layer], rw_k_a[layer], rw_r_k[layer], rw_ln_w[layer], rw_ln_b[layer],
                     ones_bd, tri)
        last = layer == depth - 1
        h = _ffn(h, ffn2_norm[layer], bf(ffn2_w_gate), bf(ffn2_w_up), bf(ffn2_w_down),
                 mix=(o_da.reshape(bsz * l_pad, DA_WIDTH), o_rw.reshape(bsz * l_pad, RW_WIDTH), bf(w_out)),
                 compact=(l_pad, N_META, t) if last else None)
    return h.reshape(bsz, t, d)
```

```python
import functools
import math

import jax
import jax.numpy as jnp
from jax import lax
from jax.experimental import pallas as pl
from jax.experimental.pallas import tpu as pltpu

F32 = jnp.float32
BF16 = jnp.bfloat16

D_MODEL = 1024
N_META = 16
Q_BLOCK = 128
ROPE_THETA = 500000.0
NORM_EPS = 1e-6
DA_HEADS = 4
DA_QK_DIM = 64
DA_V_DIM = 2 * DA_QK_DIM
DA_WIDTH = DA_HEADS * DA_V_DIM
DA_QK_WIDTH = DA_HEADS * 2 * DA_QK_DIM
ROPE_DIM = DA_QK_DIM // 4
RW_HEAD = 64
RW_WIDTH = D_MODEL - DA_WIDTH
RW_HEADS = RW_WIDTH // RW_HEAD
D_DECAY_LORA = 64
D_AAA_LORA = 64
D_GATE_LORA = 160
GN_EPS = 64e-5
DA_IN = 2 * DA_QK_WIDTH + DA_WIDTH
RW_IN = 3 * RW_WIDTH + D_DECAY_LORA + D_AAA_LORA + D_GATE_LORA
D_FF = 2816

LANES = 128
SUBLANES = 8
SEG_LANES = 256
VMEM_LIMIT_BYTES = 56 * 1024 * 1024

ROW_TILE = 640
COMPACT_ROW_TILE = 512
FF_CHUNK = 256
ATTN_TILE = 640
ATTN_HEADS_PER_STEP = 2
ATTN_SCORE_LEAD = 2
RW_CHUNK = 64
RW_ROWS = 2 * RW_CHUNK
RW_BLOCK_ROWS = 5 * RW_ROWS
RW_LORA_PAD = 2 * LANES
RW_IN_PAD = 3 * RW_WIDTH + LANES + RW_LORA_PAD
N_IN_PAD = DA_IN + RW_IN_PAD

_NT = (((1,), (1,)), ((), ()))
_TN = (((0,), (0,)), ((), ()))


def _dot(a, b):
    return jnp.dot(a, b, preferred_element_type=F32)


def _dot_nt(a, b):
    return lax.dot_general(a, b, _NT, preferred_element_type=F32)


def _dot_tn(a, b):
    return lax.dot_general(a, b, _TN, preferred_element_type=F32)


def _split2(x):
    hi = x.astype(BF16)
    lo = (x - hi.astype(F32)).astype(BF16)
    return hi, lo


def _dot_split(a, b):
    a_hi, a_lo = _split2(a)
    b_hi, b_lo = _split2(b)
    return _dot(a_hi, b_hi) + (_dot(a_hi, b_lo) + _dot(a_lo, b_hi))


def _seg_sum(x, ones_bd):
    hi, lo = _split2(x)
    return _dot(hi, ones_bd) + _dot(lo, ones_bd)


def _rms_rows(h, gain):
    ms = jnp.mean(h * h, axis=-1, keepdims=True)
    return h * lax.rsqrt(ms + NORM_EPS) * gain


def _ffn_kernel(*refs, has_mix):
    if has_mix:
        (h_ref, oda_ref, orw_ref, wout_ref, g_ref, wg_ref, wu_ref, wd_ref, o_ref, xn_ref, act_ref) = refs
        h = (h_ref[...] + _dot(oda_ref[...], wout_ref[0:DA_WIDTH, :])
             + _dot(orw_ref[...], wout_ref[DA_WIDTH:DA_WIDTH + RW_WIDTH, :]))
        o_ref[...] = h
        res_ref = o_ref
    else:
        (h_ref, g_ref, wg_ref, wu_ref, wd_ref, o_ref, xn_ref, act_ref) = refs
        h = h_ref[...]
        res_ref = h_ref
    xn_ref[...] = _rms_rows(h, g_ref[...]).astype(BF16)
    d_ff = wg_ref.shape[1]
    for c in range(d_ff // FF_CHUNK):
        cols = slice(c * FF_CHUNK, (c + 1) * FF_CHUNK)
        g = _dot(xn_ref[...], wg_ref[:, cols])
        u = _dot(xn_ref[...], wu_ref[:, cols])
        act_ref[:, cols] = ((g * jax.nn.sigmoid(g)) * u).astype(BF16)
    o_ref[...] = res_ref[...] + 0.5 * _dot(act_ref[...], wd_ref[...])


def _ffn(h, norm_g, wg, wu, wd, mix=None, compact=None):
    m, d = h.shape
    d_ff = wg.shape[1]
    row = lambda i: (i, 0)
    const2 = lambda i: (0, 0)
    if compact is None:
        tm = ROW_TILE
        m_out = m
        row_spec = lambda width: pl.BlockSpec((tm, width), row)
    else:
        l_pad, first_row, n_rows = compact
        tm = COMPACT_ROW_TILE
        align = 2 * SUBLANES
        assert n_rows % tm == 0 and first_row % align == 0 and l_pad % align == 0 and tm % align == 0
        tiles = n_rows // tm
        m_out = (m // l_pad) * n_rows
        window = lambda i: (pl.multiple_of((i // tiles) * l_pad + first_row + (i % tiles) * tm, align), 0)
        row_spec = lambda width: pl.BlockSpec((pl.Element(tm), pl.Element(width)), window)
    in_specs = [row_spec(d)]
    args = [h]
    if mix is not None:
        o_da, o_rw, w_out = mix
        in_specs += [row_spec(DA_WIDTH), row_spec(RW_WIDTH), pl.BlockSpec(w_out.shape, const2)]
        args += [o_da, o_rw, w_out]
    in_specs += [pl.BlockSpec((1, d), const2),
                 pl.BlockSpec(wg.shape, const2, pipeline_mode=pl.Buffered(1)),
                 pl.BlockSpec(wu.shape, const2, pipeline_mode=pl.Buffered(1)),
                 pl.BlockSpec(wd.shape, const2, pipeline_mode=pl.Buffered(1))]
    args += [norm_g.reshape(1, d), wg, wu, wd]
    return pl.pallas_call(
        functools.partial(_ffn_kernel, has_mix=mix is not None),
        grid=(m_out // tm,),
        in_specs=in_specs,
        out_specs=pl.BlockSpec((tm, d), row),
        out_shape=jax.ShapeDtypeStruct((m_out, d), F32),
        scratch_shapes=[pltpu.VMEM((tm, d), BF16), pltpu.VMEM((tm, d_ff), BF16)],
        compiler_params=pltpu.CompilerParams(dimension_semantics=("arbitrary",),
                                             vmem_limit_bytes=VMEM_LIMIT_BYTES),
        name="ffn_mix" if mix is not None else "ffn",
    )(*args)


def _qk_prep(u, gain_ref, c, s1, s2, ones_bd, scale, out_ref):
    rep = SEG_LANES // LANES
    c, s1, s2 = (jnp.concatenate([t] * rep, axis=1) for t in (c, s1, s2))
    for j in range(DA_QK_WIDTH // SEG_LANES):
        sl = slice(SEG_LANES * j, SEG_LANES * (j + 1))
        x = u[:, sl]
        ss = _seg_sum(x * x, ones_bd)
        xn = x * lax.rsqrt(ss * (1.0 / DA_QK_DIM) + NORM_EPS) * gain_ref[:, sl]
        half = ROPE_DIM // 2
        xr = xn * c + pltpu.roll(xn, SEG_LANES - half, 1) * s1 + pltpu.roll(xn, half, 1) * s2
        out_ref[:, sl] = (xr * scale).astype(BF16)


def _mixin_kernel(h_ref, g_ref, win_ref, qg_ref, kg_ref, c_ref, s1_ref, s2_ref, ones_ref,
                  q_ref, k_ref, v_ref, urw_ref, *, q_scale):
    xn = _rms_rows(h_ref[...], g_ref[...]).astype(BF16)
    qw = DA_QK_WIDTH
    c, s1, s2, ones_bd = c_ref[...], s1_ref[...], s2_ref[...], ones_ref[...]
    uq = _dot(xn, win_ref[:, 0:qw])
    uk = _dot(xn, win_ref[:, qw:2 * qw])
    _qk_prep(uq, qg_ref, c, s1, s2, ones_bd, q_scale, q_ref)
    _qk_prep(uk, kg_ref, c, s1, s2, ones_bd, 1.0, k_ref)
    v_ref[...] = _dot(xn, win_ref[:, 2 * qw:DA_IN]).astype(BF16)
    urw_ref[...] = _dot(xn, win_ref[:, DA_IN:N_IN_PAD])


def _mix_in(h, norm_g, w_in_p, q_gain, k_gain, rope_c, rope_s1, rope_s2, ones_bd, l_pad):
    m, d = h.shape
    tm = ROW_TILE
    tiles_per_seq = l_pad // tm
    row = lambda i: (i, 0)
    pos = lambda i: (i % tiles_per_seq, 0)
    const2 = lambda i: (0, 0)
    q_scale = DA_QK_DIM ** -0.5 * math.log2(math.e)
    return pl.pallas_call(
        functools.partial(_mixin_kernel, q_scale=q_scale),
        grid=(m // tm,),
        in_specs=[pl.BlockSpec((tm, d), row), pl.BlockSpec((1, d), const2),
                  pl.BlockSpec(w_in_p.shape, const2, pipeline_mode=pl.Buffered(1)),
                  pl.BlockSpec((1, DA_QK_WIDTH), const2), pl.BlockSpec((1, DA_QK_WIDTH), const2),
                  pl.BlockSpec((tm, LANES), pos), pl.BlockSpec((tm, LANES), pos),
                  pl.BlockSpec((tm, LANES), pos), pl.BlockSpec((SEG_LANES, SEG_LANES), const2)],
        out_specs=[pl.BlockSpec((tm, DA_QK_WIDTH), row), pl.BlockSpec((tm, DA_QK_WIDTH), row),
                   pl.BlockSpec((tm, DA_WIDTH), row), pl.BlockSpec((tm, RW_IN_PAD), row)],
        out_shape=[jax.ShapeDtypeStruct((m, DA_QK_WIDTH), BF16), jax.ShapeDtypeStruct((m, DA_QK_WIDTH), BF16),
                   jax.ShapeDtypeStruct((m, DA_WIDTH), BF16), jax.ShapeDtypeStruct((m, RW_IN_PAD), F32)],
        compiler_params=pltpu.CompilerParams(dimension_semantics=("arbitrary",),
                                             vmem_limit_bytes=VMEM_LIMIT_BYTES),
        name="mix_in",
    )(h, norm_g.reshape(1, d), w_in_p, q_gain, k_gain, rope_c, rope_s1, rope_s2, ones_bd)


def _attn_kernel(lam_ref, q_ref, k_ref, v_ref, sub_ref, o_ref, q2_s, vx_s, m_s, l_s, acc_s, s_s, *, tq, lam_init):
    qi = pl.program_id(2)
    n_heads = q_ref.shape[1] // LANES
    head_lanes = [slice(LANES * hh, LANES * (hh + 1)) for hh in range(n_heads)]
    vx_lanes = [slice(2 * LANES * hh, 2 * LANES * (hh + 1)) for hh in range(n_heads)]
    streams = [(hh, slice((2 * hh + c) * tq, (2 * hh + c + 1) * tq)) for hh in range(n_heads) for c in range(2)]

    @pl.when(qi == 0)
    def _():
        for hh in range(n_heads):
            vx_s[:, 2 * LANES * hh:2 * LANES * hh + LANES] = v_ref[:, head_lanes[hh]]
            vx_s[:, 2 * LANES * hh + LANES:2 * LANES * (hh + 1)] = jnp.ones((vx_s.shape[0], LANES), BF16)

    for hh in range(n_heads):
        q = q_ref[:, head_lanes[hh]]
        lane = lax.broadcasted_iota(jnp.int32, q.shape, 1)
        zero = jnp.zeros_like(q)
        q2_s[streams[2 * hh][1], :] = jnp.where(lane < DA_QK_DIM, q, zero)
        q2_s[streams[2 * hh + 1][1], :] = jnp.where(lane < DA_QK_DIM, zero, q)
    m_s[...] = jnp.full(m_s.shape, -1e30, F32)
    l_s[...] = jnp.zeros_like(l_s)
    acc_s[...] = jnp.zeros_like(acc_s)

    def process(start, width, diag_offset):
        def scores(st):
            hh, rows = streams[st]
            s_s[rows, 0:width] = _dot_nt(q2_s[rows, :], k_ref[pl.ds(start, width), head_lanes[hh]])

        lead = min(ATTN_SCORE_LEAD, len(streams))
        for st in range(lead):
            scores(st)
        for st, (hh, rows) in enumerate(streams):
            s = s_s[rows, 0:width]
            if diag_offset is not None:
                row = lax.broadcasted_iota(jnp.int32, s.shape, 0)
                col = lax.broadcasted_iota(jnp.int32, s.shape, 1)
                s = jnp.where(col <= row + diag_offset, s, -jnp.inf)
            m_prev = m_s[rows, :]
            m_new = jnp.maximum(m_prev, jnp.max(s, axis=1, keepdims=True))
            alpha = jnp.exp2(m_prev - m_new)
            p = jnp.exp2(s - jnp.concatenate([m_new] * (width // LANES), axis=1))
            pv = _dot(p.astype(BF16), vx_s[pl.ds(start, width), vx_lanes[hh]])
            acc_s[rows, :] = acc_s[rows, :] * alpha + pv[:, 0:LANES]
            l_s[rows, :] = l_s[rows, :] * alpha + pv[:, LANES:2 * LANES]
            m_s[rows, :] = m_new
            if st + lead < len(streams):
                scores(st + lead)

    def wide_body(t, carry):
        process(pl.multiple_of(t * (2 * tq), 2 * tq), 2 * tq, None)
        return carry

    lax.fori_loop(0, qi // 2, wide_body, 0)

    @pl.when(qi % 2 == 0)
    def _():
        process(pl.multiple_of(qi * tq, tq), tq, 0)

    @pl.when(qi % 2 == 1)
    def _():
        process(pl.multiple_of((qi - 1) * tq, tq), 2 * tq, tq)

    lam1 = jnp.exp(jnp.sum(lam_ref[0:1, :] * lam_ref[1:2, :], axis=1, keepdims=True))
    lam2 = jnp.exp(jnp.sum(lam_ref[2:3, :] * lam_ref[3:4, :], axis=1, keepdims=True))
    lam = lam1 - lam2 + lam_init
    for hh in range(n_heads):
        r0, r1 = streams[2 * hh][1], streams[2 * hh + 1][1]
        o = acc_s[r0, :] / l_s[r0, :] - lam * (acc_s[r1, :] / l_s[r1, :])
        o = _rms_rows(o, sub_ref[...]) * (1.0 - lam_init)
        o_ref[:, head_lanes[hh]] = o.astype(o_ref.dtype)


def _attention(q, k, v, lam_vecs, subln, lam_init):
    b, l_pad, _ = q.shape
    tq = ATTN_TILE
    nq = l_pad // tq
    nh = ATTN_HEADS_PER_STEP
    hw = nh * LANES
    n_streams = 2 * nh
    kv_spec = pl.BlockSpec((None, l_pad, hw), lambda bi, h, qi: (bi, 0, h), pipeline_mode=pl.Buffered(1))
    return pl.pallas_call(
        functools.partial(_attn_kernel, tq=tq, lam_init=lam_init),
        grid=(b, DA_HEADS // nh, nq),
        in_specs=[pl.BlockSpec(lam_vecs.shape, lambda bi, h, qi: (0, 0)),
                  pl.BlockSpec((None, tq, hw), lambda bi, h, qi: (bi, qi, h)),
                  kv_spec, kv_spec,
                  pl.BlockSpec((1, DA_V_DIM), lambda bi, h, qi: (0, 0))],
        out_specs=pl.BlockSpec((None, tq, hw), lambda bi, h, qi: (bi, qi, h)),
        out_shape=jax.ShapeDtypeStruct((b, l_pad, DA_WIDTH), BF16),
        scratch_shapes=[pltpu.VMEM((n_streams * tq, LANES), BF16), pltpu.VMEM((l_pad, 2 * hw), BF16),
                        pltpu.VMEM((n_streams * tq, LANES), F32), pltpu.VMEM((n_streams * tq, LANES), F32),
                        pltpu.VMEM((n_streams * tq, LANES), F32),
                        pltpu.VMEM((n_streams * tq, 2 * tq), F32)],
        compiler_params=pltpu.CompilerParams(dimension_semantics=("arbitrary", "arbitrary", "arbitrary"),
                                             vmem_limit_bytes=VMEM_LIMIT_BYTES),
        name="diff_attention",
    )(lam_vecs, q, k, v, subln.reshape(1, DA_V_DIM))


def _expand_heads(x, lo_mask):
    zero = jnp.zeros_like(x)
    return jnp.concatenate([jnp.where(lo_mask, x, zero), jnp.where(lo_mask, zero, x)], axis=0)


def _run(stages):
    try:
        while True:
            next(stages)
    except StopIteration as stop:
        return stop.value


def _run_interleaved(stages_a, stages_b):
    live = [stages_a, stages_b]
    results = [None, None]
    while live[0] is not None or live[1] is not None:
        for n in range(2):
            if live[n] is None:
                continue
            try:
                next(live[n])
            except StopIteration as stop:
                results[n] = stop.value
                live[n] = None
    return results


def _unit_lower_inverse(a_list):
    n = a_list[0].shape[0]
    eye = (lax.broadcasted_iota(jnp.int32, (n, n), 0) == lax.broadcasted_iota(jnp.int32, (n, n), 1)).astype(F32)
    mm = lambda x, y: _dot(x.astype(BF16), y.astype(BF16))
    t_list = [eye + a for a in a_list]
    p_list = [mm(a, a) for a in a_list]
    yield
    levels = int(math.log2(RW_CHUNK))
    for _ in range(levels - 2):
        pt_list = [mm(p, jnp.concatenate([p, t], axis=1)) for p, t in zip(p_list, t_list)]
        t_list = [t + pt[:, n:] for t, pt in zip(t_list, pt_list)]
        p_list = [pt[:, :n] for pt in pt_list]
        yield
    return [t + mm(p, t) for p, t in zip(p_list, t_list)]


_RW_CARRY = ("ar", "bke", "v2", "t_inv", "akv", "arbk", "g_end", "bonus", "gate")


def _rwkv_kernel(u_ref, next_ref, mix_ref, w2a2_ref, w0_ref, a0_ref, g2_ref, kk_ref, ka_ref, rk_ref,
                 lnw_ref, lnb_ref, ones_ref, tri_ref, o_ref, s_ref, *carry_refs):
    i = pl.program_id(1)
    carry = dict(zip(_RW_CARRY, carry_refs))
    c_len = RW_CHUNK
    w = RW_WIDTH
    n2 = 2 * c_len
    n_sub = u_ref.shape[0] // RW_ROWS
    n_chunks = RW_ROWS // c_len
    n_pairs = w // LANES
    pair_lanes = [slice(LANES * p, LANES * (p + 1)) for p in range(n_pairs)]
    seg_lanes = [slice(SEG_LANES * p, SEG_LANES * (p + 1)) for p in range(w // SEG_LANES)]

    @pl.when(i == 0)
    def _():
        s_ref[...] = jnp.zeros_like(s_ref)

    ones_bd = ones_ref[...]
    tri = tri_ref[...]
    r_i = lax.broadcasted_iota(jnp.int32, (n2, n2), 0)
    c_i = lax.broadcasted_iota(jnp.int32, (n2, n2), 1)
    t_row = jnp.where(r_i >= c_len, r_i - c_len, r_i)
    t_col = jnp.where(c_i >= c_len, c_i - c_len, c_i)
    strict = t_col < t_row
    incl = t_col <= t_row
    lo_mask = lax.broadcasted_iota(jnp.int32, (c_len, LANES), 1) < RW_HEAD

    def prepare(src_ref, base, prev_row):
        def shifted(lo, hi):
            u = src_ref[base:base + RW_ROWS, lo:hi]
            last_prev = prev_row(lo, hi)
            rolled = pltpu.roll(u, 1, 0)
            row = lax.broadcasted_iota(jnp.int32, u.shape, 0)
            prev = jnp.where(row == 0, last_prev, rolled)
            return u + (prev - u) * mix_ref[:, lo:hi]

        r = shifted(0, w)
        k = shifted(w, 2 * w)
        v = shifted(2 * w, 3 * w)
        lora_in = shifted(3 * w, 3 * w + LANES)
        g_in = shifted(3 * w + LANES, RW_IN_PAD)

        lane = lax.broadcasted_iota(jnp.int32, lora_in.shape, 1)
        lora_act = jnp.where(lane < D_DECAY_LORA, jnp.tanh(lora_in), lora_in)
        wa = _dot_split(lora_act, w2a2_ref[...])
        g = _dot(jax.nn.sigmoid(g_in).astype(BF16), g2_ref[...])
        yield
        w_pre = w0_ref[...] + wa[:, 0:w]
        lw = (-math.exp(-0.5)) * jax.nn.sigmoid(w_pre)
        a = jax.nn.sigmoid(a0_ref[...] + wa[:, w:2 * w])

        cums = []
        for c in range(n_chunks):
            lw_c = lw[c * c_len:(c + 1) * c_len]
            hi = lw_c.astype(BF16)
            mid_f = lw_c - hi.astype(F32)
            mid = mid_f.astype(BF16)
            lo = (mid_f - mid.astype(F32)).astype(BF16)
            cums.append(_dot(tri, hi) + (_dot(tri, mid) + _dot(tri, lo)))

        kk_raw = k * kk_ref[...]
        k_mod = k * (1.0 + (a - 1.0) * ka_ref[...])
        rkr = r * k_mod * rk_ref[...]
        ss_l = [_seg_sum(kk_raw[:, sl] * kk_raw[:, sl], ones_bd) for sl in seg_lanes]
        rk_l = [_seg_sum(rkr[:, sl], ones_bd) for sl in seg_lanes]
        yield
        kk = jnp.concatenate([kk_raw[:, sl] * lax.rsqrt(jnp.maximum(ss, 1e-24))
                              for sl, ss in zip(seg_lanes, ss_l)], axis=1)
        bonus = jnp.concatenate([s * v[:, sl] for sl, s in zip(seg_lanes, rk_l)], axis=1)
        a_neg = -kk
        b_vec = kk * a

        ar_l, bk_l, bke_l, v2_l, g_end_l = [], [], [], [], []
        for c in range(n_chunks):
            rs = slice(c * c_len, (c + 1) * c_len)
            cum = cums[c]
            cum_end = cum[c_len - 1:c_len, :]
            e_in = jnp.exp(cum)
            e_out = jnp.exp(-cum)
            e_end = jnp.exp(cum_end - cum)
            r_t = r[rs] * e_in
            a_t = a_neg[rs] * jnp.exp(cum - lw[rs])
            k_t = k_mod[rs] * e_out
            b_t = b_vec[rs] * e_out
            k_e = k_mod[rs] * e_end
            b_e = b_vec[rs] * e_end
            g_end_l.append(jnp.exp(cum_end))
            v_c = v[rs]
            for sl in pair_lanes:
                ar_l.append(jnp.concatenate([_expand_heads(a_t[:, sl], lo_mask), _expand_heads(r_t[:, sl], lo_mask)],
                                            axis=0).astype(BF16))
                bk_l.append(jnp.concatenate([_expand_heads(b_t[:, sl], lo_mask), _expand_heads(k_t[:, sl], lo_mask)],
                                            axis=0).astype(BF16))
                bke_l.append(jnp.concatenate([_expand_heads(b_e[:, sl], lo_mask), _expand_heads(k_e[:, sl], lo_mask)],
                                             axis=0).astype(BF16))
                v2_l.append(_expand_heads(v_c[:, sl], lo_mask).astype(BF16))
        aa_l = [_dot_nt(ar, bk) for ar, bk in zip(ar_l, bk_l)]
        yield
        akv_l = [_dot(jnp.where(strict, aa[0:n2, n2:2 * n2], 0.0).astype(BF16), v2) for aa, v2 in zip(aa_l, v2_l)]
        arbk_l = [jnp.concatenate([jnp.where(incl, aa[n2:2 * n2, 0:n2], 0.0),
                                   jnp.where(incl, aa[n2:2 * n2, n2:2 * n2], 0.0)], axis=1).astype(BF16)
                  for aa in aa_l]
        t_inv_l = yield from _unit_lower_inverse([jnp.where(strict, aa[0:n2, 0:n2], 0.0) for aa in aa_l])
        t_inv_l = [t.astype(BF16) for t in t_inv_l]
        yield
        return dict(ar=ar_l, bke=bke_l, v2=v2_l, g_end=g_end_l, t_inv=t_inv_l, akv=akv_l, arbk=arbk_l,
                    bonus=bonus, gate=g)

    def advance(j, pre):
        base = j * RW_ROWS
        y_chunks = []
        for c in range(n_chunks):
            idx = [c * n_pairs + p for p in range(n_pairs)]
            states = [s_ref[p] for p in range(n_pairs)]
            arh = [_dot_nt(pre["ar"][q], s.astype(BF16)) for q, s in zip(idx, states)]
            yield
            u_l = [_dot(pre["t_inv"][q], (h[0:n2] + pre["akv"][q]).astype(BF16)) for q, h in zip(idx, arh)]
            yield
            uv = [jnp.concatenate([u.astype(BF16), pre["v2"][q]], axis=0) for q, u in zip(idx, u_l)]
            y2 = [h[n2:2 * n2] + _dot(pre["arbk"][q], x) for q, h, x in zip(idx, arh, uv)]
            for p in range(n_pairs):
                s_ref[p] = states[p] * pre["g_end"][c][:, pair_lanes[p]] + _dot_tn(uv[p], pre["bke"][idx[p]])
            yield
            y_chunks.append(jnp.concatenate([y[0:c_len] + y[c_len:n2] for y in y2], axis=1))
        y = jnp.concatenate(y_chunks, axis=0)

        mu_l = [_seg_sum(y[:, sl], ones_bd) * (1.0 / RW_HEAD) for sl in seg_lanes]
        yield
        d_l = [y[:, sl] - mu for sl, mu in zip(seg_lanes, mu_l)]
        var_l = [_seg_sum(d * d, ones_bd) * (1.0 / RW_HEAD) for d in d_l]
        yield
        yn = jnp.concatenate([d * lax.rsqrt(var + GN_EPS) for d, var in zip(d_l, var_l)], axis=1)
        yn = yn * lnw_ref[...] + lnb_ref[...]
        o_ref[base:base + RW_ROWS, :] = ((yn + pre["bonus"]) * pre["gate"]).astype(o_ref.dtype)

    def save(pre):
        for name in _RW_CARRY:
            value, ref = pre[name], carry[name]
            if isinstance(value, list):
                for q, item in enumerate(value):
                    ref[q] = item
            else:
                ref[...] = value

    def load():
        return {name: ([ref[q] for q in range(ref.shape[0])] if len(ref.shape) == 3 else ref[...])
                for name, ref in carry.items()}

    def row_before(src_ref, row):
        return lambda lo, hi: src_ref[row:row + 1, lo:hi]

    @pl.when(i == 0)
    def _():
        save(_run(prepare(u_ref, 0, lambda lo, hi: jnp.zeros((1, hi - lo), F32))))

    pre = load()
    for j in range(n_sub):
        last_row = (j + 1) * RW_ROWS - 1
        if j + 1 < n_sub:
            upcoming = prepare(u_ref, (j + 1) * RW_ROWS, row_before(u_ref, last_row))
        else:
            upcoming = prepare(next_ref, 0, row_before(u_ref, last_row))
        pre, _ = _run_interleaved(upcoming, advance(j, pre))
    save(pre)


def _rwkv(u_rw, shift_mix_p, w2a2, w0, a0, g2_p, k_k, k_a, r_k, ln_w, ln_b, ones_bd, tri):
    b, l_pad, _ = u_rw.shape
    rows = RW_BLOCK_ROWS
    sub_per_block = rows // RW_ROWS
    last_sub = l_pad // RW_ROWS - 1
    n_prob = (RW_ROWS // RW_CHUNK) * (RW_WIDTH // LANES)
    n2 = 2 * RW_CHUNK
    vec = lambda x: x.reshape(1, -1).astype(F32)
    const = lambda bi, i: (0, 0)
    vec_spec = pl.BlockSpec((1, RW_WIDTH), const)
    carry_shapes = dict(
        ar=pltpu.VMEM((n_prob, 2 * n2, LANES), BF16), bke=pltpu.VMEM((n_prob, 2 * n2, LANES), BF16),
        v2=pltpu.VMEM((n_prob, n2, LANES), BF16), t_inv=pltpu.VMEM((n_prob, n2, n2), BF16),
        akv=pltpu.VMEM((n_prob, n2, LANES), F32), arbk=pltpu.VMEM((n_prob, n2, 2 * n2), BF16),
        g_end=pltpu.VMEM((RW_ROWS // RW_CHUNK, 1, RW_WIDTH), F32),
        bonus=pltpu.VMEM((RW_ROWS, RW_WIDTH), F32), gate=pltpu.VMEM((RW_ROWS, RW_WIDTH), F32))
    return pl.pallas_call(
        _rwkv_kernel,
        grid=(b, l_pad // rows),
        in_specs=[pl.BlockSpec((None, rows, RW_IN_PAD), lambda bi, i: (bi, i, 0)),
                  pl.BlockSpec((None, RW_ROWS, RW_IN_PAD),
                               lambda bi, i: (bi, jnp.minimum((i + 1) * sub_per_block, last_sub), 0)),
                  pl.BlockSpec((1, RW_IN_PAD), const),
                  pl.BlockSpec(w2a2.shape, const),
                  vec_spec, vec_spec,
                  pl.BlockSpec(g2_p.shape, const),
                  vec_spec, vec_spec, vec_spec, vec_spec, vec_spec,
                  pl.BlockSpec((SEG_LANES, SEG_LANES), const),
                  pl.BlockSpec((RW_CHUNK, RW_CHUNK), const)],
        out_specs=pl.BlockSpec((None, rows, RW_WIDTH), lambda bi, i: (bi, i, 0)),
        out_shape=jax.ShapeDtypeStruct((b, l_pad, RW_WIDTH), BF16),
        scratch_shapes=[pltpu.VMEM((RW_WIDTH // LANES, LANES, LANES), F32)] + [carry_shapes[n] for n in _RW_CARRY],
        compiler_params=pltpu.CompilerParams(dimension_semantics=("arbitrary", "arbitrary"),
                                             vmem_limit_bytes=VMEM_LIMIT_BYTES),
        name="rwkv7",
    )(u_rw, u_rw, vec(shift_mix_p), w2a2, vec(w0), vec(a0), g2_p, vec(k_k), vec(k_a), vec(r_k),
      vec(ln_w), vec(ln_b), ones_bd, tri)


def _rope_lane_tables(length):
    pos = jnp.arange(length, dtype=F32)
    inv_freq = ROPE_THETA ** (-jnp.arange(0, ROPE_DIM, 2, dtype=F32) / ROPE_DIM)
    ang = pos[:, None] * inv_freq[None, :]
    cos, sin = jnp.cos(ang), jnp.sin(ang)
    half = ROPE_DIM // 2
    rest = DA_QK_DIM - ROPE_DIM
    ones = jnp.ones((length, rest), F32)
    zeros = lambda n: jnp.zeros((length, n), F32)
    c = jnp.concatenate([cos, cos, ones], axis=1)
    s1 = jnp.concatenate([-sin, zeros(half + rest)], axis=1)
    s2 = jnp.concatenate([zeros(half), sin, zeros(rest)], axis=1)
    rep = LANES // DA_QK_DIM
    return jnp.tile(c, (1, rep)), jnp.tile(s1, (1, rep)), jnp.tile(s2, (1, rep))


def kernel(x, meta_tokens, ffn1_norm, ffn1_w_gate, ffn1_w_up, ffn1_w_down, mix_norm, w_in, da_q_norm, da_k_norm, da_lambda_q1, da_lambda_k1, da_lambda_q2, da_lambda_k2, da_subln, rw_shift_mix, rw_w0, rw_w2, rw_a0, rw_a2, rw_g2, rw_k_k, rw_k_a, rw_r_k, rw_ln_w, rw_ln_b, w_out, ffn2_norm, ffn2_w_gate, ffn2_w_up, ffn2_w_down):
    bsz, t, d = x.shape
    depth = w_in.shape[0]
    l = N_META + t
    l_pad = -(-l // Q_BLOCK) * Q_BLOCK
    assert d == D_MODEL and l_pad % ROW_TILE == 0 and l_pad % ATTN_TILE == 0 and l_pad % RW_BLOCK_ROWS == 0
    meta = jnp.broadcast_to(meta_tokens.astype(x.dtype)[None], (bsz, N_META, d))
    h = jnp.concatenate([meta, x], axis=1)
    h = jnp.pad(h, ((0, 0), (0, l_pad - l), (0, 0))).reshape(bsz * l_pad, d)

    rope_c, rope_s1, rope_s2 = _rope_lane_tables(l_pad)
    lane_head = jnp.arange(SEG_LANES) // RW_HEAD
    ones_bd =(lane_head[:, None] == lane_head[None, :]).astype(BF16)
    tri = (jnp.arange(RW_CHUNK)[:, None] >= jnp.arange(RW_CHUNK)[None, :]).astype(BF16)

    for layer in range(depth):
        lam_init = 0.8 - 0.6 * math.exp(-0.3 * layer)
        bf = lambda a: a[layer].astype(BF16)
        w_in_p = jnp.pad(w_in[layer], ((0, 0), (0, N_IN_PAD - w_in.shape[2]))).astype(BF16)
        shift_mix_p = jnp.pad(rw_shift_mix[layer], (0, RW_IN_PAD - RW_IN))
        zeros_lora = jnp.zeros((D_DECAY_LORA, RW_WIDTH), F32)
        w2a2 = jnp.concatenate([jnp.concatenate([rw_w2[layer], zeros_lora], axis=1),
                                jnp.concatenate([zeros_lora, rw_a2[layer]], axis=1)], axis=0)
        g2_p = jnp.pad(rw_g2[layer], ((0, RW_LORA_PAD - D_GATE_LORA), (0, 0))).astype(BF16)
        q_gain = jnp.tile(da_q_norm[layer], DA_QK_WIDTH // DA_QK_DIM).reshape(1, DA_QK_WIDTH)
        k_gain = jnp.tile(da_k_norm[layer], DA_QK_WIDTH // DA_QK_DIM).reshape(1, DA_QK_WIDTH)
        lam_vecs = jnp.stack([da_lambda_q1[layer], da_lambda_k1[layer],
                              da_lambda_q2[layer], da_lambda_k2[layer]]).astype(F32)

        h = _ffn(h, ffn1_norm[layer], bf(ffn1_w_gate), bf(ffn1_w_up), bf(ffn1_w_down))
        q, k, v, u_rw = _mix_in(h, mix_norm[layer], w_in_p, q_gain, k_gain,
                                rope_c, rope_s1, rope_s2, ones_bd, l_pad)
        shape3 = lambda a: a.reshape(bsz, l_pad, a.shape[-1])
        o_da = _attention(shape3(q), shape3(k), shape3(v), lam_vecs, da_subln[layer], lam_init)
        o_rw = _rwkv(shape3(u_rw), shift_mix_p, w2a2, rw_w0[layer], rw_a0[layer], g2_p,
                     rw_k_k[layer], rw_k_a[layer], rw_r_k[layer], rw_ln_w[layer], rw_ln_b[layer],
                     ones_bd, tri)
        last = layer == depth - 1
        h = _ffn(h, ffn2_norm[layer], bf(ffn2_w_gate), bf(ffn2_w_up), bf(ffn2_w_down),
                 mix=(o_da.reshape(bsz * l_pad, DA_WIDTH), o_rw.reshape(bsz * l_pad, RW_WIDTH), bf(w_out)),
                 compact=(l_pad, N_META, t) if last else None)
    return h.reshape(bsz, t, d)
```

```python
import functools
import math

import jax
import jax.numpy as jnp
from jax import lax
from jax.experimental import pallas as pl
from jax.experimental.pallas import tpu as pltpu

F32 = jnp.float32
BF16 = jnp.bfloat16

D_MODEL = 1024
N_META = 16
Q_BLOCK = 128
ROPE_THETA = 500000.0
NORM_EPS = 1e-6
DA_HEADS = 4
DA_QK_DIM = 64
DA_V_DIM = 2 * DA_QK_DIM
DA_WIDTH = DA_HEADS * DA_V_DIM
DA_QK_WIDTH = DA_HEADS * 2 * DA_QK_DIM
ROPE_DIM = DA_QK_DIM // 4
RW_HEAD = 64
RW_WIDTH = D_MODEL - DA_WIDTH
RW_HEADS = RW_WIDTH // RW_HEAD
D_DECAY_LORA = 64
D_AAA_LORA = 64
D_GATE_LORA = 160
GN_EPS = 64e-5
DA_IN = 2 * DA_QK_WIDTH + DA_WIDTH
RW_IN = 3 * RW_WIDTH + D_DECAY_LORA + D_AAA_LORA + D_GATE_LORA
D_FF = 2816

LANES = 128
SUBLANES = 8
SEG_LANES = 256
VMEM_LIMIT_BYTES = 56 * 1024 * 1024

ROW_TILE = 640
COMPACT_ROW_TILE = 512
FF_CHUNK = 256
ATTN_TILE = 640
ATTN_HEADS_PER_STEP = 2
ATTN_SCORE_LEAD = 2
RW_CHUNK = 64
RW_ROWS = 2 * RW_CHUNK
RW_BLOCK_ROWS = 5 * RW_ROWS
RW_LORA_PAD = 2 * LANES
RW_IN_PAD = 3 * RW_WIDTH + LANES + RW_LORA_PAD
N_IN_PAD = DA_IN + RW_IN_PAD

_NT = (((1,), (1,)), ((), ()))
_TN = (((0,), (0,)), ((), ()))


def _dot(a, b):
    return jnp.dot(a, b, preferred_element_type=F32)


def _dot_nt(a, b):
    return lax.dot_general(a, b, _NT, preferred_element_type=F32)


def _dot_tn(a, b):
    return lax.dot_general(a, b, _TN, preferred_element_type=F32)


def _split2(x):
    hi = x.astype(BF16)
    lo = (x - hi.astype(F32)).astype(BF16)
    return hi, lo


def _dot_split(a, b):
    a_hi, a_lo = _split2(a)
    b_hi, b_lo = _split2(b)
    return _dot(a_hi, b_hi) + (_dot(a_hi, b_lo) + _dot(a_lo, b_hi))


def _seg_sum(x, ones_bd):
    hi, lo = _split2(x)
    return _dot(hi, ones_bd) + _dot(lo, ones_bd)


def _rms_rows(h, gain):
    ms = jnp.mean(h * h, axis=-1, keepdims=True)
    return h * lax.rsqrt(ms + NORM_EPS) * gain


def _ffn_kernel(*refs, has_mix, assemble):
    if has_mix:
        (h_ref, oda_ref, orw_ref, wout_ref, g_ref, wg_ref, wu_ref, wd_ref, o_ref, xn_ref, act_ref) = refs
        h = (h_ref[...] + _dot(oda_ref[...], wout_ref[0:DA_WIDTH, :])
             + _dot(orw_ref[...], wout_ref[DA_WIDTH:DA_WIDTH + RW_WIDTH, :]))
        o_ref[...] = h
        res_ref = o_ref
    elif assemble is not None:
        (x_ref, meta_ref, g_ref, wg_ref, wu_ref, wd_ref, o_ref, xn_ref, act_ref) = refs
        tiles, n_prefix, last_shift, last_valid = assemble
        tm, d = x_ref.shape
        il = pl.program_id(0) % tiles
        x = x_ref[...]
        first, last = il == 0, il == tiles - 1
        h = jnp.where(first, pltpu.roll(x, n_prefix, 0), jnp.where(last, pltpu.roll(x, tm - last_shift, 0), x))
        row = lax.broadcasted_iota(jnp.int32, (tm, d), 0)
        prefix = jnp.concatenate([meta_ref[...], jnp.zeros((tm - n_prefix, d), F32)], axis=0)
        h = jnp.where(first & (row < n_prefix), prefix, h)
        h = jnp.where(last & (row >= last_valid), 0.0, h)
        o_ref[...] = h
        res_ref = o_ref
    else:
        (h_ref, g_ref, wg_ref, wu_ref, wd_ref, o_ref, xn_ref, act_ref) = refs
        h = h_ref[...]
        res_ref = h_ref
    xn_ref[...] = _rms_rows(h, g_ref[...]).astype(BF16)
    d_ff = wg_ref.shape[1]
    for c in range(d_ff // FF_CHUNK):
        cols = slice(c * FF_CHUNK, (c + 1) * FF_CHUNK)
        g = _dot(xn_ref[...], wg_ref[:, cols])
        u = _dot(xn_ref[...], wu_ref[:, cols])
        act_ref[:, cols] = ((g * jax.nn.sigmoid(g)) * u).astype(BF16)
    o_ref[...] = res_ref[...] + 0.5 * _dot(act_ref[...], wd_ref[...])


def _ffn(h, norm_g, wg, wu, wd, mix=None, compact=None, assemble=None):
    m, d = h.shape
    d_ff = wg.shape[1]
    row = lambda i: (i, 0)
    const2 = lambda i: (0, 0)
    kernel_assemble = None
    if assemble is not None:
        prefix, l_pad, n_rows = assemble
        n_prefix = prefix.shape[0]
        tm = ROW_TILE
        tiles = l_pad // tm
        n_seq = m // n_rows
        last_start = (tiles - 1) * tm - n_prefix
        assert tiles >= 2 and n_rows >= tm and n_prefix % SUBLANES == 0 and n_rows % SUBLANES == 0
        assert 0 <= last_start - (n_rows - tm) < tm
        kernel_assemble = (tiles, n_prefix, last_start - (n_rows - tm), n_prefix + n_rows - (tiles - 1) * tm)
        m_out = n_seq * l_pad
        window = lambda i: (pl.multiple_of(
            (i // tiles) * n_rows + jnp.clip((i % tiles) * tm - n_prefix, 0, n_rows - tm), SUBLANES), 0)
        row_spec = lambda width: pl.BlockSpec((pl.Element(tm), pl.Element(width)), window)
    elif compact is None:
        tm = ROW_TILE
        m_out = m
        row_spec = lambda width: pl.BlockSpec((tm, width), row)
    else:
        l_pad, first_row, n_rows = compact
        tm = COMPACT_ROW_TILE
        align = 2 * SUBLANES
        assert n_rows % tm == 0 and first_row % align == 0 and l_pad % align == 0 and tm % align == 0
        tiles = n_rows // tm
        m_out = (m // l_pad) * n_rows
        window = lambda i: (pl.multiple_of((i // tiles) * l_pad + first_row + (i % tiles) * tm, align), 0)
        row_spec = lambda width: pl.BlockSpec((pl.Element(tm), pl.Element(width)), window)
    in_specs = [row_spec(d)]
    args = [h]
    if assemble is not None:
        in_specs.append(pl.BlockSpec(prefix.shape, const2))
        args.append(prefix)
    if mix is not None:
        o_da, o_rw, w_out = mix
        in_specs += [row_spec(DA_WIDTH), row_spec(RW_WIDTH), pl.BlockSpec(w_out.shape, const2)]
        args += [o_da, o_rw, w_out]
    in_specs += [pl.BlockSpec((1, d), const2),
                 pl.BlockSpec(wg.shape, const2, pipeline_mode=pl.Buffered(1)),
                 pl.BlockSpec(wu.shape, const2, pipeline_mode=pl.Buffered(1)),
                 pl.BlockSpec(wd.shape, const2, pipeline_mode=pl.Buffered(1))]
    args += [norm_g.reshape(1, d), wg, wu, wd]
    return pl.pallas_call(
        functools.partial(_ffn_kernel, has_mix=mix is not None, assemble=kernel_assemble),
        grid=(m_out // tm,),
        in_specs=in_specs,
        out_specs=pl.BlockSpec((tm, d), row),
        out_shape=jax.ShapeDtypeStruct((m_out, d), F32),
        scratch_shapes=[pltpu.VMEM((tm, d), BF16), pltpu.VMEM((tm, d_ff), BF16)],
        compiler_params=pltpu.CompilerParams(dimension_semantics=("arbitrary",),
                                             vmem_limit_bytes=VMEM_LIMIT_BYTES),
        name="ffn_mix" if mix is not None else "ffn",
    )(*args)


def _qk_prep(u, gain_ref, c, s1, s2, ones_bd, scale, out_ref):
    rep = SEG_LANES // LANES
    c, s1, s2 = (jnp.concatenate([t] * rep, axis=1) for t in (c, s1, s2))
    for j in range(DA_QK_WIDTH // SEG_LANES):
        sl = slice(SEG_LANES * j, SEG_LANES * (j + 1))
        x = u[:, sl]
        ss = _seg_sum(x * x, ones_bd)
        xn = x * lax.rsqrt(ss * (1.0 / DA_QK_DIM) + NORM_EPS) * gain_ref[:, sl]
        half = ROPE_DIM // 2
        xr = xn * c + pltpu.roll(xn, SEG_LANES - half, 1) * s1 + pltpu.roll(xn, half, 1) * s2
        out_ref[:, sl] = (xr * scale).astype(BF16)


def _mixin_kernel(h_ref, g_ref, win_ref, qg_ref, kg_ref, c_ref, s1_ref, s2_ref, ones_ref,
                  q_ref, k_ref, v_ref, urw_ref, *, q_scale):
    xn = _rms_rows(h_ref[...], g_ref[...]).astype(BF16)
    qw = DA_QK_WIDTH
    c, s1, s2, ones_bd = c_ref[...], s1_ref[...], s2_ref[...], ones_ref[...]
    uq = _dot(xn, win_ref[:, 0:qw])
    uk = _dot(xn, win_ref[:, qw:2 * qw])
    _qk_prep(uq, qg_ref, c, s1, s2, ones_bd, q_scale, q_ref)
    _qk_prep(uk, kg_ref, c, s1, s2, ones_bd, 1.0, k_ref)
    v_ref[...] = _dot(xn, win_ref[:, 2 * qw:DA_IN]).astype(BF16)
    urw_ref[...] = _dot(xn, win_ref[:, DA_IN:N_IN_PAD])


def _mix_in(h, norm_g, w_in_p, q_gain, k_gain, rope_c, rope_s1, rope_s2, ones_bd, l_pad):
    m, d = h.shape
    tm = ROW_TILE
    tiles_per_seq = l_pad // tm
    row = lambda i: (i, 0)
    pos = lambda i: (i % tiles_per_seq, 0)
    const2 = lambda i: (0, 0)
    q_scale = DA_QK_DIM ** -0.5 * math.log2(math.e)
    return pl.pallas_call(
        functools.partial(_mixin_kernel, q_scale=q_scale),
        grid=(m // tm,),
        in_specs=[pl.BlockSpec((tm, d), row), pl.BlockSpec((1, d), const2),
                  pl.BlockSpec(w_in_p.shape, const2, pipeline_mode=pl.Buffered(1)),
                  pl.BlockSpec((1, DA_QK_WIDTH), const2), pl.BlockSpec((1, DA_QK_WIDTH), const2),
                  pl.BlockSpec((tm, LANES), pos), pl.BlockSpec((tm, LANES), pos),
                  pl.BlockSpec((tm, LANES), pos), pl.BlockSpec((SEG_LANES, SEG_LANES), const2)],
        out_specs=[pl.BlockSpec((tm, DA_QK_WIDTH), row), pl.BlockSpec((tm, DA_QK_WIDTH), row),
                   pl.BlockSpec((tm, DA_WIDTH), row), pl.BlockSpec((tm, RW_IN_PAD), row)],
        out_shape=[jax.ShapeDtypeStruct((m, DA_QK_WIDTH), BF16), jax.ShapeDtypeStruct((m, DA_QK_WIDTH), BF16),
                   jax.ShapeDtypeStruct((m, DA_WIDTH), BF16), jax.ShapeDtypeStruct((m, RW_IN_PAD), F32)],
        compiler_params=pltpu.CompilerParams(dimension_semantics=("arbitrary",),
                                             vmem_limit_bytes=VMEM_LIMIT_BYTES),
        name="mix_in",
    )(h, norm_g.reshape(1, d), w_in_p, q_gain, k_gain, rope_c, rope_s1, rope_s2, ones_bd)


def _attn_kernel(lam_ref, q_ref, k_ref, v_ref, sub_ref, o_ref, q2_s, vx_s, m_s, l_s, acc_s, s_s, *, tq, lam_init):
    qi = pl.program_id(2)
    n_heads = q_ref.shape[1] // LANES
    head_lanes = [slice(LANES * hh, LANES * (hh + 1)) for hh in range(n_heads)]
    vx_lanes = [slice(2 * LANES * hh, 2 * LANES * (hh + 1)) for hh in range(n_heads)]
    streams = [(hh, slice((2 * hh + c) * tq, (2 * hh + c + 1) * tq)) for hh in range(n_heads) for c in range(2)]

    @pl.when(qi == 0)
    def _():
        for hh in range(n_heads):
            vx_s[:, 2 * LANES * hh:2 * LANES * hh + LANES] = v_ref[:, head_lanes[hh]]
            vx_s[:, 2 * LANES * hh + LANES:2 * LANES * (hh + 1)] = jnp.ones((vx_s.shape[0], LANES), BF16)

    for hh in range(n_heads):
        q = q_ref[:, head_lanes[hh]]
        lane = lax.broadcasted_iota(jnp.int32, q.shape, 1)
        zero = jnp.zeros_like(q)
        q2_s[streams[2 * hh][1], :] = jnp.where(lane < DA_QK_DIM, q, zero)
        q2_s[streams[2 * hh + 1][1], :] = jnp.where(lane < DA_QK_DIM, zero, q)
    m_s[...] = jnp.full(m_s.shape, -1e30, F32)
    l_s[...] = jnp.zeros_like(l_s)
    acc_s[...] = jnp.zeros_like(acc_s)

    def process(start, width, diag_offset):
        def scores(st):
            hh, rows = streams[st]
            s_s[rows, 0:width] = _dot_nt(q2_s[rows, :], k_ref[pl.ds(start, width), head_lanes[hh]])

        lead = min(ATTN_SCORE_LEAD, len(streams))
        for st in range(lead):
            scores(st)
        for st, (hh, rows) in enumerate(streams):
            s = s_s[rows, 0:width]
            if diag_offset is not None:
                row = lax.broadcasted_iota(jnp.int32, s.shape, 0)
                col = lax.broadcasted_iota(jnp.int32, s.shape, 1)
                s = jnp.where(col <= row + diag_offset, s, -jnp.inf)
            m_prev = m_s[rows, :]
            m_new = jnp.maximum(m_prev, jnp.max(s, axis=1, keepdims=True))
            alpha = jnp.exp2(m_prev - m_new)
            p = jnp.exp2(s - jnp.concatenate([m_new] * (width // LANES), axis=1))
            pv = _dot(p.astype(BF16), vx_s[pl.ds(start, width), vx_lanes[hh]])
            acc_s[rows, :] = acc_s[rows, :] * alpha + pv[:, 0:LANES]
            l_s[rows, :] = l_s[rows, :] * alpha + pv[:, LANES:2 * LANES]
            m_s[rows, :] = m_new
            if st + lead < len(streams):
                scores(st + lead)

    def wide_body(t, carry):
        process(pl.multiple_of(t * (2 * tq), 2 * tq), 2 * tq, None)
        return carry

    lax.fori_loop(0, qi // 2, wide_body, 0)

    @pl.when(qi % 2 == 0)
    def _():
        process(pl.multiple_of(qi * tq, tq), tq, 0)

    @pl.when(qi % 2 == 1)
    def _():
        process(pl.multiple_of((qi - 1) * tq, tq), 2 * tq, tq)

    lam1 = jnp.exp(jnp.sum(lam_ref[0:1, :] * lam_ref[1:2, :], axis=1, keepdims=True))
    lam2 = jnp.exp(jnp.sum(lam_ref[2:3, :] * lam_ref[3:4, :], axis=1, keepdims=True))
    lam = lam1 - lam2 + lam_init
    for hh in range(n_heads):
        r0, r1 = streams[2 * hh][1], streams[2 * hh + 1][1]
        o = acc_s[r0, :] / l_s[r0, :] - lam * (acc_s[r1, :] / l_s[r1, :])
        o = _rms_rows(o, sub_ref[...]) * (1.0 - lam_init)
        o_ref[:, head_lanes[hh]] = o.astype(o_ref.dtype)


def _attention(q, k, v, lam_vecs, subln, lam_init):
    b, l_pad, _ = q.shape
    tq = ATTN_TILE
    nq = l_pad // tq
    nh = ATTN_HEADS_PER_STEP
    hw = nh * LANES
    n_streams = 2 * nh
    kv_spec = pl.BlockSpec((None, l_pad, hw), lambda bi, h, qi: (bi, 0, h), pipeline_mode=pl.Buffered(1))
    return pl.pallas_call(
        functools.partial(_attn_kernel, tq=tq, lam_init=lam_init),
        grid=(b, DA_HEADS // nh, nq),
        in_specs=[pl.BlockSpec(lam_vecs.shape, lambda bi, h, qi: (0, 0)),
                  pl.BlockSpec((None, tq, hw), lambda bi, h, qi: (bi, qi, h)),
                  kv_spec, kv_spec,
                  pl.BlockSpec((1, DA_V_DIM), lambda bi, h, qi: (0, 0))],
        out_specs=pl.BlockSpec((None, tq, hw), lambda bi, h, qi: (bi, qi, h)),
        out_shape=jax.ShapeDtypeStruct((b, l_pad, DA_WIDTH), BF16),
        scratch_shapes=[pltpu.VMEM((n_streams * tq, LANES), BF16), pltpu.VMEM((l_pad, 2 * hw), BF16),
                        pltpu.VMEM((n_streams * tq, LANES), F32), pltpu.VMEM((n_streams * tq, LANES), F32),
                        pltpu.VMEM((n_streams * tq, LANES), F32),
                        pltpu.VMEM((n_streams * tq, 2 * tq), F32)],
        compiler_params=pltpu.CompilerParams(dimension_semantics=("arbitrary", "arbitrary", "arbitrary"),
                                             vmem_limit_bytes=VMEM_LIMIT_BYTES),
        name="diff_attention",
    )(lam_vecs, q, k, v, subln.reshape(1, DA_V_DIM))


def _expand_heads(x, lo_mask):
    zero = jnp.zeros_like(x)
    return jnp.concatenate([jnp.where(lo_mask, x, zero), jnp.where(lo_mask, zero, x)], axis=0)


def _run(stages):
    try:
        while True:
            next(stages)
    except StopIteration as stop:
        return stop.value


def _run_interleaved(stages_a, stages_b):
    live = [stages_a, stages_b]
    results = [None, None]
    while live[0] is not None or live[1] is not None:
        for n in range(2):
            if live[n] is None:
                continue
            try:
                next(live[n])
            except StopIteration as stop:
                results[n] = stop.value
                live[n] = None
    return results


def _unit_lower_inverse(a_list):
    n = a_list[0].shape[0]
    eye = (lax.broadcasted_iota(jnp.int32, (n, n), 0) == lax.broadcasted_iota(jnp.int32, (n, n), 1)).astype(F32)
    mm = lambda x, y: _dot(x.astype(BF16), y.astype(BF16))
    t_list = [eye + a for a in a_list]
    p_list = [mm(a, a) for a in a_list]
    yield
    levels = int(math.log2(RW_CHUNK))
    for _ in range(levels - 2):
        pt_list = [mm(p, jnp.concatenate([p, t], axis=1)) for p, t in zip(p_list, t_list)]
        t_list = [t + pt[:, n:] for t, pt in zip(t_list, pt_list)]
        p_list = [pt[:, :n] for pt in pt_list]
        yield
    return [t + mm(p, t) for p, t in zip(p_list, t_list)]


_RW_CARRY = ("ar", "bke", "v2", "t_inv", "akv", "arbk", "g_end", "bonus", "gate")


def _rwkv_kernel(u_ref, next_ref, mix_ref, w2a2_ref, w0_ref, a0_ref, g2_ref, kk_ref, ka_ref, rk_ref,
                 lnw_ref, lnb_ref, ones_ref, tri_ref, o_ref, s_ref, *carry_refs):
    i = pl.program_id(1)
    carry = dict(zip(_RW_CARRY, carry_refs))
    c_len = RW_CHUNK
    w = RW_WIDTH
    n2 = 2 * c_len
    n_sub = u_ref.shape[0] // RW_ROWS
    n_chunks = RW_ROWS // c_len
    n_pairs = w // LANES
    pair_lanes = [slice(LANES * p, LANES * (p + 1)) for p in range(n_pairs)]
    seg_lanes = [slice(SEG_LANES * p, SEG_LANES * (p + 1)) for p in range(w // SEG_LANES)]

    @pl.when(i == 0)
    def _():
        s_ref[...] = jnp.zeros_like(s_ref)

    ones_bd = ones_ref[...]
    tri = tri_ref[...]
    r_i = lax.broadcasted_iota(jnp.int32, (n2, n2), 0)
    c_i = lax.broadcasted_iota(jnp.int32, (n2, n2), 1)
    t_row = jnp.where(r_i >= c_len, r_i - c_len, r_i)
    t_col = jnp.where(c_i >= c_len, c_i - c_len, c_i)
    strict = t_col < t_row
    incl = t_col <= t_row
    lo_mask = lax.broadcasted_iota(jnp.int32, (c_len, LANES), 1) < RW_HEAD

    def prepare(src_ref, base, prev_row):
        def shifted(lo, hi):
            u = src_ref[base:base + RW_ROWS, lo:hi]
            last_prev = prev_row(lo, hi)
            rolled = pltpu.roll(u, 1, 0)
            row = lax.broadcasted_iota(jnp.int32, u.shape, 0)
            prev = jnp.where(row == 0, last_prev, rolled)
            return u + (prev - u) * mix_ref[:, lo:hi]

        r = shifted(0, w)
        k = shifted(w, 2 * w)
        v = shifted(2 * w, 3 * w)
        lora_in = shifted(3 * w, 3 * w + LANES)
        g_in = shifted(3 * w + LANES, RW_IN_PAD)

        lane = lax.broadcasted_iota(jnp.int32, lora_in.shape, 1)
        lora_act = jnp.where(lane < D_DECAY_LORA, jnp.tanh(lora_in), lora_in)
        wa = _dot_split(lora_act, w2a2_ref[...])
        g = _dot(jax.nn.sigmoid(g_in).astype(BF16), g2_ref[...])
        yield
        w_pre = w0_ref[...] + wa[:, 0:w]
        lw = (-math.exp(-0.5)) * jax.nn.sigmoid(w_pre)
        a = jax.nn.sigmoid(a0_ref[...] + wa[:, w:2 * w])

        cums = []
        for c in range(n_chunks):
            lw_c = lw[c * c_len:(c + 1) * c_len]
            hi = lw_c.astype(BF16)
            mid_f = lw_c - hi.astype(F32)
            mid = mid_f.astype(BF16)
            lo = (mid_f - mid.astype(F32)).astype(BF16)
            cums.append(_dot(tri, hi) + (_dot(tri, mid) + _dot(tri, lo)))

        kk_raw = k * kk_ref[...]
        k_mod = k * (1.0 + (a - 1.0) * ka_ref[...])
        rkr = r * k_mod * rk_ref[...]
        ss_l = [_seg_sum(kk_raw[:, sl] * kk_raw[:, sl], ones_bd) for sl in seg_lanes]
        rk_l = [_seg_sum(rkr[:, sl], ones_bd) for sl in seg_lanes]
        yield
        kk = jnp.concatenate([kk_raw[:, sl] * lax.rsqrt(jnp.maximum(ss, 1e-24))
                              for sl, ss in zip(seg_lanes, ss_l)], axis=1)
        bonus = jnp.concatenate([s * v[:, sl] for sl, s in zip(seg_lanes, rk_l)], axis=1)
        a_neg = -kk
        b_vec = kk * a

        ar_l, bk_l, bke_l, v2_l, g_end_l = [], [], [], [], []
        for c in range(n_chunks):
            rs = slice(c * c_len, (c + 1) * c_len)
            cum = cums[c]
            cum_end = cum[c_len - 1:c_len, :]
            e_in = jnp.exp(cum)
            e_out = jnp.exp(-cum)
            e_end = jnp.exp(cum_end - cum)
            r_t = r[rs] * e_in
            a_t = a_neg[rs] * jnp.exp(cum - lw[rs])
            k_t = k_mod[rs] * e_out
            b_t = b_vec[rs] * e_out
            k_e = k_mod[rs] * e_end
            b_e = b_vec[rs] * e_end
            g_end_l.append(jnp.exp(cum_end))
            v_c = v[rs]
            for sl in pair_lanes:
                ar_l.append(jnp.concatenate([_expand_heads(a_t[:, sl], lo_mask), _expand_heads(r_t[:, sl], lo_mask)],
                                            axis=0).astype(BF16))
                bk_l.append(jnp.concatenate([_expand_heads(b_t[:, sl], lo_mask), _expand_heads(k_t[:, sl], lo_mask)],
                                            axis=0).astype(BF16))
                bke_l.append(jnp.concatenate([_expand_heads(b_e[:, sl], lo_mask), _expand_heads(k_e[:, sl], lo_mask)],
                                             axis=0).astype(BF16))
                v2_l.append(_expand_heads(v_c[:, sl], lo_mask).astype(BF16))
        aa_l = [_dot_nt(ar, bk) for ar, bk in zip(ar_l, bk_l)]
        yield
        akv_l = [_dot(jnp.where(strict, aa[0:n2, n2:2 * n2], 0.0).astype(BF16), v2) for aa, v2 in zip(aa_l, v2_l)]
        arbk_l = [jnp.concatenate([jnp.where(incl, aa[n2:2 * n2, 0:n2], 0.0),
                                   jnp.where(incl, aa[n2:2 * n2, n2:2 * n2], 0.0)], axis=1).astype(BF16)
                  for aa in aa_l]
        t_inv_l = yield from _unit_lower_inverse([jnp.where(strict, aa[0:n2, 0:n2], 0.0) for aa in aa_l])
        t_inv_l = [t.astype(BF16) for t in t_inv_l]
        yield
        return dict(ar=ar_l, bke=bke_l, v2=v2_l, g_end=g_end_l, t_inv=t_inv_l, akv=akv_l, arbk=arbk_l,
                    bonus=bonus, gate=g)

    def advance(j, pre):
        base = j * RW_ROWS
        y_chunks = []
        for c in range(n_chunks):
            idx = [c * n_pairs + p for p in range(n_pairs)]
            states = [s_ref[p] for p in range(n_pairs)]
            arh = [_dot_nt(pre["ar"][q], s.astype(BF16)) for q, s in zip(idx, states)]
            yield
            u_l = [_dot(pre["t_inv"][q], (h[0:n2] + pre["akv"][q]).astype(BF16)) for q, h in zip(idx, arh)]
            yield
            uv = [jnp.concatenate([u.astype(BF16), pre["v2"][q]], axis=0) for q, u in zip(idx, u_l)]
            y2 = [h[n2:2 * n2] + _dot(pre["arbk"][q], x) for q, h, x in zip(idx, arh, uv)]
            for p in range(n_pairs):
                s_ref[p] = states[p] * pre["g_end"][c][:, pair_lanes[p]] + _dot_tn(uv[p], pre["bke"][idx[p]])
            yield
            y_chunks.append(jnp.concatenate([y[0:c_len] + y[c_len:n2] for y in y2], axis=1))
        y = jnp.concatenate(y_chunks, axis=0)

        mu_l = [_seg_sum(y[:, sl], ones_bd) * (1.0 / RW_HEAD) for sl in seg_lanes]
        yield
        d_l = [y[:, sl] - mu for sl, mu in zip(seg_lanes, mu_l)]
        var_l = [_seg_sum(d * d, ones_bd) * (1.0 / RW_HEAD) for d in d_l]
        yield
        yn = jnp.concatenate([d * lax.rsqrt(var + GN_EPS) for d, var in zip(d_l, var_l)], axis=1)
        yn = yn * lnw_ref[...] + lnb_ref[...]
        o_ref[base:base + RW_ROWS, :] = ((yn + pre["bonus"]) * pre["gate"]).astype(o_ref.dtype)

    def save(pre):
        for name in _RW_CARRY:
            value, ref = pre[name], carry[name]
            if isinstance(value, list):
                for q, item in enumerate(value):
                    ref[q] = item
            else:
                ref[...] = value

    def load():
        return {name: ([ref[q] for q in range(ref.shape[0])] if len(ref.shape) == 3 else ref[...])
                for name, ref in carry.items()}

    def row_before(src_ref, row):
        return lambda lo, hi: src_ref[row:row + 1, lo:hi]

    @pl.when(i == 0)
    def _():
        save(_run(prepare(u_ref, 0, lambda lo, hi: jnp.zeros((1, hi - lo), F32))))

    pre = load()
    for j in range(n_sub):
        last_row = (j + 1) * RW_ROWS - 1
        if j + 1 < n_sub:
            upcoming = prepare(u_ref, (j + 1) * RW_ROWS, row_before(u_ref, last_row))
        else:
            upcoming = prepare(next_ref, 0, row_before(u_ref, last_row))
        pre, _ = _run_interleaved(upcoming, advance(j, pre))
    save(pre)


def _rwkv(u_rw, shift_mix_p, w2a2, w0, a0, g2_p, k_k, k_a, r_k, ln_w, ln_b, ones_bd, tri):
    b, l_pad, _ = u_rw.shape
    rows = RW_BLOCK_ROWS
    sub_per_block = rows // RW_ROWS
    last_sub = l_pad // RW_ROWS - 1
    n_prob = (RW_ROWS // RW_CHUNK) * (RW_WIDTH // LANES)
    n2 = 2 * RW_CHUNK
    vec = lambda x: x.reshape(1, -1).astype(F32)
    const = lambda bi, i: (0, 0)
    vec_spec = pl.BlockSpec((1, RW_WIDTH), const)
    carry_shapes = dict(
        ar=pltpu.VMEM((n_prob, 2 * n2, LANES), BF16), bke=pltpu.VMEM((n_prob, 2 * n2, LANES), BF16),
        v2=pltpu.VMEM((n_prob, n2, LANES), BF16), t_inv=pltpu.VMEM((n_prob, n2, n2), BF16),
        akv=pltpu.VMEM((n_prob, n2, LANES), F32), arbk=pltpu.VMEM((n_prob, n2, 2 * n2), BF16),
        g_end=pltpu.VMEM((RW_ROWS // RW_CHUNK, 1, RW_WIDTH), F32),
        bonus=pltpu.VMEM((RW_ROWS, RW_WIDTH), F32), gate=pltpu.VMEM((RW_ROWS, RW_WIDTH), F32))
    return pl.pallas_call(
        _rwkv_kernel,
        grid=(b, l_pad // rows),
        in_specs=[pl.BlockSpec((None, rows, RW_IN_PAD), lambda bi, i: (bi, i, 0)),
                  pl.BlockSpec((None, RW_ROWS, RW_IN_PAD),
                               lambda bi, i: (bi, jnp.minimum((i + 1) * sub_per_block, last_sub), 0)),
                  pl.BlockSpec((1, RW_IN_PAD), const),
                  pl.BlockSpec(w2a2.shape, const),
                  vec_spec, vec_spec,
                  pl.BlockSpec(g2_p.shape, const),
                  vec_spec, vec_spec, vec_spec, vec_spec, vec_spec,
                  pl.BlockSpec((SEG_LANES, SEG_LANES), const),
                  pl.BlockSpec((RW_CHUNK, RW_CHUNK), const)],
        out_specs=pl.BlockSpec((None, rows, RW_WIDTH), lambda bi, i: (bi, i, 0)),
        out_shape=jax.ShapeDtypeStruct((b, l_pad, RW_WIDTH), BF16),
        scratch_shapes=[pltpu.VMEM((RW_WIDTH // LANES, LANES, LANES), F32)] + [carry_shapes[n] for n in _RW_CARRY],
        compiler_params=pltpu.CompilerParams(dimension_semantics=("arbitrary", "arbitrary"),
                                             vmem_limit_bytes=VMEM_LIMIT_BYTES),
        name="rwkv7",
    )(u_rw, u_rw, vec(shift_mix_p), w2a2, vec(w0), vec(a0), g2_p, vec(k_k), vec(k_a), vec(r_k),
      vec(ln_w), vec(ln_b), ones_bd, tri)


def _rope_lane_tables(length):
    pos = jnp.arange(length, dtype=F32)
    inv_freq = ROPE_THETA ** (-jnp.arange(0, ROPE_DIM, 2, dtype=F32) / ROPE_DIM)
    ang = pos[:, None] * inv_freq[None, :]
    cos, sin = jnp.cos(ang), jnp.sin(ang)
    half = ROPE_DIM // 2
    rest = DA_QK_DIM - ROPE_DIM
    ones = jnp.ones((length, rest), F32)
    zeros = lambda n: jnp.zeros((length, n), F32)
    c = jnp.concatenate([cos, cos, ones], axis=1)
    s1 = jnp.concatenate([-sin, zeros(half + rest)], axis=1)
    s2 = jnp.concatenate([zeros(half), sin, zeros(rest)], axis=1)
    rep = LANES // DA_QK_DIM
    return jnp.tile(c, (1, rep)), jnp.tile(s1, (1, rep)), jnp.tile(s2, (1, rep))


def kernel(x, meta_tokens, ffn1_norm, ffn1_w_gate, ffn1_w_up, ffn1_w_down, mix_norm, w_in, da_q_norm, da_k_norm, da_lambda_q1, da_lambda_k1, da_lambda_q2, da_lambda_k2, da_subln, rw_shift_mix, rw_w0, rw_w2, rw_a0, rw_a2, rw_g2, rw_k_k, rw_k_a, rw_r_k, rw_ln_w, rw_ln_b, w_out, ffn2_norm, ffn2_w_gate, ffn2_w_up, ffn2_w_down):
    bsz, t, d = x.shape
    depth = w_in.shape[0]
    l = N_META + t
    l_pad = -(-l // Q_BLOCK) * Q_BLOCK
    assert d == D_MODEL and l_pad % ROW_TILE == 0 and l_pad % ATTN_TILE == 0 and l_pad % RW_BLOCK_ROWS == 0
    h = x.reshape(bsz * t, d)
    meta = meta_tokens.astype(x.dtype)

    rope_c, rope_s1, rope_s2 = _rope_lane_tables(l_pad)
    lane_head = jnp.arange(SEG_LANES) // RW_HEAD
    ones_bd =(lane_head[:, None] == lane_head[None, :]).astype(BF16)
    tri = (jnp.arange(RW_CHUNK)[:, None] >= jnp.arange(RW_CHUNK)[None, :]).astype(BF16)

    for layer in range(depth):
        lam_init = 0.8 - 0.6 * math.exp(-0.3 * layer)
        bf = lambda a: a[layer].astype(BF16)
        w_in_p = jnp.pad(w_in[layer], ((0, 0), (0, N_IN_PAD - w_in.shape[2]))).astype(BF16)
        shift_mix_p = jnp.pad(rw_shift_mix[layer], (0, RW_IN_PAD - RW_IN))
        zeros_lora = jnp.zeros((D_DECAY_LORA, RW_WIDTH), F32)
        w2a2 = jnp.concatenate([jnp.concatenate([rw_w2[layer], zeros_lora], axis=1),
                                jnp.concatenate([zeros_lora, rw_a2[layer]], axis=1)], axis=0)
        g2_p = jnp.pad(rw_g2[layer], ((0, RW_LORA_PAD - D_GATE_LORA), (0, 0))).astype(BF16)
        q_gain = jnp.tile(da_q_norm[layer], DA_QK_WIDTH // DA_QK_DIM).reshape(1, DA_QK_WIDTH)
        k_gain = jnp.tile(da_k_norm[layer], DA_QK_WIDTH // DA_QK_DIM).reshape(1, DA_QK_WIDTH)
        lam_vecs = jnp.stack([da_lambda_q1[layer], da_lambda_k1[layer],
                              da_lambda_q2[layer], da_lambda_k2[layer]]).astype(F32)

        h = _ffn(h, ffn1_norm[layer], bf(ffn1_w_gate), bf(ffn1_w_up), bf(ffn1_w_down),
                 assemble=(meta, l_pad, t) if layer == 0 else None)
        q, k, v, u_rw = _mix_in(h, mix_norm[layer], w_in_p, q_gain, k_gain,
                                rope_c, rope_s1, rope_s2, ones_bd, l_pad)
        shape3 = lambda a: a.reshape(bsz, l_pad, a.shape[-1])
        o_da = _attention(shape3(q), shape3(k), shape3(v), lam_vecs, da_subln[layer], lam_init)
        o_rw = _rwkv(shape3(u_rw), shift_mix_p, w2a2, rw_w0[layer], rw_a0[layer], g2_p,
                     rw_k_k[layer], rw_k_a[layer], rw_r_k[layer], rw_ln_w[layer], rw_ln_b[layer],
                     ones_bd, tri)
        last = layer == depth - 1
        h = _ffn(h, ffn2_norm[layer], bf(ffn2_w_gate), bf(ffn2_w_up), bf(ffn2_w_down),
                 mix=(o_da.reshape(bsz * l_pad, DA_WIDTH), o_rw.reshape(bsz * l_pad, RW_WIDTH), bf(w_out)),
                 compact=(l_pad, N_META, t) if last else None)
    return h.reshape(bsz, t, d)
```

```python
import functools
import math

import jax
import jax.numpy as jnp
from jax import lax
from jax.experimental import pallas as pl
from jax.experimental.pallas import tpu as pltpu

F32 = jnp.float32
BF16 = jnp.bfloat16

D_MODEL = 1024
N_META = 16
Q_BLOCK = 128
ROPE_THETA = 500000.0
NORM_EPS = 1e-6
DA_HEADS = 4
DA_QK_DIM = 64
DA_V_DIM = 2 * DA_QK_DIM
DA_WIDTH = DA_HEADS * DA_V_DIM
DA_QK_WIDTH = DA_HEADS * 2 * DA_QK_DIM
ROPE_DIM = DA_QK_DIM // 4
RW_HEAD = 64
RW_WIDTH = D_MODEL - DA_WIDTH
RW_HEADS = RW_WIDTH // RW_HEAD
D_DECAY_LORA = 64
D_AAA_LORA = 64
D_GATE_LORA = 160
GN_EPS = 64e-5
DA_IN = 2 * DA_QK_WIDTH + DA_WIDTH
RW_IN = 3 * RW_WIDTH + D_DECAY_LORA + D_AAA_LORA + D_GATE_LORA
D_FF = 2816

LANES = 128
SUBLANES = 8
SEG_LANES = 256
VMEM_LIMIT_BYTES = 56 * 1024 * 1024

ROW_TILE = 640
COMPACT_ROW_TILE = 512
FF_CHUNK = 256
WEIGHT_SLAB_ROWS = 256
ATTN_TILE = 640
ATTN_HEADS_PER_STEP = 2
ATTN_SCORE_LEAD = 2
RW_CHUNK = 64
RW_ROWS = 2 * RW_CHUNK
RW_BLOCK_ROWS = 5 * RW_ROWS
RW_LORA_PAD = 2 * LANES
RW_IN_PAD = 3 * RW_WIDTH + LANES + RW_LORA_PAD
N_IN_PAD = DA_IN + RW_IN_PAD

_NT = (((1,), (1,)), ((), ()))
_TN = (((0,), (0,)), ((), ()))


def _dot(a, b):
    return jnp.dot(a, b, preferred_element_type=F32)


def _dot_nt(a, b):
    return lax.dot_general(a, b, _NT, preferred_element_type=F32)


def _dot_tn(a, b):
    return lax.dot_general(a, b, _TN, preferred_element_type=F32)


def _split2(x):
    hi = x.astype(BF16)
    lo = (x - hi.astype(F32)).astype(BF16)
    return hi, lo


def _dot_split(a, b):
    a_hi, a_lo = _split2(a)
    b_hi, b_lo = _split2(b)
    return _dot(a_hi, b_hi) + (_dot(a_hi, b_lo) + _dot(a_lo, b_hi))


def _seg_sum(x, ones_bd):
    hi, lo = _split2(x)
    return _dot(hi, ones_bd) + _dot(lo, ones_bd)


def _rms_rows(h, gain):
    ms = jnp.mean(h * h, axis=-1, keepdims=True)
    return h * lax.rsqrt(ms + NORM_EPS) * gain


def _fetch_weights_bf16(jobs):
    slabs = []
    used = {}
    for hbm, vmem, stage, sem in jobs:
        rows, cols = hbm.shape
        assert rows % WEIGHT_SLAB_ROWS == 0 and stage.shape == (2, WEIGHT_SLAB_ROWS, cols)
        for r0 in range(0, rows, WEIGHT_SLAB_ROWS):
            slot = used.get(id(stage), 0) % 2
            used[id(stage)] = used.get(id(stage), 0) + 1
            slabs.append((hbm, vmem, stage, sem, r0, cols, slot))

    def copy(k):
        hbm, _, stage, sem, r0, _, slot = slabs[k]
        return pltpu.make_async_copy(hbm.at[r0:r0 + WEIGHT_SLAB_ROWS, :], stage.at[slot], sem.at[slot])

    copy(0).start()
    for k, (_, vmem, stage, _, r0, cols, slot) in enumerate(slabs):
        if k + 1 < len(slabs):
            copy(k + 1).start()
        copy(k).wait()
        vmem[r0:r0 + WEIGHT_SLAB_ROWS, 0:cols] = stage[slot].astype(BF16)
    for hbm, vmem, _, _ in jobs:
        if vmem.shape[1] > hbm.shape[1]:
            vmem[:, hbm.shape[1]:] = jnp.zeros((vmem.shape[0], vmem.shape[1] - hbm.shape[1]), BF16)


def _ffn_kernel(*refs, has_mix, assemble):
    n_w = 4 if has_mix else 3
    n_in = len(refs) - (1 + 2 + n_w + 4)
    w_hbm = refs[n_in - n_w:n_in]
    o_ref, xn_ref, act_ref = refs[n_in:n_in + 3]
    w_vmem = refs[n_in + 3:n_in + 3 + n_w]
    stage_wide, stage_narrow, sem_wide, sem_narrow = refs[n_in + 3 + n_w:]
    g_ref = refs[n_in - n_w - 1]
    wg_ref, wu_ref, wd_ref = w_vmem[0:3]

    @pl.when(pl.program_id(0) == 0)
    def _():
        stages = [(stage_wide, sem_wide), (stage_wide, sem_wide), (stage_narrow, sem_narrow),
                  (stage_narrow, sem_narrow)]
        _fetch_weights_bf16([(h, v) + s for h, v, s in zip(w_hbm, w_vmem, stages)])

    if has_mix:
        h_ref, oda_ref, orw_ref = refs[0:3]
        wout_ref = w_vmem[3]
        h = (h_ref[...] + _dot(oda_ref[...], wout_ref[0:DA_WIDTH, :])
             + _dot(orw_ref[...], wout_ref[DA_WIDTH:DA_WIDTH + RW_WIDTH, :]))
        o_ref[...] = h
        res_ref = o_ref
    elif assemble is not None:
        x_ref, meta_ref = refs[0:2]
        tiles, n_prefix, last_shift, last_valid = assemble
        tm, d = x_ref.shape
        il = pl.program_id(0) % tiles
        x = x_ref[...]
        first, last = il == 0, il == tiles - 1
        h = jnp.where(first, pltpu.roll(x, n_prefix, 0), jnp.where(last, pltpu.roll(x, tm - last_shift, 0), x))
        row = lax.broadcasted_iota(jnp.int32, (tm, d), 0)
        prefix = jnp.concatenate([meta_ref[...], jnp.zeros((tm - n_prefix, d), F32)], axis=0)
        h = jnp.where(first & (row < n_prefix), prefix, h)
        h = jnp.where(last & (row >= last_valid), 0.0, h)
        o_ref[...] = h
        res_ref = o_ref
    else:
        h_ref = refs[0]
        h = h_ref[...]
        res_ref = h_ref
    xn_ref[...] = _rms_rows(h, g_ref[...]).astype(BF16)
    d_ff = wg_ref.shape[1]
    for c in range(d_ff // FF_CHUNK):
        cols = slice(c * FF_CHUNK, (c + 1) * FF_CHUNK)
        g = _dot(xn_ref[...], wg_ref[:, cols])
        u = _dot(xn_ref[...], wu_ref[:, cols])
        act_ref[:, cols] = ((g * jax.nn.sigmoid(g)) * u).astype(BF16)
    o_ref[...] = res_ref[...] + 0.5 * _dot(act_ref[...], wd_ref[...])


def _ffn(h, norm_g, wg, wu, wd, mix=None, compact=None, assemble=None):
    m, d = h.shape
    d_ff = wg.shape[1]
    row = lambda i: (i, 0)
    const2 = lambda i: (0, 0)
    kernel_assemble = None
    if assemble is not None:
        prefix, l_pad, n_rows = assemble
        n_prefix = prefix.shape[0]
        tm = ROW_TILE
        tiles = l_pad // tm
        n_seq = m // n_rows
        last_start = (tiles - 1) * tm - n_prefix
        assert tiles >= 2 and n_rows >= tm and n_prefix % SUBLANES == 0 and n_rows % SUBLANES == 0
        assert 0 <= last_start - (n_rows - tm) < tm
        kernel_assemble = (tiles, n_prefix, last_start - (n_rows - tm), n_prefix + n_rows - (tiles - 1) * tm)
        m_out = n_seq * l_pad
        window = lambda i: (pl.multiple_of(
            (i // tiles) * n_rows + jnp.clip((i % tiles) * tm - n_prefix, 0, n_rows - tm), SUBLANES), 0)
        row_spec = lambda width: pl.BlockSpec((pl.Element(tm), pl.Element(width)), window)
    elif compact is None:
        tm = ROW_TILE
        m_out = m
        row_spec = lambda width: pl.BlockSpec((tm, width), row)
    else:
        l_pad, first_row, n_rows = compact
        tm = COMPACT_ROW_TILE
        align = 2 * SUBLANES
        assert n_rows % tm == 0 and first_row % align == 0 and l_pad % align == 0 and tm % align == 0
        tiles = n_rows // tm
        m_out = (m // l_pad) * n_rows
        window = lambda i: (pl.multiple_of((i // tiles) * l_pad + first_row + (i % tiles) * tm, align), 0)
        row_spec = lambda width: pl.BlockSpec((pl.Element(tm), pl.Element(width)), window)
    in_specs = [row_spec(d)]
    args = [h]
    if assemble is not None:
        in_specs.append(pl.BlockSpec(prefix.shape, const2))
        args.append(prefix)
    weights = [wg, wu, wd]
    if mix is not None:
        o_da, o_rw, w_out = mix
        in_specs += [row_spec(DA_WIDTH), row_spec(RW_WIDTH)]
        args += [o_da, o_rw]
        weights.append(w_out)
    in_specs += [pl.BlockSpec((1, d), const2)] + [pl.BlockSpec(memory_space=pl.ANY)] * len(weights)
    args += [norm_g.reshape(1, d)] + weights
    return pl.pallas_call(
        functools.partial(_ffn_kernel, has_mix=mix is not None, assemble=kernel_assemble),
        grid=(m_out // tm,),
        in_specs=in_specs,
        out_specs=pl.BlockSpec((tm, d), row),
        out_shape=jax.ShapeDtypeStruct((m_out, d), F32),
        scratch_shapes=([pltpu.VMEM((tm, d), BF16), pltpu.VMEM((tm, d_ff), BF16)]
                        + [pltpu.VMEM(w.shape, BF16) for w in weights]
                        + [pltpu.VMEM((2, WEIGHT_SLAB_ROWS, d_ff), F32), pltpu.VMEM((2, WEIGHT_SLAB_ROWS, d), F32),
                           pltpu.SemaphoreType.DMA((2,)), pltpu.SemaphoreType.DMA((2,))]),
        compiler_params=pltpu.CompilerParams(dimension_semantics=("arbitrary",),
                                             vmem_limit_bytes=VMEM_LIMIT_BYTES),
        name="ffn_mix" if mix is not None else "ffn",
    )(*args)


def _qk_prep(u, gain_ref, c, s1, s2, ones_bd, scale, out_ref):
    rep = SEG_LANES // LANES
    c, s1, s2 = (jnp.concatenate([t] * rep, axis=1) for t in (c, s1, s2))
    for j in range(DA_QK_WIDTH // SEG_LANES):
        sl = slice(SEG_LANES * j, SEG_LANES * (j + 1))
        x = u[:, sl]
        ss = _seg_sum(x * x, ones_bd)
        xn = x * lax.rsqrt(ss * (1.0 / DA_QK_DIM) + NORM_EPS) * gain_ref[:, sl]
        half = ROPE_DIM // 2
        xr = xn * c + pltpu.roll(xn, SEG_LANES - half, 1) * s1 + pltpu.roll(xn, half, 1) * s2
        out_ref[:, sl] = (xr * scale).astype(BF16)


def _mixin_kernel(h_ref, g_ref, win_hbm, qg_ref, kg_ref, c_ref, s1_ref, s2_ref, ones_ref,
                  q_ref, k_ref, v_ref, urw_ref, win_ref, stage_ref, sem_ref, *, q_scale):
    @pl.when(pl.program_id(0) == 0)
    def _():
        _fetch_weights_bf16([(win_hbm, win_ref, stage_ref, sem_ref)])

    xn = _rms_rows(h_ref[...], g_ref[...]).astype(BF16)
    qw = DA_QK_WIDTH
    c, s1, s2, ones_bd = c_ref[...], s1_ref[...], s2_ref[...], ones_ref[...]
    uq = _dot(xn, win_ref[:, 0:qw])
    uk = _dot(xn, win_ref[:, qw:2 * qw])
    _qk_prep(uq, qg_ref, c, s1, s2, ones_bd, q_scale, q_ref)
    _qk_prep(uk, kg_ref, c, s1, s2, ones_bd, 1.0, k_ref)
    v_ref[...] = _dot(xn, win_ref[:, 2 * qw:DA_IN]).astype(BF16)
    urw_ref[...] = _dot(xn, win_ref[:, DA_IN:N_IN_PAD])


def _mix_in(h, norm_g, w_in, q_gain, k_gain, rope_c, rope_s1, rope_s2, ones_bd, l_pad):
    m, d = h.shape
    tm = ROW_TILE
    tiles_per_seq = l_pad // tm
    row = lambda i: (i, 0)
    pos = lambda i: (i % tiles_per_seq, 0)
    const2 = lambda i: (0, 0)
    q_scale = DA_QK_DIM ** -0.5 * math.log2(math.e)
    return pl.pallas_call(
        functools.partial(_mixin_kernel, q_scale=q_scale),
        grid=(m // tm,),
        in_specs=[pl.BlockSpec((tm, d), row), pl.BlockSpec((1, d), const2),
                  pl.BlockSpec(memory_space=pl.ANY),
                  pl.BlockSpec((1, DA_QK_WIDTH), const2), pl.BlockSpec((1, DA_QK_WIDTH), const2),
                  pl.BlockSpec((tm, LANES), pos), pl.BlockSpec((tm, LANES), pos),
                  pl.BlockSpec((tm, LANES), pos), pl.BlockSpec((SEG_LANES, SEG_LANES), const2)],
        out_specs=[pl.BlockSpec((tm, DA_QK_WIDTH), row), pl.BlockSpec((tm, DA_QK_WIDTH), row),
                   pl.BlockSpec((tm, DA_WIDTH), row), pl.BlockSpec((tm, RW_IN_PAD), row)],
        out_shape=[jax.ShapeDtypeStruct((m, DA_QK_WIDTH), BF16), jax.ShapeDtypeStruct((m, DA_QK_WIDTH), BF16),
                   jax.ShapeDtypeStruct((m, DA_WIDTH), BF16), jax.ShapeDtypeStruct((m, RW_IN_PAD), F32)],
        scratch_shapes=[pltpu.VMEM((d, N_IN_PAD), BF16), pltpu.VMEM((2, WEIGHT_SLAB_ROWS, w_in.shape[1]), F32),
                        pltpu.SemaphoreType.DMA((2,))],
        compiler_params=pltpu.CompilerParams(dimension_semantics=("arbitrary",),
                                             vmem_limit_bytes=VMEM_LIMIT_BYTES),
        name="mix_in",
    )(h, norm_g.reshape(1, d), w_in, q_gain, k_gain, rope_c, rope_s1, rope_s2, ones_bd)


def _attn_kernel(lam_ref, q_ref, k_ref, v_ref, sub_ref, o_ref, q2_s, vx_s, m_s, l_s, acc_s, s_s, *, tq, lam_init):
    qi = pl.program_id(2)
    n_heads = q_ref.shape[1] // LANES
    head_lanes = [slice(LANES * hh, LANES * (hh + 1)) for hh in range(n_heads)]
    vx_lanes = [slice(2 * LANES * hh, 2 * LANES * (hh + 1)) for hh in range(n_heads)]
    streams = [(hh, slice((2 * hh + c) * tq, (2 * hh + c + 1) * tq)) for hh in range(n_heads) for c in range(2)]

    @pl.when(qi == 0)
    def _():
        for hh in range(n_heads):
            vx_s[:, 2 * LANES * hh:2 * LANES * hh + LANES] = v_ref[:, head_lanes[hh]]
            vx_s[:, 2 * LANES * hh + LANES:2 * LANES * (hh + 1)] = jnp.ones((vx_s.shape[0], LANES), BF16)

    for hh in range(n_heads):
        q = q_ref[:, head_lanes[hh]]
        lane = lax.broadcasted_iota(jnp.int32, q.shape, 1)
        zero = jnp.zeros_like(q)
        q2_s[streams[2 * hh][1], :] = jnp.where(lane < DA_QK_DIM, q, zero)
        q2_s[streams[2 * hh + 1][1], :] = jnp.where(lane < DA_QK_DIM, zero, q)
    m_s[...] = jnp.full(m_s.shape, -1e30, F32)
    l_s[...] = jnp.zeros_like(l_s)
    acc_s[...] = jnp.zeros_like(acc_s)

    def process(start, width, diag_offset):
        def scores(st):
            hh, rows = streams[st]
            s_s[rows, 0:width] = _dot_nt(q2_s[rows, :], k_ref[pl.ds(start, width), head_lanes[hh]])

        lead = min(ATTN_SCORE_LEAD, len(streams))
        for st in range(lead):
            scores(st)
        for st, (hh, rows) in enumerate(streams):
            s = s_s[rows, 0:width]
            if diag_offset is not None:
                row = lax.broadcasted_iota(jnp.int32, s.shape, 0)
                col = lax.broadcasted_iota(jnp.int32, s.shape, 1)
                s = jnp.where(col <= row + diag_offset, s, -jnp.inf)
            m_prev = m_s[rows, :]
            m_new = jnp.maximum(m_prev, jnp.max(s, axis=1, keepdims=True))
            alpha = jnp.exp2(m_prev - m_new)
            p = jnp.exp2(s - jnp.concatenate([m_new] * (width // LANES), axis=1))
            pv = _dot(p.astype(BF16), vx_s[pl.ds(start, width), vx_lanes[hh]])
            acc_s[rows, :] = acc_s[rows, :] * alpha + pv[:, 0:LANES]
            l_s[rows, :] = l_s[rows, :] * alpha + pv[:, LANES:2 * LANES]
            m_s[rows, :] = m_new
            if st + lead < len(streams):
                scores(st + lead)

    def wide_body(t, carry):
        process(pl.multiple_of(t * (2 * tq), 2 * tq), 2 * tq, None)
        return carry

    lax.fori_loop(0, qi // 2, wide_body, 0)

    @pl.when(qi % 2 == 0)
    def _():
        process(pl.multiple_of(qi * tq, tq), tq, 0)

    @pl.when(qi % 2 == 1)
    def _():
        process(pl.multiple_of((qi - 1) * tq, tq), 2 * tq, tq)

    lam1 = jnp.exp(jnp.sum(lam_ref[0:1, :] * lam_ref[1:2, :], axis=1, keepdims=True))
    lam2 = jnp.exp(jnp.sum(lam_ref[2:3, :] * lam_ref[3:4, :], axis=1, keepdims=True))
    lam = lam1 - lam2 + lam_init
    for hh in range(n_heads):
        r0, r1 = streams[2 * hh][1], streams[2 * hh + 1][1]
        o = acc_s[r0, :] / l_s[r0, :] - lam * (acc_s[r1, :] / l_s[r1, :])
        o = _rms_rows(o, sub_ref[...]) * (1.0 - lam_init)
        o_ref[:, head_lanes[hh]] = o.astype(o_ref.dtype)


def _attention(q, k, v, lam_vecs, subln, lam_init):
    b, l_pad, _ = q.shape
    tq = ATTN_TILE
    nq = l_pad // tq
    nh = ATTN_HEADS_PER_STEP
    hw = nh * LANES
    n_streams = 2 * nh
    kv_spec = pl.BlockSpec((None, l_pad, hw), lambda bi, h, qi: (bi, 0, h), pipeline_mode=pl.Buffered(1))
    return pl.pallas_call(
        functools.partial(_attn_kernel, tq=tq, lam_init=lam_init),
        grid=(b, DA_HEADS // nh, nq),
        in_specs=[pl.BlockSpec(lam_vecs.shape, lambda bi, h, qi: (0, 0)),
                  pl.BlockSpec((None, tq, hw), lambda bi, h, qi: (bi, qi, h)),
                  kv_spec, kv_spec,
                  pl.BlockSpec((1, DA_V_DIM), lambda bi, h, qi: (0, 0))],
        out_specs=pl.BlockSpec((None, tq, hw), lambda bi, h, qi: (bi, qi, h)),
        out_shape=jax.ShapeDtypeStruct((b, l_pad, DA_WIDTH), BF16),
        scratch_shapes=[pltpu.VMEM((n_streams * tq, LANES), BF16), pltpu.VMEM((l_pad, 2 * hw), BF16),
                        pltpu.VMEM((n_streams * tq, LANES), F32), pltpu.VMEM((n_streams * tq, LANES), F32),
                        pltpu.VMEM((n_streams * tq, LANES), F32),
                        pltpu.VMEM((n_streams * tq, 2 * tq), F32)],
        compiler_params=pltpu.CompilerParams(dimension_semantics=("arbitrary", "arbitrary", "arbitrary"),
                                             vmem_limit_bytes=VMEM_LIMIT_BYTES),
        name="diff_attention",
    )(lam_vecs, q, k, v, subln.reshape(1, DA_V_DIM))


def _expand_heads(x, lo_mask):
    zero = jnp.zeros_like(x)
    return jnp.concatenate([jnp.where(lo_mask, x, zero), jnp.where(lo_mask, zero, x)], axis=0)


def _run(stages):
    try:
        while True:
            next(stages)
    except StopIteration as stop:
        return stop.value


def _run_interleaved(stages_a, stages_b):
    live = [stages_a, stages_b]
    results = [None, None]
    while live[0] is not None or live[1] is not None:
        for n in range(2):
            if live[n] is None:
                continue
            try:
                next(live[n])
            except StopIteration as stop:
                results[n] = stop.value
                live[n] = None
    return results


def _unit_lower_inverse(a_list):
    n = a_list[0].shape[0]
    eye = (lax.broadcasted_iota(jnp.int32, (n, n), 0) == lax.broadcasted_iota(jnp.int32, (n, n), 1)).astype(F32)
    mm = lambda x, y: _dot(x.astype(BF16), y.astype(BF16))
    t_list = [eye + a for a in a_list]
    p_list = [mm(a, a) for a in a_list]
    yield
    levels = int(math.log2(RW_CHUNK))
    for _ in range(levels - 2):
        pt_list = [mm(p, jnp.concatenate([p, t], axis=1)) for p, t in zip(p_list, t_list)]
        t_list = [t + pt[:, n:] for t, pt in zip(t_list, pt_list)]
        p_list = [pt[:, :n] for pt in pt_list]
        yield
    return [t + mm(p, t) for p, t in zip(p_list, t_list)]


_RW_CARRY = ("ar", "bke", "v2", "t_inv", "akv", "arbk", "g_end", "bonus", "gate")


def _rwkv_kernel(u_ref, next_ref, mix_ref, w2a2_ref, w0_ref, a0_ref, g2_ref, kk_ref, ka_ref, rk_ref,
                 lnw_ref, lnb_ref, ones_ref, tri_ref, o_ref, s_ref, *carry_refs):
    i = pl.program_id(1)
    carry = dict(zip(_RW_CARRY, carry_refs))
    c_len = RW_CHUNK
    w = RW_WIDTH
    n2 = 2 * c_len
    n_sub = u_ref.shape[0] // RW_ROWS
    n_chunks = RW_ROWS // c_len
    n_pairs = w // LANES
    pair_lanes = [slice(LANES * p, LANES * (p + 1)) for p in range(n_pairs)]
    seg_lanes = [slice(SEG_LANES * p, SEG_LANES * (p + 1)) for p in range(w // SEG_LANES)]

    @pl.when(i == 0)
    def _():
        s_ref[...] = jnp.zeros_like(s_ref)

    ones_bd = ones_ref[...]
    tri = tri_ref[...]
    r_i = lax.broadcasted_iota(jnp.int32, (n2, n2), 0)
    c_i = lax.broadcasted_iota(jnp.int32, (n2, n2), 1)
    t_row = jnp.where(r_i >= c_len, r_i - c_len, r_i)
    t_col = jnp.where(c_i >= c_len, c_i - c_len, c_i)
    strict = t_col < t_row
    incl = t_col <= t_row
    lo_mask = lax.broadcasted_iota(jnp.int32, (c_len, LANES), 1) < RW_HEAD

    def prepare(src_ref, base, prev_row):
        def shifted(lo, hi):
            u = src_ref[base:base + RW_ROWS, lo:hi]
            last_prev = prev_row(lo, hi)
            rolled = pltpu.roll(u, 1, 0)
            row = lax.broadcasted_iota(jnp.int32, u.shape, 0)
            prev = jnp.where(row == 0, last_prev, rolled)
            return u + (prev - u) * mix_ref[:, lo:hi]

        r = shifted(0, w)
        k = shifted(w, 2 * w)
        v = shifted(2 * w, 3 * w)
        lora_in = shifted(3 * w, 3 * w + LANES)
        g_in = shifted(3 * w + LANES, RW_IN_PAD)

        lane = lax.broadcasted_iota(jnp.int32, lora_in.shape, 1)
        lora_act = jnp.where(lane < D_DECAY_LORA, jnp.tanh(lora_in), lora_in)
        wa = _dot_split(lora_act, w2a2_ref[...])
        g = _dot(jax.nn.sigmoid(g_in).astype(BF16), g2_ref[...])
        yield
        w_pre = w0_ref[...] + wa[:, 0:w]
        lw = (-math.exp(-0.5)) * jax.nn.sigmoid(w_pre)
        a = jax.nn.sigmoid(a0_ref[...] + wa[:, w:2 * w])

        cums = []
        for c in range(n_chunks):
            lw_c = lw[c * c_len:(c + 1) * c_len]
            hi = lw_c.astype(BF16)
            mid_f = lw_c - hi.astype(F32)
            mid = mid_f.astype(BF16)
            lo = (mid_f - mid.astype(F32)).astype(BF16)
            cums.append(_dot(tri, hi) + (_dot(tri, mid) + _dot(tri, lo)))

        kk_raw = k * kk_ref[...]
        k_mod = k * (1.0 + (a - 1.0) * ka_ref[...])
        rkr = r * k_mod * rk_ref[...]
        ss_l = [_seg_sum(kk_raw[:, sl] * kk_raw[:, sl], ones_bd) for sl in seg_lanes]
        rk_l = [_seg_sum(rkr[:, sl], ones_bd) for sl in seg_lanes]
        yield
        kk = jnp.concatenate([kk_raw[:, sl] * lax.rsqrt(jnp.maximum(ss, 1e-24))
                              for sl, ss in zip(seg_lanes, ss_l)], axis=1)
        bonus = jnp.concatenate([s * v[:, sl] for sl, s in zip(seg_lanes, rk_l)], axis=1)
        a_neg = -kk
        b_vec = kk * a

        ar_l, bk_l, bke_l, v2_l, g_end_l = [], [], [], [], []
        for c in range(n_chunks):
            rs = slice(c * c_len, (c + 1) * c_len)
            cum = cums[c]
            cum_end = cum[c_len - 1:c_len, :]
            e_in = jnp.exp(cum)
            e_out = jnp.exp(-cum)
            e_end = jnp.exp(cum_end - cum)
            r_t = r[rs] * e_in
            a_t = a_neg[rs] * jnp.exp(cum - lw[rs])
            k_t = k_mod[rs] * e_out
            b_t = b_vec[rs] * e_out
            k_e = k_mod[rs] * e_end
            b_e = b_vec[rs] * e_end
            g_end_l.append(jnp.exp(cum_end))
            v_c = v[rs]
            for sl in pair_lanes:
                ar_l.append(jnp.concatenate([_expand_heads(a_t[:, sl], lo_mask), _expand_heads(r_t[:, sl], lo_mask)],
                                            axis=0).astype(BF16))
                bk_l.append(jnp.concatenate([_expand_heads(b_t[:, sl], lo_mask), _expand_heads(k_t[:, sl], lo_mask)],
                                            axis=0).astype(BF16))
                bke_l.append(jnp.concatenate([_expand_heads(b_e[:, sl], lo_mask), _expand_heads(k_e[:, sl], lo_mask)],
                                             axis=0).astype(BF16))
                v2_l.append(_expand_heads(v_c[:, sl], lo_mask).astype(BF16))
        aa_l = [_dot_nt(ar, bk) for ar, bk in zip(ar_l, bk_l)]
        yield
        akv_l = [_dot(jnp.where(strict, aa[0:n2, n2:2 * n2], 0.0).astype(BF16), v2) for aa, v2 in zip(aa_l, v2_l)]
        arbk_l = [jnp.concatenate([jnp.where(incl, aa[n2:2 * n2, 0:n2], 0.0),
                                   jnp.where(incl, aa[n2:2 * n2, n2:2 * n2], 0.0)], axis=1).astype(BF16)
                  for aa in aa_l]
        t_inv_l = yield from _unit_lower_inverse([jnp.where(strict, aa[0:n2, 0:n2], 0.0) for aa in aa_l])
        t_inv_l = [t.astype(BF16) for t in t_inv_l]
        yield
        return dict(ar=ar_l, bke=bke_l, v2=v2_l, g_end=g_end_l, t_inv=t_inv_l, akv=akv_l, arbk=arbk_l,
                    bonus=bonus, gate=g)

    def advance(j, pre):
        base = j * RW_ROWS
        y_chunks = []
        for c in range(n_chunks):
            idx = [c * n_pairs + p for p in range(n_pairs)]
            states = [s_ref[p] for p in range(n_pairs)]
            arh = [_dot_nt(pre["ar"][q], s.astype(BF16)) for q, s in zip(idx, states)]
            yield
            u_l = [_dot(pre["t_inv"][q], (h[0:n2] + pre["akv"][q]).astype(BF16)) for q, h in zip(idx, arh)]
            yield
            uv = [jnp.concatenate([u.astype(BF16), pre["v2"][q]], axis=0) for q, u in zip(idx, u_l)]
            y2 = [h[n2:2 * n2] + _dot(pre["arbk"][q], x) for q, h, x in zip(idx, arh, uv)]
            for p in range(n_pairs):
                s_ref[p] = states[p] * pre["g_end"][c][:, pair_lanes[p]] + _dot_tn(uv[p], pre["bke"][idx[p]])
            yield
            y_chunks.append(jnp.concatenate([y[0:c_len] + y[c_len:n2] for y in y2], axis=1))
        y = jnp.concatenate(y_chunks, axis=0)

        mu_l = [_seg_sum(y[:, sl], ones_bd) * (1.0 / RW_HEAD) for sl in seg_lanes]
        yield
        d_l = [y[:, sl] - mu for sl, mu in zip(seg_lanes, mu_l)]
        var_l = [_seg_sum(d * d, ones_bd) * (1.0 / RW_HEAD) for d in d_l]
        yield
        yn = jnp.concatenate([d * lax.rsqrt(var + GN_EPS) for d, var in zip(d_l, var_l)], axis=1)
        yn = yn * lnw_ref[...] + lnb_ref[...]
        o_ref[base:base + RW_ROWS, :] = ((yn + pre["bonus"]) * pre["gate"]).astype(o_ref.dtype)

    def save(pre):
        for name in _RW_CARRY:
            value, ref = pre[name], carry[name]
            if isinstance(value, list):
                for q, item in enumerate(value):
                    ref[q] = item
            else:
                ref[...] = value

    def load():
        return {name: ([ref[q] for q in range(ref.shape[0])] if len(ref.shape) == 3 else ref[...])
                for name, ref in carry.items()}

    def row_before(src_ref, row):
        return lambda lo, hi: src_ref[row:row + 1, lo:hi]

    @pl.when(i == 0)
    def _():
        save(_run(prepare(u_ref, 0, lambda lo, hi: jnp.zeros((1, hi - lo), F32))))

    pre = load()
    for j in range(n_sub):
        last_row = (j + 1) * RW_ROWS - 1
        if j + 1 < n_sub:
            upcoming = prepare(u_ref, (j + 1) * RW_ROWS, row_before(u_ref, last_row))
        else:
            upcoming = prepare(next_ref, 0, row_before(u_ref, last_row))
        pre, _ = _run_interleaved(upcoming, advance(j, pre))
    save(pre)


def _rwkv(u_rw, shift_mix_p, w2a2, w0, a0, g2_p, k_k, k_a, r_k, ln_w, ln_b, ones_bd, tri):
    b, l_pad, _ = u_rw.shape
    rows = RW_BLOCK_ROWS
    sub_per_block = rows // RW_ROWS
    last_sub = l_pad // RW_ROWS - 1
    n_prob = (RW_ROWS // RW_CHUNK) * (RW_WIDTH // LANES)
    n2 = 2 * RW_CHUNK
    vec = lambda x: x.reshape(1, -1).astype(F32)
    const = lambda bi, i: (0, 0)
    vec_spec = pl.BlockSpec((1, RW_WIDTH), const)
    carry_shapes = dict(
        ar=pltpu.VMEM((n_prob, 2 * n2, LANES), BF16), bke=pltpu.VMEM((n_prob, 2 * n2, LANES), BF16),
        v2=pltpu.VMEM((n_prob, n2, LANES), BF16), t_inv=pltpu.VMEM((n_prob, n2, n2), BF16),
        akv=pltpu.VMEM((n_prob, n2, LANES), F32), arbk=pltpu.VMEM((n_prob, n2, 2 * n2), BF16),
        g_end=pltpu.VMEM((RW_ROWS // RW_CHUNK, 1, RW_WIDTH), F32),
        bonus=pltpu.VMEM((RW_ROWS, RW_WIDTH), F32), gate=pltpu.VMEM((RW_ROWS, RW_WIDTH), F32))
    return pl.pallas_call(
        _rwkv_kernel,
        grid=(b, l_pad // rows),
        in_specs=[pl.BlockSpec((None, rows, RW_IN_PAD), lambda bi, i: (bi, i, 0)),
                  pl.BlockSpec((None, RW_ROWS, RW_IN_PAD),
                               lambda bi, i: (bi, jnp.minimum((i + 1) * sub_per_block, last_sub), 0)),
                  pl.BlockSpec((1, RW_IN_PAD), const),
                  pl.BlockSpec(w2a2.shape, const),
                  vec_spec, vec_spec,
                  pl.BlockSpec(g2_p.shape, const),
                  vec_spec, vec_spec, vec_spec, vec_spec, vec_spec,
                  pl.BlockSpec((SEG_LANES, SEG_LANES), const),
                  pl.BlockSpec((RW_CHUNK, RW_CHUNK), const)],
        out_specs=pl.BlockSpec((None, rows, RW_WIDTH), lambda bi, i: (bi, i, 0)),
        out_shape=jax.ShapeDtypeStruct((b, l_pad, RW_WIDTH), BF16),
        scratch_shapes=[pltpu.VMEM((RW_WIDTH // LANES, LANES, LANES), F32)] + [carry_shapes[n] for n in _RW_CARRY],
        compiler_params=pltpu.CompilerParams(dimension_semantics=("arbitrary", "arbitrary"),
                                             vmem_limit_bytes=VMEM_LIMIT_BYTES),
        name="rwkv7",
    )(u_rw, u_rw, vec(shift_mix_p), w2a2, vec(w0), vec(a0), g2_p, vec(k_k), vec(k_a), vec(r_k),
      vec(ln_w), vec(ln_b), ones_bd, tri)


def _rope_lane_tables(length):
    pos = jnp.arange(length, dtype=F32)
    inv_freq = ROPE_THETA ** (-jnp.arange(0, ROPE_DIM, 2, dtype=F32) / ROPE_DIM)
    ang = pos[:, None] * inv_freq[None, :]
    cos, sin = jnp.cos(ang), jnp.sin(ang)
    half = ROPE_DIM // 2
    rest = DA_QK_DIM - ROPE_DIM
    ones = jnp.ones((length, rest), F32)
    zeros = lambda n: jnp.zeros((length, n), F32)
    c = jnp.concatenate([cos, cos, ones], axis=1)
    s1 = jnp.concatenate([-sin, zeros(half + rest)], axis=1)
    s2 = jnp.concatenate([zeros(half), sin, zeros(rest)], axis=1)
    rep = LANES // DA_QK_DIM
    return jnp.tile(c, (1, rep)), jnp.tile(s1, (1, rep)), jnp.tile(s2, (1, rep))


def kernel(x, meta_tokens, ffn1_norm, ffn1_w_gate, ffn1_w_up, ffn1_w_down, mix_norm, w_in, da_q_norm, da_k_norm, da_lambda_q1, da_lambda_k1, da_lambda_q2, da_lambda_k2, da_subln, rw_shift_mix, rw_w0, rw_w2, rw_a0, rw_a2, rw_g2, rw_k_k, rw_k_a, rw_r_k, rw_ln_w, rw_ln_b, w_out, ffn2_norm, ffn2_w_gate, ffn2_w_up, ffn2_w_down):
    bsz, t, d = x.shape
    depth = w_in.shape[0]
    l = N_META + t
    l_pad = -(-l // Q_BLOCK) * Q_BLOCK
    assert d == D_MODEL and l_pad % ROW_TILE == 0 and l_pad % ATTN_TILE == 0 and l_pad % RW_BLOCK_ROWS == 0
    h = x.reshape(bsz * t, d)
    meta = meta_tokens.astype(x.dtype)

    rope_c, rope_s1, rope_s2 = _rope_lane_tables(l_pad)
    lane_head = jnp.arange(SEG_LANES) // RW_HEAD
    ones_bd =(lane_head[:, None] == lane_head[None, :]).astype(BF16)
    tri = (jnp.arange(RW_CHUNK)[:, None] >= jnp.arange(RW_CHUNK)[None, :]).astype(BF16)

    for layer in range(depth):
        lam_init = 0.8 - 0.6 * math.exp(-0.3 * layer)
        at = lambda a: a[layer]
        shift_mix_p = jnp.pad(rw_shift_mix[layer], (0, RW_IN_PAD - RW_IN))
        zeros_lora = jnp.zeros((D_DECAY_LORA, RW_WIDTH), F32)
        w2a2 = jnp.concatenate([jnp.concatenate([rw_w2[layer], zeros_lora], axis=1),
                                jnp.concatenate([zeros_lora, rw_a2[layer]], axis=1)], axis=0)
        g2_p = jnp.pad(rw_g2[layer], ((0, RW_LORA_PAD - D_GATE_LORA), (0, 0))).astype(BF16)
        q_gain = jnp.tile(da_q_norm[layer], DA_QK_WIDTH // DA_QK_DIM).reshape(1, DA_QK_WIDTH)
        k_gain = jnp.tile(da_k_norm[layer], DA_QK_WIDTH // DA_QK_DIM).reshape(1, DA_QK_WIDTH)
        lam_vecs = jnp.stack([da_lambda_q1[layer], da_lambda_k1[layer],
                              da_lambda_q2[layer], da_lambda_k2[layer]]).astype(F32)

        h = _ffn(h, ffn1_norm[layer], at(ffn1_w_gate), at(ffn1_w_up), at(ffn1_w_down),
                 assemble=(meta, l_pad, t) if layer == 0 else None)
        q, k, v, u_rw = _mix_in(h, mix_norm[layer], at(w_in), q_gain, k_gain,
                                rope_c, rope_s1, rope_s2, ones_bd, l_pad)
        shape3 = lambda a: a.reshape(bsz, l_pad, a.shape[-1])
        o_da = _attention(shape3(q), shape3(k), shape3(v), lam_vecs, da_subln[layer], lam_init)
        o_rw = _rwkv(shape3(u_rw), shift_mix_p, w2a2, rw_w0[layer], rw_a0[layer], g2_p,
                     rw_k_k[layer], rw_k_a[layer], rw_r_k[layer], rw_ln_w[layer], rw_ln_b[layer],
                     ones_bd, tri)
        last = layer == depth - 1
        h = _ffn(h, ffn2_norm[layer], at(ffn2_w_gate), at(ffn2_w_up), at(ffn2_w_down),
                 mix=(o_da.reshape(bsz * l_pad, DA_WIDTH), o_rw.reshape(bsz * l_pad, RW_WIDTH), at(w_out)),
                 compact=(l_pad, N_META, t) if last else None)
    return h.reshape(bsz, t, d)
```

```python
import functools
import math

import jax
import jax.numpy as jnp
from jax import lax
from jax.experimental import pallas as pl
from jax.experimental.pallas import tpu as pltpu

F32 = jnp.float32
BF16 = jnp.bfloat16

D_MODEL = 1024
N_META = 16
Q_BLOCK = 128
ROPE_THETA = 500000.0
NORM_EPS = 1e-6
DA_HEADS = 4
DA_QK_DIM = 64
DA_V_DIM = 2 * DA_QK_DIM
DA_WIDTH = DA_HEADS * DA_V_DIM
DA_QK_WIDTH = DA_HEADS * 2 * DA_QK_DIM
ROPE_DIM = DA_QK_DIM // 4
RW_HEAD = 64
RW_WIDTH = D_MODEL - DA_WIDTH
RW_HEADS = RW_WIDTH // RW_HEAD
D_DECAY_LORA = 64
D_AAA_LORA = 64
D_GATE_LORA = 160
GN_EPS = 64e-5
DA_IN = 2 * DA_QK_WIDTH + DA_WIDTH
RW_IN = 3 * RW_WIDTH + D_DECAY_LORA + D_AAA_LORA + D_GATE_LORA
D_FF = 2816

LANES = 128
SUBLANES = 8
SEG_LANES = 256
VMEM_LIMIT_BYTES = 56 * 1024 * 1024

ROW_TILE = 640
COMPACT_ROW_TILE = 512
FF_CHUNK = 256
WEIGHT_SLAB_ROWS = 256
ATTN_TILE = 640
ATTN_HEADS_PER_STEP = 2
ATTN_SCORE_LEAD = 2
RW_CHUNK = 64
RW_ROWS = 2 * RW_CHUNK
RW_BLOCK_ROWS = 5 * RW_ROWS
RW_LORA_PAD = 2 * LANES
RW_IN_PAD = 3 * RW_WIDTH + LANES + RW_LORA_PAD
N_IN_PAD = DA_IN + RW_IN_PAD

_NT = (((1,), (1,)), ((), ()))
_TN = (((0,), (0,)), ((), ()))


def _dot(a, b):
    return jnp.dot(a, b, preferred_element_type=F32)


def _dot_nt(a, b):
    return lax.dot_general(a, b, _NT, preferred_element_type=F32)


def _dot_tn(a, b):
    return lax.dot_general(a, b, _TN, preferred_element_type=F32)


def _split2(x):
    hi = x.astype(BF16)
    lo = (x - hi.astype(F32)).astype(BF16)
    return hi, lo


def _dot_split(a, b):
    a_hi, a_lo = _split2(a)
    b_hi, b_lo = _split2(b)
    return _dot(a_hi, b_hi) + (_dot(a_hi, b_lo) + _dot(a_lo, b_hi))


def _seg_sum(x, ones_bd):
    hi, lo = _split2(x)
    return _dot(hi, ones_bd) + _dot(lo, ones_bd)


def _rms_rows(h, gain):
    ms = jnp.mean(h * h, axis=-1, keepdims=True)
    return h * lax.rsqrt(ms + NORM_EPS) * gain


def _fetch_weights_bf16(jobs):
    slabs = []
    used = {}
    for hbm, vmem, stage, sem in jobs:
        rows, cols = hbm.shape
        assert rows % WEIGHT_SLAB_ROWS == 0 and stage.shape == (2, WEIGHT_SLAB_ROWS, cols), (hbm.shape, stage.shape)
        for r0 in range(0, rows, WEIGHT_SLAB_ROWS):
            slot = used.get(id(stage), 0) % 2
            used[id(stage)] = used.get(id(stage), 0) + 1
            slabs.append((hbm, vmem, stage, sem, r0, cols, slot))

    def copy(k):
        hbm, _, stage, sem, r0, _, slot = slabs[k]
        return pltpu.make_async_copy(hbm.at[r0:r0 + WEIGHT_SLAB_ROWS, :], stage.at[slot], sem.at[slot])

    copy(0).start()
    for k, (_, vmem, stage, _, r0, cols, slot) in enumerate(slabs):
        if k + 1 < len(slabs):
            copy(k + 1).start()
        copy(k).wait()
        vmem[r0:r0 + WEIGHT_SLAB_ROWS, 0:cols] = stage[slot].astype(BF16)
    for hbm, vmem, _, _ in jobs:
        if vmem.shape[1] > hbm.shape[1]:
            vmem[:, hbm.shape[1]:] = jnp.zeros((vmem.shape[0], vmem.shape[1] - hbm.shape[1]), BF16)


def _ffn_kernel(*refs, has_mix, assemble, layer):
    n_w = 4 if has_mix else 3
    n_in = len(refs) - (1 + 2 + n_w + 4)
    w_hbm = [w.at[layer] for w in refs[n_in - n_w:n_in]]
    o_ref, xn_ref, act_ref = refs[n_in:n_in + 3]
    w_vmem = refs[n_in + 3:n_in + 3 + n_w]
    stage_wide, stage_narrow, sem_wide, sem_narrow = refs[n_in + 3 + n_w:]
    g_ref = refs[n_in - n_w - 1]
    wg_ref, wu_ref, wd_ref = w_vmem[0:3]

    @pl.when(pl.program_id(0) == 0)
    def _():
        stages = [(stage_wide, sem_wide), (stage_wide, sem_wide), (stage_narrow, sem_narrow),
                  (stage_narrow, sem_narrow)]
        _fetch_weights_bf16([(h, v) + s for h, v, s in zip(w_hbm, w_vmem, stages)])

    if has_mix:
        h_ref, oda_ref, orw_ref = refs[0:3]
        wout_ref = w_vmem[3]
        h = (h_ref[...] + _dot(oda_ref[...], wout_ref[0:DA_WIDTH, :])
             + _dot(orw_ref[...], wout_ref[DA_WIDTH:DA_WIDTH + RW_WIDTH, :]))
        o_ref[...] = h
        res_ref = o_ref
    elif assemble is not None:
        x_ref, meta_ref = refs[0:2]
        tiles, n_prefix, last_shift, last_valid = assemble
        tm, d = x_ref.shape
        il = pl.program_id(0) % tiles
        x = x_ref[...]
        first, last = il == 0, il == tiles - 1
        h = jnp.where(first, pltpu.roll(x, n_prefix, 0), jnp.where(last, pltpu.roll(x, tm - last_shift, 0), x))
        row = lax.broadcasted_iota(jnp.int32, (tm, d), 0)
        prefix = jnp.concatenate([meta_ref[...], jnp.zeros((tm - n_prefix, d), F32)], axis=0)
        h = jnp.where(first & (row < n_prefix), prefix, h)
        h = jnp.where(last & (row >= last_valid), 0.0, h)
        o_ref[...] = h
        res_ref = o_ref
    else:
        h_ref = refs[0]
        h = h_ref[...]
        res_ref = h_ref
    xn_ref[...] = _rms_rows(h, g_ref[...]).astype(BF16)
    d_ff = wg_ref.shape[1]
    for c in range(d_ff // FF_CHUNK):
        cols = slice(c * FF_CHUNK, (c + 1) * FF_CHUNK)
        g = _dot(xn_ref[...], wg_ref[:, cols])
        u = _dot(xn_ref[...], wu_ref[:, cols])
        act_ref[:, cols] = ((g * jax.nn.sigmoid(g)) * u).astype(BF16)
    o_ref[...] = res_ref[...] + 0.5 * _dot(act_ref[...], wd_ref[...])


def _ffn(h, norm_g, wg, wu, wd, layer, mix=None, compact=None, assemble=None):
    m, d = h.shape
    d_ff = wg.shape[-1]
    row = lambda i: (i, 0)
    const2 = lambda i: (0, 0)
    kernel_assemble = None
    if assemble is not None:
        prefix, l_pad, n_rows = assemble
        n_prefix = prefix.shape[0]
        tm = ROW_TILE
        tiles = l_pad // tm
        n_seq = m // n_rows
        last_start = (tiles - 1) * tm - n_prefix
        assert tiles >= 2 and n_rows >= tm and n_prefix % SUBLANES == 0 and n_rows % SUBLANES == 0
        assert 0 <= last_start - (n_rows - tm) < tm
        kernel_assemble = (tiles, n_prefix, last_start - (n_rows - tm), n_prefix + n_rows - (tiles - 1) * tm)
        m_out = n_seq * l_pad
        window = lambda i: (pl.multiple_of(
            (i // tiles) * n_rows + jnp.clip((i % tiles) * tm - n_prefix, 0, n_rows - tm), SUBLANES), 0)
        row_spec = lambda width: pl.BlockSpec((pl.Element(tm), pl.Element(width)), window)
    elif compact is None:
        tm = ROW_TILE
        m_out = m
        row_spec = lambda width: pl.BlockSpec((tm, width), row)
    else:
        l_pad, first_row, n_rows = compact
        tm = COMPACT_ROW_TILE
        align = 2 * SUBLANES
        assert n_rows % tm == 0 and first_row % align == 0 and l_pad % align == 0 and tm % align == 0
        tiles = n_rows // tm
        m_out = (m // l_pad) * n_rows
        window = lambda i: (pl.multiple_of((i // tiles) * l_pad + first_row + (i % tiles) * tm, align), 0)
        row_spec = lambda width: pl.BlockSpec((pl.Element(tm), pl.Element(width)), window)
    in_specs = [row_spec(d)]
    args = [h]
    if assemble is not None:
        in_specs.append(pl.BlockSpec(prefix.shape, const2))
        args.append(prefix)
    weights = [wg, wu, wd]
    if mix is not None:
        o_da, o_rw, w_out = mix
        in_specs += [row_spec(DA_WIDTH), row_spec(RW_WIDTH)]
        args += [o_da, o_rw]
        weights.append(w_out)
    in_specs += [pl.BlockSpec((1, d), const2)] + [pl.BlockSpec(memory_space=pl.ANY)] * len(weights)
    args += [norm_g.reshape(1, d)] + weights
    return pl.pallas_call(
        functools.partial(_ffn_kernel, has_mix=mix is not None, assemble=kernel_assemble, layer=layer),
        grid=(m_out // tm,),
        in_specs=in_specs,
        out_specs=pl.BlockSpec((tm, d), row),
        out_shape=jax.ShapeDtypeStruct((m_out, d), F32),
        scratch_shapes=([pltpu.VMEM((tm, d), BF16), pltpu.VMEM((tm, d_ff), BF16)]
                        + [pltpu.VMEM(w.shape[-2:], BF16) for w in weights]
                        + [pltpu.VMEM((2, WEIGHT_SLAB_ROWS, d_ff), F32), pltpu.VMEM((2, WEIGHT_SLAB_ROWS, d), F32),
                           pltpu.SemaphoreType.DMA((2,)), pltpu.SemaphoreType.DMA((2,))]),
        compiler_params=pltpu.CompilerParams(dimension_semantics=("arbitrary",),
                                             vmem_limit_bytes=VMEM_LIMIT_BYTES),
        name="ffn_mix" if mix is not None else "ffn",
    )(*args)


def _qk_prep(u, gain_ref, c, s1, s2, ones_bd, scale, out_ref):
    rep = SEG_LANES // LANES
    c, s1, s2 = (jnp.concatenate([t] * rep, axis=1) for t in (c, s1, s2))
    for j in range(DA_QK_WIDTH // SEG_LANES):
        sl = slice(SEG_LANES * j, SEG_LANES * (j + 1))
        x = u[:, sl]
        ss = _seg_sum(x * x, ones_bd)
        xn = x * lax.rsqrt(ss * (1.0 / DA_QK_DIM) + NORM_EPS) * gain_ref[:, sl]
        half = ROPE_DIM // 2
        xr = xn * c + pltpu.roll(xn, SEG_LANES - half, 1) * s1 + pltpu.roll(xn, half, 1) * s2
        out_ref[:, sl] = (xr * scale).astype(BF16)


def _mixin_kernel(h_ref, g_ref, win_hbm, qg_ref, kg_ref, c_ref, s1_ref, s2_ref, ones_ref,
                  q_ref, k_ref, v_ref, urw_ref, win_ref, stage_ref, sem_ref, *, q_scale, layer):
    @pl.when(pl.program_id(0) == 0)
    def _():
        _fetch_weights_bf16([(win_hbm.at[layer], win_ref, stage_ref, sem_ref)])

    xn = _rms_rows(h_ref[...], g_ref[...]).astype(BF16)
    qw = DA_QK_WIDTH
    c, s1, s2, ones_bd = c_ref[...], s1_ref[...], s2_ref[...], ones_ref[...]
    uq = _dot(xn, win_ref[:, 0:qw])
    uk = _dot(xn, win_ref[:, qw:2 * qw])
    _qk_prep(uq, qg_ref, c, s1, s2, ones_bd, q_scale, q_ref)
    _qk_prep(uk, kg_ref, c, s1, s2, ones_bd, 1.0, k_ref)
    v_ref[...] = _dot(xn, win_ref[:, 2 * qw:DA_IN]).astype(BF16)
    urw_ref[...] = _dot(xn, win_ref[:, DA_IN:N_IN_PAD])


def _mix_in(h, norm_g, w_in, layer, q_gain, k_gain, rope_c, rope_s1, rope_s2, ones_bd, l_pad):
    m, d = h.shape
    tm = ROW_TILE
    tiles_per_seq = l_pad // tm
    row = lambda i: (i, 0)
    pos = lambda i: (i % tiles_per_seq, 0)
    const2 = lambda i: (0, 0)
    q_scale = DA_QK_DIM ** -0.5 * math.log2(math.e)
    return pl.pallas_call(
        functools.partial(_mixin_kernel, q_scale=q_scale, layer=layer),
        grid=(m // tm,),
        in_specs=[pl.BlockSpec((tm, d), row), pl.BlockSpec((1, d), const2),
                  pl.BlockSpec(memory_space=pl.ANY),
                  pl.BlockSpec((1, DA_QK_WIDTH), const2), pl.BlockSpec((1, DA_QK_WIDTH), const2),
                  pl.BlockSpec((tm, LANES), pos), pl.BlockSpec((tm, LANES), pos),
                  pl.BlockSpec((tm, LANES), pos), pl.BlockSpec((SEG_LANES, SEG_LANES), const2)],
        out_specs=[pl.BlockSpec((tm, DA_QK_WIDTH), row), pl.BlockSpec((tm, DA_QK_WIDTH), row),
                   pl.BlockSpec((tm, DA_WIDTH), row), pl.BlockSpec((tm, RW_IN_PAD), row)],
        out_shape=[jax.ShapeDtypeStruct((m, DA_QK_WIDTH), BF16), jax.ShapeDtypeStruct((m, DA_QK_WIDTH), BF16),
                   jax.ShapeDtypeStruct((m, DA_WIDTH), BF16), jax.ShapeDtypeStruct((m, RW_IN_PAD), F32)],
        scratch_shapes=[pltpu.VMEM((d, N_IN_PAD), BF16), pltpu.VMEM((2, WEIGHT_SLAB_ROWS, w_in.shape[-1]), F32),
                        pltpu.SemaphoreType.DMA((2,))],
        compiler_params=pltpu.CompilerParams(dimension_semantics=("arbitrary",),
                                             vmem_limit_bytes=VMEM_LIMIT_BYTES),
        name="mix_in",
    )(h, norm_g.reshape(1, d), w_in, q_gain, k_gain, rope_c, rope_s1, rope_s2, ones_bd)


def _attn_kernel(lam_ref, q_ref, k_ref, v_ref, sub_ref, o_ref, q2_s, vx_s, m_s, l_s, acc_s, s_s, *, tq, lam_init):
    qi = pl.program_id(2)
    n_heads = q_ref.shape[1] // LANES
    head_lanes = [slice(LANES * hh, LANES * (hh + 1)) for hh in range(n_heads)]
    vx_lanes = [slice(2 * LANES * hh, 2 * LANES * (hh + 1)) for hh in range(n_heads)]
    streams = [(hh, slice((2 * hh + c) * tq, (2 * hh + c + 1) * tq)) for hh in range(n_heads) for c in range(2)]

    @pl.when(qi == 0)
    def _():
        for hh in range(n_heads):
            vx_s[:, 2 * LANES * hh:2 * LANES * hh + LANES] = v_ref[:, head_lanes[hh]]
            vx_s[:, 2 * LANES * hh + LANES:2 * LANES * (hh + 1)] = jnp.ones((vx_s.shape[0], LANES), BF16)

    for hh in range(n_heads):
        q = q_ref[:, head_lanes[hh]]
        lane = lax.broadcasted_iota(jnp.int32, q.shape, 1)
        zero = jnp.zeros_like(q)
        q2_s[streams[2 * hh][1], :] = jnp.where(lane < DA_QK_DIM, q, zero)
        q2_s[streams[2 * hh + 1][1], :] = jnp.where(lane < DA_QK_DIM, zero, q)
    m_s[...] = jnp.full(m_s.shape, -1e30, F32)
    l_s[...] = jnp.zeros_like(l_s)
    acc_s[...] = jnp.zeros_like(acc_s)

    def process(start, width, diag_offset):
        def scores(st):
            hh, rows = streams[st]
            s_s[rows, 0:width] = _dot_nt(q2_s[rows, :], k_ref[pl.ds(start, width), head_lanes[hh]])

        lead = min(ATTN_SCORE_LEAD, len(streams))
        for st in range(lead):
            scores(st)
        for st, (hh, rows) in enumerate(streams):
            s = s_s[rows, 0:width]
            if diag_offset is not None:
                row = lax.broadcasted_iota(jnp.int32, s.shape, 0)
                col = lax.broadcasted_iota(jnp.int32, s.shape, 1)
                s = jnp.where(col <= row + diag_offset, s, -jnp.inf)
            m_prev = m_s[rows, :]
            m_new = jnp.maximum(m_prev, jnp.max(s, axis=1, keepdims=True))
            alpha = jnp.exp2(m_prev - m_new)
            p = jnp.exp2(s - jnp.concatenate([m_new] * (width // LANES), axis=1))
            pv = _dot(p.astype(BF16), vx_s[pl.ds(start, width), vx_lanes[hh]])
            acc_s[rows, :] = acc_s[rows, :] * alpha + pv[:, 0:LANES]
            l_s[rows, :] = l_s[rows, :] * alpha + pv[:, LANES:2 * LANES]
            m_s[rows, :] = m_new
            if st + lead < len(streams):
                scores(st + lead)

    def wide_body(t, carry):
        process(pl.multiple_of(t * (2 * tq), 2 * tq), 2 * tq, None)
        return carry

    lax.fori_loop(0, qi // 2, wide_body, 0)

    @pl.when(qi % 2 == 0)
    def _():
        process(pl.multiple_of(qi * tq, tq), tq, 0)

    @pl.when(qi % 2 == 1)
    def _():
        process(pl.multiple_of((qi - 1) * tq, tq), 2 * tq, tq)

    lam1 = jnp.exp(jnp.sum(lam_ref[0:1, :] * lam_ref[1:2, :], axis=1, keepdims=True))
    lam2 = jnp.exp(jnp.sum(lam_ref[2:3, :] * lam_ref[3:4, :], axis=1, keepdims=True))
    lam = lam1 - lam2 + lam_init
    for hh in range(n_heads):
        r0, r1 = streams[2 * hh][1], streams[2 * hh + 1][1]
        o = acc_s[r0, :] / l_s[r0, :] - lam * (acc_s[r1, :] / l_s[r1, :])
        o = _rms_rows(o, sub_ref[...]) * (1.0 - lam_init)
        o_ref[:, head_lanes[hh]] = o.astype(o_ref.dtype)


def _attention(q, k, v, lam_vecs, subln, lam_init):
    b, l_pad, _ = q.shape
    tq = ATTN_TILE
    nq = l_pad // tq
    nh = ATTN_HEADS_PER_STEP
    hw = nh * LANES
    n_streams = 2 * nh
    kv_spec = pl.BlockSpec((None, l_pad, hw), lambda bi, h, qi: (bi, 0, h), pipeline_mode=pl.Buffered(1))
    return pl.pallas_call(
        functools.partial(_attn_kernel, tq=tq, lam_init=lam_init),
        grid=(b, DA_HEADS // nh, nq),
        in_specs=[pl.BlockSpec(lam_vecs.shape, lambda bi, h, qi: (0, 0)),
                  pl.BlockSpec((None, tq, hw), lambda bi, h, qi: (bi, qi, h)),
                  kv_spec, kv_spec,
                  pl.BlockSpec((1, DA_V_DIM), lambda bi, h, qi: (0, 0))],
        out_specs=pl.BlockSpec((None, tq, hw), lambda bi, h, qi: (bi, qi, h)),
        out_shape=jax.ShapeDtypeStruct((b, l_pad, DA_WIDTH), BF16),
        scratch_shapes=[pltpu.VMEM((n_streams * tq, LANES), BF16), pltpu.VMEM((l_pad, 2 * hw), BF16),
                        pltpu.VMEM((n_streams * tq, LANES), F32), pltpu.VMEM((n_streams * tq, LANES), F32),
                        pltpu.VMEM((n_streams * tq, LANES), F32),
                        pltpu.VMEM((n_streams * tq, 2 * tq), F32)],
        compiler_params=pltpu.CompilerParams(dimension_semantics=("arbitrary", "arbitrary", "arbitrary"),
                                             vmem_limit_bytes=VMEM_LIMIT_BYTES),
        name="diff_attention",
    )(lam_vecs, q, k, v, subln.reshape(1, DA_V_DIM))


def _expand_heads(x, lo_mask):
    zero = jnp.zeros_like(x)
    return jnp.concatenate([jnp.where(lo_mask, x, zero), jnp.where(lo_mask, zero, x)], axis=0)


def _run(stages):
    try:
        while True:
            next(stages)
    except StopIteration as stop:
        return stop.value


def _run_interleaved(stages_a, stages_b):
    live = [stages_a, stages_b]
    results = [None, None]
    while live[0] is not None or live[1] is not None:
        for n in range(2):
            if live[n] is None:
                continue
            try:
                next(live[n])
            except StopIteration as stop:
                results[n] = stop.value
                live[n] = None
    return results


def _unit_lower_inverse(a_list):
    n = a_list[0].shape[0]
    eye = (lax.broadcasted_iota(jnp.int32, (n, n), 0) == lax.broadcasted_iota(jnp.int32, (n, n), 1)).astype(F32)
    mm = lambda x, y: _dot(x.astype(BF16), y.astype(BF16))
    t_list = [eye + a for a in a_list]
    p_list = [mm(a, a) for a in a_list]
    yield
    levels = int(math.log2(RW_CHUNK))
    for _ in range(levels - 2):
        pt_list = [mm(p, jnp.concatenate([p, t], axis=1)) for p, t in zip(p_list, t_list)]
        t_list = [t + pt[:, n:] for t, pt in zip(t_list, pt_list)]
        p_list = [pt[:, :n] for pt in pt_list]
        yield
    return [t + mm(p, t) for p, t in zip(p_list, t_list)]


_RW_CARRY = ("ar", "bke", "v2", "t_inv", "akv", "arbk", "g_end", "bonus", "gate")


def _rwkv_kernel(u_ref, next_ref, mix_ref, w2a2_ref, w0_ref, a0_ref, g2_ref, kk_ref, ka_ref, rk_ref,
                 lnw_ref, lnb_ref, ones_ref, tri_ref, o_ref, s_ref, *carry_refs):
    i = pl.program_id(1)
    carry = dict(zip(_RW_CARRY, carry_refs))
    c_len = RW_CHUNK
    w = RW_WIDTH
    n2 = 2 * c_len
    n_sub = u_ref.shape[0] // RW_ROWS
    n_chunks = RW_ROWS // c_len
    n_pairs = w // LANES
    pair_lanes = [slice(LANES * p, LANES * (p + 1)) for p in range(n_pairs)]
    seg_lanes = [slice(SEG_LANES * p, SEG_LANES * (p + 1)) for p in range(w // SEG_LANES)]

    @pl.when(i == 0)
    def _():
        s_ref[...] = jnp.zeros_like(s_ref)

    ones_bd = ones_ref[...]
    tri = tri_ref[...]
    r_i = lax.broadcasted_iota(jnp.int32, (n2, n2), 0)
    c_i = lax.broadcasted_iota(jnp.int32, (n2, n2), 1)
    t_row = jnp.where(r_i >= c_len, r_i - c_len, r_i)
    t_col = jnp.where(c_i >= c_len, c_i - c_len, c_i)
    strict = t_col < t_row
    incl = t_col <= t_row
    lo_mask = lax.broadcasted_iota(jnp.int32, (c_len, LANES), 1) < RW_HEAD

    def prepare(src_ref, base, prev_row):
        def shifted(lo, hi):
            u = src_ref[base:base + RW_ROWS, lo:hi]
            last_prev = prev_row(lo, hi)
            rolled = pltpu.roll(u, 1, 0)
            row = lax.broadcasted_iota(jnp.int32, u.shape, 0)
            prev = jnp.where(row == 0, last_prev, rolled)
            return u + (prev - u) * mix_ref[:, lo:hi]

        r = shifted(0, w)
        k = shifted(w, 2 * w)
        v = shifted(2 * w, 3 * w)
        lora_in = shifted(3 * w, 3 * w + LANES)
        g_in = shifted(3 * w + LANES, RW_IN_PAD)

        lane = lax.broadcasted_iota(jnp.int32, lora_in.shape, 1)
        lora_act = jnp.where(lane < D_DECAY_LORA, jnp.tanh(lora_in), lora_in)
        wa = _dot_split(lora_act, w2a2_ref[...])
        g = _dot(jax.nn.sigmoid(g_in).astype(BF16), g2_ref[...])
        yield
        w_pre = w0_ref[...] + wa[:, 0:w]
        lw = (-math.exp(-0.5)) * jax.nn.sigmoid(w_pre)
        a = jax.nn.sigmoid(a0_ref[...] + wa[:, w:2 * w])

        cums = []
        for c in range(n_chunks):
            lw_c = lw[c * c_len:(c + 1) * c_len]
            hi = lw_c.astype(BF16)
            mid_f = lw_c - hi.astype(F32)
            mid = mid_f.astype(BF16)
            lo = (mid_f - mid.astype(F32)).astype(BF16)
            cums.append(_dot(tri, hi) + (_dot(tri, mid) + _dot(tri, lo)))

        kk_raw = k * kk_ref[...]
        k_mod = k * (1.0 + (a - 1.0) * ka_ref[...])
        rkr = r * k_mod * rk_ref[...]
        ss_l = [_seg_sum(kk_raw[:, sl] * kk_raw[:, sl], ones_bd) for sl in seg_lanes]
        rk_l = [_seg_sum(rkr[:, sl], ones_bd) for sl in seg_lanes]
        yield
        kk = jnp.concatenate([kk_raw[:, sl] * lax.rsqrt(jnp.maximum(ss, 1e-24))
                              for sl, ss in zip(seg_lanes, ss_l)], axis=1)
        bonus = jnp.concatenate([s * v[:, sl] for sl, s in zip(seg_lanes, rk_l)], axis=1)
        a_neg = -kk
        b_vec = kk * a

        ar_l, bk_l, bke_l, v2_l, g_end_l = [], [], [], [], []
        for c in range(n_chunks):
            rs = slice(c * c_len, (c + 1) * c_len)
            cum = cums[c]
            cum_end = cum[c_len - 1:c_len, :]
            e_in = jnp.exp(cum)
            e_out = jnp.exp(-cum)
            e_end = jnp.exp(cum_end - cum)
            r_t = r[rs] * e_in
            a_t = a_neg[rs] * jnp.exp(cum - lw[rs])
            k_t = k_mod[rs] * e_out
            b_t = b_vec[rs] * e_out
            k_e = k_mod[rs] * e_end
            b_e = b_vec[rs] * e_end
            g_end_l.append(jnp.exp(cum_end))
            v_c = v[rs]
            for sl in pair_lanes:
                ar_l.append(jnp.concatenate([_expand_heads(a_t[:, sl], lo_mask), _expand_heads(r_t[:, sl], lo_mask)],
                                            axis=0).astype(BF16))
                bk_l.append(jnp.concatenate([_expand_heads(b_t[:, sl], lo_mask), _expand_heads(k_t[:, sl], lo_mask)],
                                            axis=0).astype(BF16))
                bke_l.append(jnp.concatenate([_expand_heads(b_e[:, sl], lo_mask), _expand_heads(k_e[:, sl], lo_mask)],
                                             axis=0).astype(BF16))
                v2_l.append(_expand_heads(v_c[:, sl], lo_mask).astype(BF16))
        aa_l = [_dot_nt(ar, bk) for ar, bk in zip(ar_l, bk_l)]
        yield
        akv_l = [_dot(jnp.where(strict, aa[0:n2, n2:2 * n2], 0.0).astype(BF16), v2) for aa, v2 in zip(aa_l, v2_l)]
        arbk_l = [jnp.concatenate([jnp.where(incl, aa[n2:2 * n2, 0:n2], 0.0),
                                   jnp.where(incl, aa[n2:2 * n2, n2:2 * n2], 0.0)], axis=1).astype(BF16)
                  for aa in aa_l]
        t_inv_l = yield from _unit_lower_inverse([jnp.where(strict, aa[0:n2, 0:n2], 0.0) for aa in aa_l])
        t_inv_l = [t.astype(BF16) for t in t_inv_l]
        yield
        return dict(ar=ar_l, bke=bke_l, v2=v2_l, g_end=g_end_l, t_inv=t_inv_l, akv=akv_l, arbk=arbk_l,
                    bonus=bonus, gate=g)

    def advance(j, pre):
        base = j * RW_ROWS
        y_chunks = []
        for c in range(n_chunks):
            idx = [c * n_pairs + p for p in range(n_pairs)]
            states = [s_ref[p] for p in range(n_pairs)]
            arh = [_dot_nt(pre["ar"][q], s.astype(BF16)) for q, s in zip(idx, states)]
            yield
            u_l = [_dot(pre["t_inv"][q], (h[0:n2] + pre["akv"][q]).astype(BF16)) for q, h in zip(idx, arh)]
            yield
            uv = [jnp.concatenate([u.astype(BF16), pre["v2"][q]], axis=0) for q, u in zip(idx, u_l)]
            y2 = [h[n2:2 * n2] + _dot(pre["arbk"][q], x) for q, h, x in zip(idx, arh, uv)]
            for p in range(n_pairs):
                s_ref[p] = states[p] * pre["g_end"][c][:, pair_lanes[p]] + _dot_tn(uv[p], pre["bke"][idx[p]])
            yield
            y_chunks.append(jnp.concatenate([y[0:c_len] + y[c_len:n2] for y in y2], axis=1))
        y = jnp.concatenate(y_chunks, axis=0)

        mu_l = [_seg_sum(y[:, sl], ones_bd) * (1.0 / RW_HEAD) for sl in seg_lanes]
        yield
        d_l = [y[:, sl] - mu for sl, mu in zip(seg_lanes, mu_l)]
        var_l = [_seg_sum(d * d, ones_bd) * (1.0 / RW_HEAD) for d in d_l]
        yield
        yn = jnp.concatenate([d * lax.rsqrt(var + GN_EPS) for d, var in zip(d_l, var_l)], axis=1)
        yn = yn * lnw_ref[...] + lnb_ref[...]
        o_ref[base:base + RW_ROWS, :] = ((yn + pre["bonus"]) * pre["gate"]).astype(o_ref.dtype)

    def save(pre):
        for name in _RW_CARRY:
            value, ref = pre[name], carry[name]
            if isinstance(value, list):
                for q, item in enumerate(value):
                    ref[q] = item
            else:
                ref[...] = value

    def load():
        return {name: ([ref[q] for q in range(ref.shape[0])] if len(ref.shape) == 3 else ref[...])
                for name, ref in carry.items()}

    def row_before(src_ref, row):
        return lambda lo, hi: src_ref[row:row + 1, lo:hi]

    @pl.when(i == 0)
    def _():
        save(_run(prepare(u_ref, 0, lambda lo, hi: jnp.zeros((1, hi - lo), F32))))

    pre = load()
    for j in range(n_sub):
        last_row = (j + 1) * RW_ROWS - 1
        if j + 1 < n_sub:
            upcoming = prepare(u_ref, (j + 1) * RW_ROWS, row_before(u_ref, last_row))
        else:
            upcoming = prepare(next_ref, 0, row_before(u_ref, last_row))
        pre, _ = _run_interleaved(upcoming, advance(j, pre))
    save(pre)


def _rwkv(u_rw, shift_mix_p, w2a2, w0, a0, g2_p, k_k, k_a, r_k, ln_w, ln_b, ones_bd, tri):
    b, l_pad, _ = u_rw.shape
    rows = RW_BLOCK_ROWS
    sub_per_block = rows // RW_ROWS
    last_sub = l_pad // RW_ROWS - 1
    n_prob = (RW_ROWS // RW_CHUNK) * (RW_WIDTH // LANES)
    n2 = 2 * RW_CHUNK
    vec = lambda x: x.reshape(1, -1).astype(F32)
    const = lambda bi, i: (0, 0)
    vec_spec = pl.BlockSpec((1, RW_WIDTH), const)
    carry_shapes = dict(
        ar=pltpu.VMEM((n_prob, 2 * n2, LANES), BF16), bke=pltpu.VMEM((n_prob, 2 * n2, LANES), BF16),
        v2=pltpu.VMEM((n_prob, n2, LANES), BF16), t_inv=pltpu.VMEM((n_prob, n2, n2), BF16),
        akv=pltpu.VMEM((n_prob, n2, LANES), F32), arbk=pltpu.VMEM((n_prob, n2, 2 * n2), BF16),
        g_end=pltpu.VMEM((RW_ROWS // RW_CHUNK, 1, RW_WIDTH), F32),
        bonus=pltpu.VMEM((RW_ROWS, RW_WIDTH), F32), gate=pltpu.VMEM((RW_ROWS, RW_WIDTH), F32))
    return pl.pallas_call(
        _rwkv_kernel,
        grid=(b, l_pad // rows),
        in_specs=[pl.BlockSpec((None, rows, RW_IN_PAD), lambda bi, i: (bi, i, 0)),
                  pl.BlockSpec((None, RW_ROWS, RW_IN_PAD),
                               lambda bi, i: (bi, jnp.minimum((i + 1) * sub_per_block, last_sub), 0)),
                  pl.BlockSpec((1, RW_IN_PAD), const),
                  pl.BlockSpec(w2a2.shape, const),
                  vec_spec, vec_spec,
                  pl.BlockSpec(g2_p.shape, const),
                  vec_spec, vec_spec, vec_spec, vec_spec, vec_spec,
                  pl.BlockSpec((SEG_LANES, SEG_LANES), const),
                  pl.BlockSpec((RW_CHUNK, RW_CHUNK), const)],
        out_specs=pl.BlockSpec((None, rows, RW_WIDTH), lambda bi, i: (bi, i, 0)),
        out_shape=jax.ShapeDtypeStruct((b, l_pad, RW_WIDTH), BF16),
        scratch_shapes=[pltpu.VMEM((RW_WIDTH // LANES, LANES, LANES), F32)] + [carry_shapes[n] for n in _RW_CARRY],
        compiler_params=pltpu.CompilerParams(dimension_semantics=("arbitrary", "arbitrary"),
                                             vmem_limit_bytes=VMEM_LIMIT_BYTES),
        name="rwkv7",
    )(u_rw, u_rw, vec(shift_mix_p), w2a2, vec(w0), vec(a0), g2_p, vec(k_k), vec(k_a), vec(r_k),
      vec(ln_w), vec(ln_b), ones_bd, tri)


def _rope_lane_tables(length):
    pos = jnp.arange(length, dtype=F32)
    inv_freq = ROPE_THETA ** (-jnp.arange(0, ROPE_DIM, 2, dtype=F32) / ROPE_DIM)
    half = ROPE_DIM // 2
    dim = jnp.arange(LANES) % DA_QK_DIM
    ang = pos[:, None] * inv_freq[dim % half][None, :]
    cos, sin = jnp.cos(ang), jnp.sin(ang)
    first, second = dim < half, (dim >= half) & (dim < ROPE_DIM)
    c = jnp.where(first | second, cos, 1.0)
    s1 = jnp.where(first, -sin, 0.0)
    s2 = jnp.where(second, sin, 0.0)
    return c, s1, s2


def kernel(x, meta_tokens, ffn1_norm, ffn1_w_gate, ffn1_w_up, ffn1_w_down, mix_norm, w_in, da_q_norm, da_k_norm, da_lambda_q1, da_lambda_k1, da_lambda_q2, da_lambda_k2, da_subln, rw_shift_mix, rw_w0, rw_w2, rw_a0, rw_a2, rw_g2, rw_k_k, rw_k_a, rw_r_k, rw_ln_w, rw_ln_b, w_out, ffn2_norm, ffn2_w_gate, ffn2_w_up, ffn2_w_down):
    bsz, t, d = x.shape
    depth = w_in.shape[0]
    l = N_META + t
    l_pad = -(-l // Q_BLOCK) * Q_BLOCK
    assert d == D_MODEL and l_pad % ROW_TILE == 0 and l_pad % ATTN_TILE == 0 and l_pad % RW_BLOCK_ROWS == 0
    h = x.reshape(bsz * t, d)
    meta = meta_tokens.astype(x.dtype)

    rope_c, rope_s1, rope_s2 = _rope_lane_tables(l_pad)
    lane_head = jnp.arange(SEG_LANES) // RW_HEAD
    ones_bd =(lane_head[:, None] == lane_head[None, :]).astype(BF16)
    tri = (jnp.arange(RW_CHUNK)[:, None] >= jnp.arange(RW_CHUNK)[None, :]).astype(BF16)

    for layer in range(depth):
        lam_init = 0.8 - 0.6 * math.exp(-0.3 * layer)
        shift_mix_p = jnp.pad(rw_shift_mix[layer], (0, RW_IN_PAD - RW_IN))
        zeros_lora = jnp.zeros((D_DECAY_LORA, RW_WIDTH), F32)
        w2a2 = jnp.concatenate([jnp.concatenate([rw_w2[layer], zeros_lora], axis=1),
                                jnp.concatenate([zeros_lora, rw_a2[layer]], axis=1)], axis=0)
        g2_p = jnp.pad(rw_g2[layer], ((0, RW_LORA_PAD - D_GATE_LORA), (0, 0))).astype(BF16)
        q_gain = jnp.tile(da_q_norm[layer], DA_QK_WIDTH // DA_QK_DIM).reshape(1, DA_QK_WIDTH)
        k_gain = jnp.tile(da_k_norm[layer], DA_QK_WIDTH // DA_QK_DIM).reshape(1, DA_QK_WIDTH)
        lam_vecs = jnp.stack([da_lambda_q1[layer], da_lambda_k1[layer],
                              da_lambda_q2[layer], da_lambda_k2[layer]]).astype(F32)

        h = _ffn(h, ffn1_norm[layer], ffn1_w_gate, ffn1_w_up, ffn1_w_down, layer,
                 assemble=(meta, l_pad, t) if layer == 0 else None)
        q, k, v, u_rw = _mix_in(h, mix_norm[layer], w_in, layer, q_gain, k_gain,
                                rope_c, rope_s1, rope_s2, ones_bd, l_pad)
        shape3 = lambda a: a.reshape(bsz, l_pad, a.shape[-1])
        o_da = _attention(shape3(q), shape3(k), shape3(v), lam_vecs, da_subln[layer], lam_init)
        o_rw = _rwkv(shape3(u_rw), shift_mix_p, w2a2, rw_w0[layer], rw_a0[layer], g2_p,
                     rw_k_k[layer], rw_k_a[layer], rw_r_k[layer], rw_ln_w[layer], rw_ln_b[layer],
                     ones_bd, tri)
        last = layer == depth - 1
        h = _ffn(h, ffn2_norm[layer], ffn2_w_gate, ffn2_w_up, ffn2_w_down, layer,
                 mix=(o_da.reshape(bsz * l_pad, DA_WIDTH), o_rw.reshape(bsz * l_pad, RW_WIDTH), w_out),
                 compact=(l_pad, N_META, t) if last else None)
    return h.reshape(bsz, t, d)
```

```python
import functools
import math

import jax
import jax.numpy as jnp
from jax import lax
from jax.experimental import pallas as pl
from jax.experimental.pallas import tpu as pltpu

F32 = jnp.float32
BF16 = jnp.bfloat16

D_MODEL = 1024
N_META = 16
Q_BLOCK = 128
ROPE_THETA = 500000.0
NORM_EPS = 1e-6
DA_HEADS = 4
DA_QK_DIM = 64
DA_V_DIM = 2 * DA_QK_DIM
DA_WIDTH = DA_HEADS * DA_V_DIM
DA_QK_WIDTH = DA_HEADS * 2 * DA_QK_DIM
ROPE_DIM = DA_QK_DIM // 4
RW_HEAD = 64
RW_WIDTH = D_MODEL - DA_WIDTH
RW_HEADS = RW_WIDTH // RW_HEAD
D_DECAY_LORA = 64
D_AAA_LORA = 64
D_GATE_LORA = 160
GN_EPS = 64e-5
DA_IN = 2 * DA_QK_WIDTH + DA_WIDTH
RW_IN = 3 * RW_WIDTH + D_DECAY_LORA + D_AAA_LORA + D_GATE_LORA
D_FF = 2816

LANES = 128
SUBLANES = 8
SEG_LANES = 256
VMEM_LIMIT_BYTES = 56 * 1024 * 1024

ROW_TILE = 640
COMPACT_ROW_TILE = 512
FF_CHUNK = 256
WEIGHT_SLAB_ROWS = 256
ATTN_TILE = 640
ATTN_HEADS_PER_STEP = 2
ATTN_SCORE_LEAD = 2
RW_CHUNK = 64
RW_ROWS = 2 * RW_CHUNK
RW_BLOCK_ROWS = 5 * RW_ROWS
RW_LORA_PAD = 2 * LANES
RW_IN_PAD = 3 * RW_WIDTH + LANES + RW_LORA_PAD
N_IN_PAD = DA_IN + RW_IN_PAD

_NT = (((1,), (1,)), ((), ()))
_TN = (((0,), (0,)), ((), ()))


def _dot(a, b):
    return jnp.dot(a, b, preferred_element_type=F32)


def _dot_nt(a, b):
    return lax.dot_general(a, b, _NT, preferred_element_type=F32)


def _dot_tn(a, b):
    return lax.dot_general(a, b, _TN, preferred_element_type=F32)


def _split2(x):
    hi = x.astype(BF16)
    lo = (x - hi.astype(F32)).astype(BF16)
    return hi, lo


def _dot_split(a, b):
    a_hi, a_lo = _split2(a)
    b_hi, b_lo = _split2(b)
    return _dot(a_hi, b_hi) + (_dot(a_hi, b_lo) + _dot(a_lo, b_hi))


def _seg_sum(x, ones_bd):
    hi, lo = _split2(x)
    return _dot(hi, ones_bd) + _dot(lo, ones_bd)


def _rms_rows(h, gain):
    ms = jnp.mean(h * h, axis=-1, keepdims=True)
    return h * lax.rsqrt(ms + NORM_EPS) * gain


def _fetch_weights_bf16(jobs):
    slabs = []
    used = {}
    for hbm, vmem, stage, sem in jobs:
        rows, cols = hbm.shape
        assert rows % WEIGHT_SLAB_ROWS == 0 and stage.shape == (2, WEIGHT_SLAB_ROWS, cols), (hbm.shape, stage.shape)
        for r0 in range(0, rows, WEIGHT_SLAB_ROWS):
            slot = used.get(id(stage), 0) % 2
            used[id(stage)] = used.get(id(stage), 0) + 1
            slabs.append((hbm, vmem, stage, sem, r0, cols, slot))

    def copy(k):
        hbm, _, stage, sem, r0, _, slot = slabs[k]
        return pltpu.make_async_copy(hbm.at[r0:r0 + WEIGHT_SLAB_ROWS, :], stage.at[slot], sem.at[slot])

    copy(0).start()
    for k, (_, vmem, stage, _, r0, cols, slot) in enumerate(slabs):
        if k + 1 < len(slabs):
            copy(k + 1).start()
        copy(k).wait()
        vmem[r0:r0 + WEIGHT_SLAB_ROWS, 0:cols] = stage[slot].astype(BF16)
    for hbm, vmem, _, _ in jobs:
        if vmem.shape[1] > hbm.shape[1]:
            vmem[:, hbm.shape[1]:] = jnp.zeros((vmem.shape[0], vmem.shape[1] - hbm.shape[1]), BF16)


def _ffn_kernel(*refs, has_mix, assemble, layer):
    n_w = 4 if has_mix else 3
    n_in = len(refs) - (1 + 2 + n_w + 4)
    w_hbm = [w.at[layer] for w in refs[n_in - n_w:n_in]]
    o_ref, xn_ref, act_ref = refs[n_in:n_in + 3]
    w_vmem = refs[n_in + 3:n_in + 3 + n_w]
    stage_wide, stage_narrow, sem_wide, sem_narrow = refs[n_in + 3 + n_w:]
    g_ref = refs[n_in - n_w - 1]
    wg_ref, wu_ref, wd_ref = w_vmem[0:3]

    @pl.when(pl.program_id(0) == 0)
    def _():
        stages = [(stage_wide, sem_wide), (stage_wide, sem_wide), (stage_narrow, sem_narrow),
                  (stage_narrow, sem_narrow)]
        _fetch_weights_bf16([(h, v) + s for h, v, s in zip(w_hbm, w_vmem, stages)])

    if has_mix:
        h_ref, oda_ref, orw_ref = refs[0:3]
        wout_ref = w_vmem[3]
        h = (h_ref[...] + _dot(oda_ref[...], wout_ref[0:DA_WIDTH, :])
             + _dot(orw_ref[...], wout_ref[DA_WIDTH:DA_WIDTH + RW_WIDTH, :]))
        o_ref[...] = h
        res_ref = o_ref
    elif assemble is not None:
        x_ref, meta_ref = refs[0:2]
        tiles, n_prefix, last_shift, last_valid = assemble
        tm, d = x_ref.shape
        il = pl.program_id(0) % tiles
        x = x_ref[...]
        first, last = il == 0, il == tiles - 1
        h = jnp.where(first, pltpu.roll(x, n_prefix, 0), jnp.where(last, pltpu.roll(x, tm - last_shift, 0), x))
        row = lax.broadcasted_iota(jnp.int32, (tm, d), 0)
        prefix = jnp.concatenate([meta_ref[...], jnp.zeros((tm - n_prefix, d), F32)], axis=0)
        h = jnp.where(first & (row < n_prefix), prefix, h)
        h = jnp.where(last & (row >= last_valid), 0.0, h)
        o_ref[...] = h
        res_ref = o_ref
    else:
        h_ref = refs[0]
        h = h_ref[...]
        res_ref = h_ref
    xn_ref[...] = _rms_rows(h, g_ref[...]).astype(BF16)
    d_ff = wg_ref.shape[1]
    for c in range(d_ff // FF_CHUNK):
        cols = slice(c * FF_CHUNK, (c + 1) * FF_CHUNK)
        g = _dot(xn_ref[...], wg_ref[:, cols])
        u = _dot(xn_ref[...], wu_ref[:, cols])
        act_ref[:, cols] = ((g * jax.nn.sigmoid(g)) * u).astype(BF16)
    o_ref[...] = res_ref[...] + 0.5 * _dot(act_ref[...], wd_ref[...])


def _ffn(h, norm_g, wg, wu, wd, layer, mix=None, compact=None, assemble=None):
    m, d = h.shape
    d_ff = wg.shape[-1]
    row = lambda i: (i, 0)
    const2 = lambda i: (0, 0)
    kernel_assemble = None
    if assemble is not None:
        prefix, l_pad, n_rows = assemble
        n_prefix = prefix.shape[0]
        tm = ROW_TILE
        tiles = l_pad // tm
        n_seq = m // n_rows
        last_start = (tiles - 1) * tm - n_prefix
        assert tiles >= 2 and n_rows >= tm and n_prefix % SUBLANES == 0 and n_rows % SUBLANES == 0
        assert 0 <= last_start - (n_rows - tm) < tm
        kernel_assemble = (tiles, n_prefix, last_start - (n_rows - tm), n_prefix + n_rows - (tiles - 1) * tm)
        m_out = n_seq * l_pad
        window = lambda i: (pl.multiple_of(
            (i // tiles) * n_rows + jnp.clip((i % tiles) * tm - n_prefix, 0, n_rows - tm), SUBLANES), 0)
        row_spec = lambda width: pl.BlockSpec((pl.Element(tm), pl.Element(width)), window)
    elif compact is None:
        tm = ROW_TILE
        m_out = m
        row_spec = lambda width: pl.BlockSpec((tm, width), row)
    else:
        l_pad, first_row, n_rows = compact
        tm = COMPACT_ROW_TILE
        align = 2 * SUBLANES
        assert n_rows % tm == 0 and first_row % align == 0 and l_pad % align == 0 and tm % align == 0
        tiles = n_rows // tm
        m_out = (m // l_pad) * n_rows
        window = lambda i: (pl.multiple_of((i // tiles) * l_pad + first_row + (i % tiles) * tm, align), 0)
        row_spec = lambda width: pl.BlockSpec((pl.Element(tm), pl.Element(width)), window)
    in_specs = [row_spec(d)]
    args = [h]
    if assemble is not None:
        in_specs.append(pl.BlockSpec(prefix.shape, const2))
        args.append(prefix)
    weights = [wg, wu, wd]
    if mix is not None:
        o_da, o_rw, w_out = mix
        in_specs += [row_spec(DA_WIDTH), row_spec(RW_WIDTH)]
        args += [o_da, o_rw]
        weights.append(w_out)
    in_specs += [pl.BlockSpec((1, d), const2)] + [pl.BlockSpec(memory_space=pl.ANY)] * len(weights)
    args += [norm_g.reshape(1, d)] + weights
    return pl.pallas_call(
        functools.partial(_ffn_kernel, has_mix=mix is not None, assemble=kernel_assemble, layer=layer),
        grid=(m_out // tm,),
        in_specs=in_specs,
        out_specs=pl.BlockSpec((tm, d), row),
        out_shape=jax.ShapeDtypeStruct((m_out, d), F32),
        scratch_shapes=([pltpu.VMEM((tm, d), BF16), pltpu.VMEM((tm, d_ff), BF16)]
                        + [pltpu.VMEM(w.shape[-2:], BF16) for w in weights]
                        + [pltpu.VMEM((2, WEIGHT_SLAB_ROWS, d_ff), F32), pltpu.VMEM((2, WEIGHT_SLAB_ROWS, d), F32),
                           pltpu.SemaphoreType.DMA((2,)), pltpu.SemaphoreType.DMA((2,))]),
        compiler_params=pltpu.CompilerParams(dimension_semantics=("arbitrary",),
                                             vmem_limit_bytes=VMEM_LIMIT_BYTES),
        name="ffn_mix" if mix is not None else "ffn",
    )(*args)


def _qk_prep(u, gain_ref, c, s1, s2, ones_bd, scale, out_ref):
    rep = SEG_LANES // LANES
    c, s1, s2 = (jnp.concatenate([t] * rep, axis=1) for t in (c, s1, s2))
    for j in range(DA_QK_WIDTH // SEG_LANES):
        sl = slice(SEG_LANES * j, SEG_LANES * (j + 1))
        x = u[:, sl]
        ss = _seg_sum(x * x, ones_bd)
        xn = x * lax.rsqrt(ss * (1.0 / DA_QK_DIM) + NORM_EPS) * gain_ref[:, sl]
        half = ROPE_DIM // 2
        xr = xn * c + pltpu.roll(xn, SEG_LANES - half, 1) * s1 + pltpu.roll(xn, half, 1) * s2
        out_ref[:, sl] = (xr * scale).astype(BF16)


def _mixin_kernel(h_ref, g_ref, win_hbm, qg_ref, kg_ref, c_ref, s1_ref, s2_ref, ones_ref,
                  q_ref, k_ref, v_ref, urw_ref, win_ref, stage_ref, sem_ref, *, q_scale, layer):
    @pl.when(pl.program_id(0) == 0)
    def _():
        _fetch_weights_bf16([(win_hbm.at[layer], win_ref, stage_ref, sem_ref)])

    xn = _rms_rows(h_ref[...], g_ref[...]).astype(BF16)
    qw = DA_QK_WIDTH
    c, s1, s2, ones_bd = c_ref[...], s1_ref[...], s2_ref[...], ones_ref[...]
    uq = _dot(xn, win_ref[:, 0:qw])
    uk = _dot(xn, win_ref[:, qw:2 * qw])
    _qk_prep(uq, qg_ref, c, s1, s2, ones_bd, q_scale, q_ref)
    _qk_prep(uk, kg_ref, c, s1, s2, ones_bd, 1.0, k_ref)
    v_ref[...] = _dot(xn, win_ref[:, 2 * qw:DA_IN]).astype(BF16)
    urw_ref[...] = _dot(xn, win_ref[:, DA_IN:N_IN_PAD])


def _mix_in(h, norm_g, w_in, layer, q_gain, k_gain, rope_c, rope_s1, rope_s2, ones_bd, l_pad):
    m, d = h.shape
    tm = ROW_TILE
    tiles_per_seq = l_pad // tm
    row = lambda i: (i, 0)
    pos = lambda i: (i % tiles_per_seq, 0)
    const2 = lambda i: (0, 0)
    q_scale = DA_QK_DIM ** -0.5 * math.log2(math.e)
    return pl.pallas_call(
        functools.partial(_mixin_kernel, q_scale=q_scale, layer=layer),
        grid=(m // tm,),
        in_specs=[pl.BlockSpec((tm, d), row), pl.BlockSpec((1, d), const2),
                  pl.BlockSpec(memory_space=pl.ANY),
                  pl.BlockSpec((1, DA_QK_WIDTH), const2), pl.BlockSpec((1, DA_QK_WIDTH), const2),
                  pl.BlockSpec((tm, LANES), pos), pl.BlockSpec((tm, LANES), pos),
                  pl.BlockSpec((tm, LANES), pos), pl.BlockSpec((SEG_LANES, SEG_LANES), const2)],
        out_specs=[pl.BlockSpec((tm, DA_QK_WIDTH), row), pl.BlockSpec((tm, DA_QK_WIDTH), row),
                   pl.BlockSpec((tm, DA_WIDTH), row), pl.BlockSpec((tm, RW_IN_PAD), row)],
        out_shape=[jax.ShapeDtypeStruct((m, DA_QK_WIDTH), BF16), jax.ShapeDtypeStruct((m, DA_QK_WIDTH), BF16),
                   jax.ShapeDtypeStruct((m, DA_WIDTH), BF16), jax.ShapeDtypeStruct((m, RW_IN_PAD), F32)],
        scratch_shapes=[pltpu.VMEM((d, N_IN_PAD), BF16), pltpu.VMEM((2, WEIGHT_SLAB_ROWS, w_in.shape[-1]), F32),
                        pltpu.SemaphoreType.DMA((2,))],
        compiler_params=pltpu.CompilerParams(dimension_semantics=("arbitrary",),
                                             vmem_limit_bytes=VMEM_LIMIT_BYTES),
        name="mix_in",
    )(h, norm_g.reshape(1, d), w_in, q_gain, k_gain, rope_c, rope_s1, rope_s2, ones_bd)


def _attn_kernel(lam_ref, q_ref, k_ref, v_ref, sub_ref, o_ref, q2_s, vx_s, m_s, l_s, acc_s, s_s, *, tq, lam_init):
    qi = pl.program_id(2)
    n_heads = q_ref.shape[1] // LANES
    head_lanes = [slice(LANES * hh, LANES * (hh + 1)) for hh in range(n_heads)]
    vx_lanes = [slice(2 * LANES * hh, 2 * LANES * (hh + 1)) for hh in range(n_heads)]
    streams = [(hh, slice((2 * hh + c) * tq, (2 * hh + c + 1) * tq)) for hh in range(n_heads) for c in range(2)]

    @pl.when(qi == 0)
    def _():
        for hh in range(n_heads):
            vx_s[:, 2 * LANES * hh:2 * LANES * hh + LANES] = v_ref[:, head_lanes[hh]]
            vx_s[:, 2 * LANES * hh + LANES:2 * LANES * (hh + 1)] = jnp.ones((vx_s.shape[0], LANES), BF16)

    for hh in range(n_heads):
        q = q_ref[:, head_lanes[hh]]
        lane = lax.broadcasted_iota(jnp.int32, q.shape, 1)
        zero = jnp.zeros_like(q)
        q2_s[streams[2 * hh][1], :] = jnp.where(lane < DA_QK_DIM, q, zero)
        q2_s[streams[2 * hh + 1][1], :] = jnp.where(lane < DA_QK_DIM, zero, q)
    m_s[...] = jnp.full(m_s.shape, -1e30, F32)
    l_s[...] = jnp.zeros_like(l_s)
    acc_s[...] = jnp.zeros_like(acc_s)

    def process(start, width, diag_offset):
        def scores(st):
            hh, rows = streams[st]
            s_s[rows, 0:width] = _dot_nt(q2_s[rows, :], k_ref[pl.ds(start, width), head_lanes[hh]])

        lead = min(ATTN_SCORE_LEAD, len(streams))
        for st in range(lead):
            scores(st)
        for st, (hh, rows) in enumerate(streams):
            s = s_s[rows, 0:width]
            if diag_offset is not None:
                row = lax.broadcasted_iota(jnp.int32, s.shape, 0)
                col = lax.broadcasted_iota(jnp.int32, s.shape, 1)
                s = jnp.where(col <= row + diag_offset, s, -jnp.inf)
            m_prev = m_s[rows, :]
            m_new = jnp.maximum(m_prev, jnp.max(s, axis=1, keepdims=True))
            alpha = jnp.exp2(m_prev - m_new)
            p = jnp.exp2(s - jnp.concatenate([m_new] * (width // LANES), axis=1))
            pv = _dot(p.astype(BF16), vx_s[pl.ds(start, width), vx_lanes[hh]])
            acc_s[rows, :] = acc_s[rows, :] * alpha + pv[:, 0:LANES]
            l_s[rows, :] = l_s[rows, :] * alpha + pv[:, LANES:2 * LANES]
            m_s[rows, :] = m_new
            if st + lead < len(streams):
                scores(st + lead)

    def wide_body(t, carry):
        process(pl.multiple_of(t * (2 * tq), 2 * tq), 2 * tq, None)
        return carry

    n_wide = qi // 2

    def wide_pair(t, carry):
        wide_body(2 * t, carry)
        return wide_body(2 * t + 1, carry)

    lax.fori_loop(0, n_wide // 2, wide_pair, 0)

    @pl.when(n_wide % 2 == 1)
    def _():
        wide_body(n_wide - 1, 0)

    @pl.when(qi % 2 == 0)
    def _():
        process(pl.multiple_of(qi * tq, tq), tq, 0)

    @pl.when(qi % 2 == 1)
    def _():
        process(pl.multiple_of((qi - 1) * tq, tq), 2 * tq, tq)

    lam1 = jnp.exp(jnp.sum(lam_ref[0:1, :] * lam_ref[1:2, :], axis=1, keepdims=True))
    lam2 = jnp.exp(jnp.sum(lam_ref[2:3, :] * lam_ref[3:4, :], axis=1, keepdims=True))
    lam = lam1 - lam2 + lam_init
    for hh in range(n_heads):
        r0, r1 = streams[2 * hh][1], streams[2 * hh + 1][1]
        o = acc_s[r0, :] / l_s[r0, :] - lam * (acc_s[r1, :] / l_s[r1, :])
        o = _rms_rows(o, sub_ref[...]) * (1.0 - lam_init)
        o_ref[:, head_lanes[hh]] = o.astype(o_ref.dtype)


def _attention(q, k, v, lam_vecs, subln, lam_init):
    b, l_pad, _ = q.shape
    tq = ATTN_TILE
    nq = l_pad // tq
    nh = ATTN_HEADS_PER_STEP
    hw = nh * LANES
    n_streams = 2 * nh
    kv_spec = pl.BlockSpec((None, l_pad, hw), lambda bi, h, qi: (bi, 0, h), pipeline_mode=pl.Buffered(1))
    return pl.pallas_call(
        functools.partial(_attn_kernel, tq=tq, lam_init=lam_init),
        grid=(b, DA_HEADS // nh, nq),
        in_specs=[pl.BlockSpec(lam_vecs.shape, lambda bi, h, qi: (0, 0)),
                  pl.BlockSpec((None, tq, hw), lambda bi, h, qi: (bi, qi, h)),
                  kv_spec, kv_spec,
                  pl.BlockSpec((1, DA_V_DIM), lambda bi, h, qi: (0, 0))],
        out_specs=pl.BlockSpec((None, tq, hw), lambda bi, h, qi: (bi, qi, h)),
        out_shape=jax.ShapeDtypeStruct((b, l_pad, DA_WIDTH), BF16),
        scratch_shapes=[pltpu.VMEM((n_streams * tq, LANES), BF16), pltpu.VMEM((l_pad, 2 * hw), BF16),
                        pltpu.VMEM((n_streams * tq, LANES), F32), pltpu.VMEM((n_streams * tq, LANES), F32),
                        pltpu.VMEM((n_streams * tq, LANES), F32),
                        pltpu.VMEM((n_streams * tq, 2 * tq), F32)],
        compiler_params=pltpu.CompilerParams(dimension_semantics=("arbitrary", "arbitrary", "arbitrary"),
                                             vmem_limit_bytes=VMEM_LIMIT_BYTES),
        name="diff_attention",
    )(lam_vecs, q, k, v, subln.reshape(1, DA_V_DIM))


def _expand_heads(x, lo_mask):
    zero = jnp.zeros_like(x)
    return jnp.concatenate([jnp.where(lo_mask, x, zero), jnp.where(lo_mask, zero, x)], axis=0)


def _run(stages):
    try:
        while True:
            next(stages)
    except StopIteration as stop:
        return stop.value


def _run_interleaved(stages_a, stages_b):
    live = [stages_a, stages_b]
    results = [None, None]
    while live[0] is not None or live[1] is not None:
        for n in range(2):
            if live[n] is None:
                continue
            try:
                next(live[n])
            except StopIteration as stop:
                results[n] = stop.value
                live[n] = None
    return results


def _unit_lower_inverse(a_list):
    n = a_list[0].shape[0]
    eye = (lax.broadcasted_iota(jnp.int32, (n, n), 0) == lax.broadcasted_iota(jnp.int32, (n, n), 1)).astype(F32)
    mm = lambda x, y: _dot(x.astype(BF16), y.astype(BF16))
    t_list = [eye + a for a in a_list]
    p_list = [mm(a, a) for a in a_list]
    yield
    levels = int(math.log2(RW_CHUNK))
    for _ in range(levels - 2):
        pt_list = [mm(p, jnp.concatenate([p, t], axis=1)) for p, t in zip(p_list, t_list)]
        t_list = [t + pt[:, n:] for t, pt in zip(t_list, pt_list)]
        p_list = [pt[:, :n] for pt in pt_list]
        yield
    return [t + mm(p, t) for p, t in zip(p_list, t_list)]


_RW_CARRY = ("ar", "bke", "v2", "t_inv", "akv", "arbk", "g_end", "bonus", "gate")


def _rwkv_kernel(u_ref, next_ref, mix_ref, w2a2_ref, w0_ref, a0_ref, g2_ref, kk_ref, ka_ref, rk_ref,
                 lnw_ref, lnb_ref, ones_ref, tri_ref, o_ref, s_ref, *carry_refs):
    i = pl.program_id(1)
    carry = dict(zip(_RW_CARRY, carry_refs))
    c_len = RW_CHUNK
    w = RW_WIDTH
    n2 = 2 * c_len
    n_sub = u_ref.shape[0] // RW_ROWS
    n_chunks = RW_ROWS // c_len
    n_pairs = w // LANES
    pair_lanes = [slice(LANES * p, LANES * (p + 1)) for p in range(n_pairs)]
    seg_lanes = [slice(SEG_LANES * p, SEG_LANES * (p + 1)) for p in range(w // SEG_LANES)]

    @pl.when(i == 0)
    def _():
        s_ref[...] = jnp.zeros_like(s_ref)

    ones_bd = ones_ref[...]
    tri = tri_ref[...]
    r_i = lax.broadcasted_iota(jnp.int32, (n2, n2), 0)
    c_i = lax.broadcasted_iota(jnp.int32, (n2, n2), 1)
    t_row = jnp.where(r_i >= c_len, r_i - c_len, r_i)
    t_col = jnp.where(c_i >= c_len, c_i - c_len, c_i)
    strict = t_col < t_row
    incl = t_col <= t_row
    lo_mask = lax.broadcasted_iota(jnp.int32, (c_len, LANES), 1) < RW_HEAD

    def prepare(src_ref, base, prev_row):
        def shifted(lo, hi):
            u = src_ref[base:base + RW_ROWS, lo:hi]
            last_prev = prev_row(lo, hi)
            rolled = pltpu.roll(u, 1, 0)
            row = lax.broadcasted_iota(jnp.int32, u.shape, 0)
            prev = jnp.where(row == 0, last_prev, rolled)
            return u + (prev - u) * mix_ref[:, lo:hi]

        r = shifted(0, w)
        k = shifted(w, 2 * w)
        v = shifted(2 * w, 3 * w)
        lora_in = shifted(3 * w, 3 * w + LANES)
        g_in = shifted(3 * w + LANES, RW_IN_PAD)

        lane = lax.broadcasted_iota(jnp.int32, lora_in.shape, 1)
        lora_act = jnp.where(lane < D_DECAY_LORA, jnp.tanh(lora_in), lora_in)
        wa = _dot_split(lora_act, w2a2_ref[...])
        g = _dot(jax.nn.sigmoid(g_in).astype(BF16), g2_ref[...])
        yield
        w_pre = w0_ref[...] + wa[:, 0:w]
        lw = (-math.exp(-0.5)) * jax.nn.sigmoid(w_pre)
        a = jax.nn.sigmoid(a0_ref[...] + wa[:, w:2 * w])

        cums = []
        for c in range(n_chunks):
            lw_c = lw[c * c_len:(c + 1) * c_len]
            hi = lw_c.astype(BF16)
            mid_f = lw_c - hi.astype(F32)
            mid = mid_f.astype(BF16)
            lo = (mid_f - mid.astype(F32)).astype(BF16)
            cums.append(_dot(tri, hi) + (_dot(tri, mid) + _dot(tri, lo)))

        kk_raw = k * kk_ref[...]
        k_mod = k * (1.0 + (a - 1.0) * ka_ref[...])
        rkr = r * k_mod * rk_ref[...]
        ss_l = [_seg_sum(kk_raw[:, sl] * kk_raw[:, sl], ones_bd) for sl in seg_lanes]
        rk_l = [_seg_sum(rkr[:, sl], ones_bd) for sl in seg_lanes]
        yield
        kk = jnp.concatenate([kk_raw[:, sl] * lax.rsqrt(jnp.maximum(ss, 1e-24))
                              for sl, ss in zip(seg_lanes, ss_l)], axis=1)
        bonus = jnp.concatenate([s * v[:, sl] for sl, s in zip(seg_lanes, rk_l)], axis=1)
        a_neg = -kk
        b_vec = kk * a

        ar_l, bk_l, bke_l, v2_l, g_end_l = [], [], [], [], []
        for c in range(n_chunks):
            rs = slice(c * c_len, (c + 1) * c_len)
            cum = cums[c]
            cum_end = cum[c_len - 1:c_len, :]
            e_in = jnp.exp(cum)
            e_out = jnp.exp(-cum)
            e_end = jnp.exp(cum_end - cum)
            r_t = r[rs] * e_in
            a_t = a_neg[rs] * jnp.exp(cum - lw[rs])
            k_t = k_mod[rs] * e_out
            b_t = b_vec[rs] * e_out
            k_e = k_mod[rs] * e_end
            b_e = b_vec[rs] * e_end
            g_end_l.append(jnp.exp(cum_end))
            v_c = v[rs]
            for sl in pair_lanes:
                ar_l.append(jnp.concatenate([_expand_heads(a_t[:, sl], lo_mask), _expand_heads(r_t[:, sl], lo_mask)],
                                            axis=0).astype(BF16))
                bk_l.append(jnp.concatenate([_expand_heads(b_t[:, sl], lo_mask), _expand_heads(k_t[:, sl], lo_mask)],
                                            axis=0).astype(BF16))
                bke_l.append(jnp.concatenate([_expand_heads(b_e[:, sl], lo_mask), _expand_heads(k_e[:, sl], lo_mask)],
                                             axis=0).astype(BF16))
                v2_l.append(_expand_heads(v_c[:, sl], lo_mask).astype(BF16))
        aa_l = [_dot_nt(ar, bk) for ar, bk in zip(ar_l, bk_l)]
        yield
        akv_l = [_dot(jnp.where(strict, aa[0:n2, n2:2 * n2], 0.0).astype(BF16), v2) for aa, v2 in zip(aa_l, v2_l)]
        arbk_l = [jnp.concatenate([jnp.where(incl, aa[n2:2 * n2, 0:n2], 0.0),
                                   jnp.where(incl, aa[n2:2 * n2, n2:2 * n2], 0.0)], axis=1).astype(BF16)
                  for aa in aa_l]
        t_inv_l = yield from _unit_lower_inverse([jnp.where(strict, aa[0:n2, 0:n2], 0.0) for aa in aa_l])
        t_inv_l = [t.astype(BF16) for t in t_inv_l]
        yield
        return dict(ar=ar_l, bke=bke_l, v2=v2_l, g_end=g_end_l, t_inv=t_inv_l, akv=akv_l, arbk=arbk_l,
                    bonus=bonus, gate=g)

    def advance(j, pre):
        base = j * RW_ROWS
        y_chunks = []
        for c in range(n_chunks):
            idx = [c * n_pairs + p for p in range(n_pairs)]
            states = [s_ref[p] for p in range(n_pairs)]
            arh = [_dot_nt(pre["ar"][q], s.astype(BF16)) for q, s in zip(idx, states)]
            yield
            u_l = [_dot(pre["t_inv"][q], (h[0:n2] + pre["akv"][q]).astype(BF16)) for q, h in zip(idx, arh)]
            yield
            uv = [jnp.concatenate([u.astype(BF16), pre["v2"][q]], axis=0) for q, u in zip(idx, u_l)]
            y2 = [h[n2:2 * n2] + _dot(pre["arbk"][q], x) for q, h, x in zip(idx, arh, uv)]
            for p in range(n_pairs):
                s_ref[p] = states[p] * pre["g_end"][c][:, pair_lanes[p]] + _dot_tn(uv[p], pre["bke"][idx[p]])
            yield
            y_chunks.append(jnp.concatenate([y[0:c_len] + y[c_len:n2] for y in y2], axis=1))
        y = jnp.concatenate(y_chunks, axis=0)

        mu_l = [_seg_sum(y[:, sl], ones_bd) * (1.0 / RW_HEAD) for sl in seg_lanes]
        yield
        d_l = [y[:, sl] - mu for sl, mu in zip(seg_lanes, mu_l)]
        var_l = [_seg_sum(d * d, ones_bd) * (1.0 / RW_HEAD) for d in d_l]
        yield
        yn = jnp.concatenate([d * lax.rsqrt(var + GN_EPS) for d, var in zip(d_l, var_l)], axis=1)
        yn = yn * lnw_ref[...] + lnb_ref[...]
        o_ref[base:base + RW_ROWS, :] = ((yn + pre["bonus"]) * pre["gate"]).astype(o_ref.dtype)

    def save(pre):
        for name in _RW_CARRY:
            value, ref = pre[name], carry[name]
            if isinstance(value, list):
                for q, item in enumerate(value):
                    ref[q] = item
            else:
                ref[...] = value

    def load():
        return {name: ([ref[q] for q in range(ref.shape[0])] if len(ref.shape) == 3 else ref[...])
                for name, ref in carry.items()}

    def row_before(src_ref, row):
        return lambda lo, hi: src_ref[row:row + 1, lo:hi]

    @pl.when(i == 0)
    def _():
        save(_run(prepare(u_ref, 0, lambda lo, hi: jnp.zeros((1, hi - lo), F32))))

    pre = load()
    for j in range(n_sub):
        last_row = (j + 1) * RW_ROWS - 1
        if j + 1 < n_sub:
            upcoming = prepare(u_ref, (j + 1) * RW_ROWS, row_before(u_ref, last_row))
        else:
            upcoming = prepare(next_ref, 0, row_before(u_ref, last_row))
        pre, _ = _run_interleaved(upcoming, advance(j, pre))
    save(pre)


def _rwkv(u_rw, shift_mix_p, w2a2, w0, a0, g2_p, k_k, k_a, r_k, ln_w, ln_b, ones_bd, tri):
    b, l_pad, _ = u_rw.shape
    rows = RW_BLOCK_ROWS
    sub_per_block = rows // RW_ROWS
    last_sub = l_pad // RW_ROWS - 1
    n_prob = (RW_ROWS // RW_CHUNK) * (RW_WIDTH // LANES)
    n2 = 2 * RW_CHUNK
    vec = lambda x: x.reshape(1, -1).astype(F32)
    const = lambda bi, i: (0, 0)
    vec_spec = pl.BlockSpec((1, RW_WIDTH), const)
    carry_shapes = dict(
        ar=pltpu.VMEM((n_prob, 2 * n2, LANES), BF16), bke=pltpu.VMEM((n_prob, 2 * n2, LANES), BF16),
        v2=pltpu.VMEM((n_prob, n2, LANES), BF16), t_inv=pltpu.VMEM((n_prob, n2, n2), BF16),
        akv=pltpu.VMEM((n_prob, n2, LANES), F32), arbk=pltpu.VMEM((n_prob, n2, 2 * n2), BF16),
        g_end=pltpu.VMEM((RW_ROWS // RW_CHUNK, 1, RW_WIDTH), F32),
        bonus=pltpu.VMEM((RW_ROWS, RW_WIDTH), F32), gate=pltpu.VMEM((RW_ROWS, RW_WIDTH), F32))
    return pl.pallas_call(
        _rwkv_kernel,
        grid=(b, l_pad // rows),
        in_specs=[pl.BlockSpec((None, rows, RW_IN_PAD), lambda bi, i: (bi, i, 0)),
                  pl.BlockSpec((None, RW_ROWS, RW_IN_PAD),
                               lambda bi, i: (bi, jnp.minimum((i + 1) * sub_per_block, last_sub), 0)),
                  pl.BlockSpec((1, RW_IN_PAD), const),
                  pl.BlockSpec(w2a2.shape, const),
                  vec_spec, vec_spec,
                  pl.BlockSpec(g2_p.shape, const),
                  vec_spec, vec_spec, vec_spec, vec_spec, vec_spec,
                  pl.BlockSpec((SEG_LANES, SEG_LANES), const),
                  pl.BlockSpec((RW_CHUNK, RW_CHUNK), const)],
        out_specs=pl.BlockSpec((None, rows, RW_WIDTH), lambda bi, i: (bi, i, 0)),
        out_shape=jax.ShapeDtypeStruct((b, l_pad, RW_WIDTH), BF16),
        scratch_shapes=[pltpu.VMEM((RW_WIDTH // LANES, LANES, LANES), F32)] + [carry_shapes[n] for n in _RW_CARRY],
        compiler_params=pltpu.CompilerParams(dimension_semantics=("arbitrary", "arbitrary"),
                                             vmem_limit_bytes=VMEM_LIMIT_BYTES),
        name="rwkv7",
    )(u_rw, u_rw, vec(shift_mix_p), w2a2, vec(w0), vec(a0), g2_p, vec(k_k), vec(k_a), vec(r_k),
      vec(ln_w), vec(ln_b), ones_bd, tri)


def _rope_lane_tables(length):
    pos = jnp.arange(length, dtype=F32)
    inv_freq = ROPE_THETA ** (-jnp.arange(0, ROPE_DIM, 2, dtype=F32) / ROPE_DIM)
    ang = pos[:, None] * inv_freq[None, :]
    cos, sin = jnp.cos(ang), jnp.sin(ang)
    half = ROPE_DIM // 2
    rest = DA_QK_DIM - ROPE_DIM
    ones = jnp.ones((length, rest), F32)
    zeros = lambda n: jnp.zeros((length, n), F32)
    c = jnp.concatenate([cos, cos, ones], axis=1)
    s1 = jnp.concatenate([-sin, zeros(half + rest)], axis=1)
    s2 = jnp.concatenate([zeros(half), sin, zeros(rest)], axis=1)
    rep = LANES // DA_QK_DIM
    return jnp.tile(c, (1, rep)), jnp.tile(s1, (1, rep)), jnp.tile(s2, (1, rep))


def kernel(x, meta_tokens, ffn1_norm, ffn1_w_gate, ffn1_w_up, ffn1_w_down, mix_norm, w_in, da_q_norm, da_k_norm, da_lambda_q1, da_lambda_k1, da_lambda_q2, da_lambda_k2, da_subln, rw_shift_mix, rw_w0, rw_w2, rw_a0, rw_a2, rw_g2, rw_k_k, rw_k_a, rw_r_k, rw_ln_w, rw_ln_b, w_out, ffn2_norm, ffn2_w_gate, ffn2_w_up, ffn2_w_down):
    bsz, t, d = x.shape
    depth = w_in.shape[0]
    l = N_META + t
    l_pad = -(-l // Q_BLOCK) * Q_BLOCK
    assert d == D_MODEL and l_pad % ROW_TILE == 0 and l_pad % ATTN_TILE == 0 and l_pad % RW_BLOCK_ROWS == 0
    h = x.reshape(bsz * t, d)
    meta = meta_tokens.astype(x.dtype)

    rope_c, rope_s1, rope_s2 = _rope_lane_tables(l_pad)
    lane_head = jnp.arange(SEG_LANES) // RW_HEAD
    ones_bd =(lane_head[:, None] == lane_head[None, :]).astype(BF16)
    tri = (jnp.arange(RW_CHUNK)[:, None] >= jnp.arange(RW_CHUNK)[None, :]).astype(BF16)

    for layer in range(depth):
        lam_init = 0.8 - 0.6 * math.exp(-0.3 * layer)
        shift_mix_p = jnp.pad(rw_shift_mix[layer], (0, RW_IN_PAD - RW_IN))
        zeros_lora = jnp.zeros((D_DECAY_LORA, RW_WIDTH), F32)
        w2a2 = jnp.concatenate([jnp.concatenate([rw_w2[layer], zeros_lora], axis=1),
                                jnp.concatenate([zeros_lora, rw_a2[layer]], axis=1)], axis=0)
        g2_p = jnp.pad(rw_g2[layer], ((0, RW_LORA_PAD - D_GATE_LORA), (0, 0))).astype(BF16)
        q_gain = jnp.tile(da_q_norm[layer], DA_QK_WIDTH // DA_QK_DIM).reshape(1, DA_QK_WIDTH)
        k_gain = jnp.tile(da_k_norm[layer], DA_QK_WIDTH // DA_QK_DIM).reshape(1, DA_QK_WIDTH)
        lam_vecs = jnp.stack([da_lambda_q1[layer], da_lambda_k1[layer],
                              da_lambda_q2[layer], da_lambda_k2[layer]]).astype(F32)

        h = _ffn(h, ffn1_norm[layer], ffn1_w_gate, ffn1_w_up, ffn1_w_down, layer,
                 assemble=(meta, l_pad, t) if layer == 0 else None)
        q, k, v, u_rw = _mix_in(h, mix_norm[layer], w_in, layer, q_gain, k_gain,
                                rope_c, rope_s1, rope_s2, ones_bd, l_pad)
        shape3 = lambda a: a.reshape(bsz, l_pad, a.shape[-1])
        o_da = _attention(shape3(q), shape3(k), shape3(v), lam_vecs, da_subln[layer], lam_init)
        o_rw = _rwkv(shape3(u_rw), shift_mix_p, w2a2, rw_w0[layer], rw_a0[layer], g2_p,
                     rw_k_k[layer], rw_k_a[layer], rw_r_k[layer], rw_ln_w[layer], rw_ln_b[layer],
                     ones_bd, tri)
        last = layer == depth - 1
        h = _ffn(h, ffn2_norm[layer], ffn2_w_gate, ffn2_w_up, ffn2_w_down, layer,
                 mix=(o_da.reshape(bsz * l_pad, DA_WIDTH), o_rw.reshape(bsz * l_pad, RW_WIDTH), w_out),
                 compact=(l_pad, N_META, t) if last else None)
    return h.reshape(bsz, t, d)
```

```python
import functools
import math

import jax
import jax.numpy as jnp
from jax import lax
from jax.experimental import pallas as pl
from jax.experimental.pallas import tpu as pltpu

F32 = jnp.float32
BF16 = jnp.bfloat16

D_MODEL = 1024
N_META = 16
Q_BLOCK = 128
ROPE_THETA = 500000.0
NORM_EPS = 1e-6
DA_HEADS = 4
DA_QK_DIM = 64
DA_V_DIM = 2 * DA_QK_DIM
DA_WIDTH = DA_HEADS * DA_V_DIM
DA_QK_WIDTH = DA_HEADS * 2 * DA_QK_DIM
ROPE_DIM = DA_QK_DIM // 4
RW_HEAD = 64
RW_WIDTH = D_MODEL - DA_WIDTH
RW_HEADS = RW_WIDTH // RW_HEAD
D_DECAY_LORA = 64
D_AAA_LORA = 64
D_GATE_LORA = 160
GN_EPS = 64e-5
DA_IN = 2 * DA_QK_WIDTH + DA_WIDTH
RW_IN = 3 * RW_WIDTH + D_DECAY_LORA + D_AAA_LORA + D_GATE_LORA
D_FF = 2816

LANES = 128
SUBLANES = 8
SEG_LANES = 256
VMEM_LIMIT_BYTES = 56 * 1024 * 1024

ROW_TILE = 640
COMPACT_ROW_TILE = 512
FF_CHUNK = 256
WEIGHT_SLAB_ROWS = 256
ATTN_TILE = 640
ATTN_HEADS_PER_STEP = 2
ATTN_SCORE_LEAD = 2
RW_CHUNK = 64
RW_ROWS = 2 * RW_CHUNK
RW_BLOCK_ROWS = 5 * RW_ROWS
RW_LORA_PAD = 2 * LANES
RW_IN_PAD = 3 * RW_WIDTH + LANES + RW_LORA_PAD
N_IN_PAD = DA_IN + RW_IN_PAD

_NT = (((1,), (1,)), ((), ()))
_TN = (((0,), (0,)), ((), ()))


def _dot(a, b):
    return jnp.dot(a, b, preferred_element_type=F32)


def _dot_nt(a, b):
    return lax.dot_general(a, b, _NT, preferred_element_type=F32)


def _dot_tn(a, b):
    return lax.dot_general(a, b, _TN, preferred_element_type=F32)


def _split2(x):
    hi = x.astype(BF16)
    lo = (x - hi.astype(F32)).astype(BF16)
    return hi, lo


def _seg_sum(x, ones_bd):
    return _dot(x.astype(BF16), ones_bd)


def _rms_rows(h, gain):
    ms = jnp.mean(h * h, axis=-1, keepdims=True)
    return h * lax.rsqrt(ms + NORM_EPS) * gain


def _fetch_weights_bf16(jobs):
    slabs = []
    used = {}
    for hbm, vmem, stage, sem in jobs:
        rows, cols = hbm.shape
        assert rows % WEIGHT_SLAB_ROWS == 0 and stage.shape == (2, WEIGHT_SLAB_ROWS, cols), (hbm.shape, stage.shape)
        for r0 in range(0, rows, WEIGHT_SLAB_ROWS):
            slot = used.get(id(stage), 0) % 2
            used[id(stage)] = used.get(id(stage), 0) + 1
            slabs.append((hbm, vmem, stage, sem, r0, cols, slot))

    def copy(k):
        hbm, _, stage, sem, r0, _, slot = slabs[k]
        return pltpu.make_async_copy(hbm.at[r0:r0 + WEIGHT_SLAB_ROWS, :], stage.at[slot], sem.at[slot])

    copy(0).start()
    for k, (_, vmem, stage, _, r0, cols, slot) in enumerate(slabs):
        if k + 1 < len(slabs):
            copy(k + 1).start()
        copy(k).wait()
        vmem[r0:r0 + WEIGHT_SLAB_ROWS, 0:cols] = stage[slot].astype(BF16)
    for hbm, vmem, _, _ in jobs:
        if vmem.shape[1] > hbm.shape[1]:
            vmem[:, hbm.shape[1]:] = jnp.zeros((vmem.shape[0], vmem.shape[1] - hbm.shape[1]), BF16)


def _ffn_kernel(*refs, has_mix, assemble, layer):
    n_w = 4 if has_mix else 3
    n_in = len(refs) - (1 + 2 + n_w + 4)
    w_hbm = [w.at[layer] for w in refs[n_in - n_w:n_in]]
    o_ref, xn_ref, act_ref = refs[n_in:n_in + 3]
    w_vmem = refs[n_in + 3:n_in + 3 + n_w]
    stage_wide, stage_narrow, sem_wide, sem_narrow = refs[n_in + 3 + n_w:]
    g_ref = refs[n_in - n_w - 1]
    wg_ref, wu_ref, wd_ref = w_vmem[0:3]

    @pl.when(pl.program_id(0) == 0)
    def _():
        stages = [(stage_wide, sem_wide), (stage_wide, sem_wide), (stage_narrow, sem_narrow),
                  (stage_narrow, sem_narrow)]
        _fetch_weights_bf16([(h, v) + s for h, v, s in zip(w_hbm, w_vmem, stages)])

    if has_mix:
        h_ref, oda_ref, orw_ref = refs[0:3]
        wout_ref = w_vmem[3]
        h = (h_ref[...] + _dot(oda_ref[...], wout_ref[0:DA_WIDTH, :])
             + _dot(orw_ref[...], wout_ref[DA_WIDTH:DA_WIDTH + RW_WIDTH, :]))
        o_ref[...] = h
        res_ref = o_ref
    elif assemble is not None:
        x_ref, meta_ref = refs[0:2]
        tiles, n_prefix, last_shift, last_valid = assemble
        tm, d = x_ref.shape
        il = pl.program_id(0) % tiles
        x = x_ref[...]
        first, last = il == 0, il == tiles - 1
        h = jnp.where(first, pltpu.roll(x, n_prefix, 0), jnp.where(last, pltpu.roll(x, tm - last_shift, 0), x))
        row = lax.broadcasted_iota(jnp.int32, (tm, d), 0)
        prefix = jnp.concatenate([meta_ref[...], jnp.zeros((tm - n_prefix, d), F32)], axis=0)
        h = jnp.where(first & (row < n_prefix), prefix, h)
        h = jnp.where(last & (row >= last_valid), 0.0, h)
        o_ref[...] = h
        res_ref = o_ref
    else:
        h_ref = refs[0]
        h = h_ref[...]
        res_ref = h_ref
    xn_ref[...] = _rms_rows(h, g_ref[...]).astype(BF16)
    d_ff = wg_ref.shape[1]
    for c in range(d_ff // FF_CHUNK):
        cols = slice(c * FF_CHUNK, (c + 1) * FF_CHUNK)
        g = _dot(xn_ref[...], wg_ref[:, cols])
        u = _dot(xn_ref[...], wu_ref[:, cols])
        act_ref[:, cols] = ((g * jax.nn.sigmoid(g)) * u).astype(BF16)
    o_ref[...] = res_ref[...] + 0.5 * _dot(act_ref[...], wd_ref[...])


def _ffn(h, norm_g, wg, wu, wd, layer, mix=None, compact=None, assemble=None):
    m, d = h.shape
    d_ff = wg.shape[-1]
    row = lambda i: (i, 0)
    const2 = lambda i: (0, 0)
    kernel_assemble = None
    if assemble is not None:
        prefix, l_pad, n_rows = assemble
        n_prefix = prefix.shape[0]
        tm = ROW_TILE
        tiles = l_pad // tm
        n_seq = m // n_rows
        last_start = (tiles - 1) * tm - n_prefix
        assert tiles >= 2 and n_rows >= tm and n_prefix % SUBLANES == 0 and n_rows % SUBLANES == 0
        assert 0 <= last_start - (n_rows - tm) < tm
        kernel_assemble = (tiles, n_prefix, last_start - (n_rows - tm), n_prefix + n_rows - (tiles - 1) * tm)
        m_out = n_seq * l_pad
        window = lambda i: (pl.multiple_of(
            (i // tiles) * n_rows + jnp.clip((i % tiles) * tm - n_prefix, 0, n_rows - tm), SUBLANES), 0)
        row_spec = lambda width: pl.BlockSpec((pl.Element(tm), pl.Element(width)), window)
    elif compact is None:
        tm = ROW_TILE
        m_out = m
        row_spec = lambda width: pl.BlockSpec((tm, width), row)
    else:
        l_pad, first_row, n_rows = compact
        tm = COMPACT_ROW_TILE
        align = 2 * SUBLANES
        assert n_rows % tm == 0 and first_row % align == 0 and l_pad % align == 0 and tm % align == 0
        tiles = n_rows // tm
        m_out = (m // l_pad) * n_rows
        window = lambda i: (pl.multiple_of((i // tiles) * l_pad + first_row + (i % tiles) * tm, align), 0)
        row_spec = lambda width: pl.BlockSpec((pl.Element(tm), pl.Element(width)), window)
    in_specs = [row_spec(d)]
    args = [h]
    if assemble is not None:
        in_specs.append(pl.BlockSpec(prefix.shape, const2))
        args.append(prefix)
    weights = [wg, wu, wd]
    if mix is not None:
        o_da, o_rw, w_out = mix
        in_specs += [row_spec(DA_WIDTH), row_spec(RW_WIDTH)]
        args += [o_da, o_rw]
        weights.append(w_out)
    in_specs += [pl.BlockSpec((1, d), const2)] + [pl.BlockSpec(memory_space=pl.ANY)] * len(weights)
    args += [norm_g.reshape(1, d)] + weights
    return pl.pallas_call(
        functools.partial(_ffn_kernel, has_mix=mix is not None, assemble=kernel_assemble, layer=layer),
        grid=(m_out // tm,),
        in_specs=in_specs,
        out_specs=pl.BlockSpec((tm, d), row),
        out_shape=jax.ShapeDtypeStruct((m_out, d), F32),
        scratch_shapes=([pltpu.VMEM((tm, d), BF16), pltpu.VMEM((tm, d_ff), BF16)]
                        + [pltpu.VMEM(w.shape[-2:], BF16) for w in weights]
                        + [pltpu.VMEM((2, WEIGHT_SLAB_ROWS, d_ff), F32), pltpu.VMEM((2, WEIGHT_SLAB_ROWS, d), F32),
                           pltpu.SemaphoreType.DMA((2,)), pltpu.SemaphoreType.DMA((2,))]),
        compiler_params=pltpu.CompilerParams(dimension_semantics=("arbitrary",),
                                             vmem_limit_bytes=VMEM_LIMIT_BYTES),
        name="ffn_mix" if mix is not None else "ffn",
    )(*args)


def _qk_prep(u, gain_ref, c, s1, s2, ones_bd, scale, out_ref):
    rep = SEG_LANES // LANES
    c, s1, s2 = (jnp.concatenate([t] * rep, axis=1) for t in (c, s1, s2))
    for j in range(DA_QK_WIDTH // SEG_LANES):
        sl = slice(SEG_LANES * j, SEG_LANES * (j + 1))
        x = u[:, sl]
        ss = _seg_sum(x * x, ones_bd)
        xn = x * lax.rsqrt(ss * (1.0 / DA_QK_DIM) + NORM_EPS) * gain_ref[:, sl]
        half = ROPE_DIM // 2
        xr = xn * c + pltpu.roll(xn, SEG_LANES - half, 1) * s1 + pltpu.roll(xn, half, 1) * s2
        out_ref[:, sl] = (xr * scale).astype(BF16)


def _mixin_kernel(h_ref, g_ref, win_hbm, qg_ref, kg_ref, c_ref, s1_ref, s2_ref, ones_ref,
                  q_ref, k_ref, v_ref, urw_ref, win_ref, stage_ref, sem_ref, *, q_scale, layer):
    @pl.when(pl.program_id(0) == 0)
    def _():
        _fetch_weights_bf16([(win_hbm.at[layer], win_ref, stage_ref, sem_ref)])

    xn = _rms_rows(h_ref[...], g_ref[...]).astype(BF16)
    qw = DA_QK_WIDTH
    c, s1, s2, ones_bd = c_ref[...], s1_ref[...], s2_ref[...], ones_ref[...]
    uq = _dot(xn, win_ref[:, 0:qw])
    uk = _dot(xn, win_ref[:, qw:2 * qw])
    _qk_prep(uq, qg_ref, c, s1, s2, ones_bd, q_scale, q_ref)
    _qk_prep(uk, kg_ref, c, s1, s2, ones_bd, 1.0, k_ref)
    v_ref[...] = _dot(xn, win_ref[:, 2 * qw:DA_IN]).astype(BF16)
    urw_ref[...] = _dot(xn, win_ref[:, DA_IN:N_IN_PAD])


def _mix_in(h, norm_g, w_in, layer, q_gain, k_gain, rope_c, rope_s1, rope_s2, ones_bd, l_pad):
    m, d = h.shape
    tm = ROW_TILE
    tiles_per_seq = l_pad // tm
    row = lambda i: (i, 0)
    pos = lambda i: (i % tiles_per_seq, 0)
    const2 = lambda i: (0, 0)
    q_scale = DA_QK_DIM ** -0.5 * math.log2(math.e)
    return pl.pallas_call(
        functools.partial(_mixin_kernel, q_scale=q_scale, layer=layer),
        grid=(m // tm,),
        in_specs=[pl.BlockSpec((tm, d), row), pl.BlockSpec((1, d), const2),
                  pl.BlockSpec(memory_space=pl.ANY),
                  pl.BlockSpec((1, DA_QK_WIDTH), const2), pl.BlockSpec((1, DA_QK_WIDTH), const2),
                  pl.BlockSpec((tm, LANES), pos), pl.BlockSpec((tm, LANES), pos),
                  pl.BlockSpec((tm, LANES), pos), pl.BlockSpec((SEG_LANES, SEG_LANES), const2)],
        out_specs=[pl.BlockSpec((tm, DA_QK_WIDTH), row), pl.BlockSpec((tm, DA_QK_WIDTH), row),
                   pl.BlockSpec((tm, DA_WIDTH), row), pl.BlockSpec((tm, RW_IN_PAD), row)],
        out_shape=[jax.ShapeDtypeStruct((m, DA_QK_WIDTH), BF16), jax.ShapeDtypeStruct((m, DA_QK_WIDTH), BF16),
                   jax.ShapeDtypeStruct((m, DA_WIDTH), BF16), jax.ShapeDtypeStruct((m, RW_IN_PAD), F32)],
        scratch_shapes=[pltpu.VMEM((d, N_IN_PAD), BF16), pltpu.VMEM((2, WEIGHT_SLAB_ROWS, w_in.shape[-1]), F32),
                        pltpu.SemaphoreType.DMA((2,))],
        compiler_params=pltpu.CompilerParams(dimension_semantics=("arbitrary",),
                                             vmem_limit_bytes=VMEM_LIMIT_BYTES),
        name="mix_in",
    )(h, norm_g.reshape(1, d), w_in, q_gain, k_gain, rope_c, rope_s1, rope_s2, ones_bd)


def _attn_kernel(lam_ref, q_ref, k_ref, v_ref, sub_ref, o_ref, q2_s, vx_s, m_s, l_s, acc_s, s_s, *, tq, lam_init):
    qi = pl.program_id(2)
    n_heads = q_ref.shape[1] // LANES
    head_lanes = [slice(LANES * hh, LANES * (hh + 1)) for hh in range(n_heads)]
    vx_lanes = [slice(2 * LANES * hh, 2 * LANES * (hh + 1)) for hh in range(n_heads)]
    streams = [(hh, slice((2 * hh + c) * tq, (2 * hh + c + 1) * tq)) for hh in range(n_heads) for c in range(2)]

    @pl.when(qi == 0)
    def _():
        for hh in range(n_heads):
            vx_s[:, 2 * LANES * hh:2 * LANES * hh + LANES] = v_ref[:, head_lanes[hh]]
            vx_s[:, 2 * LANES * hh + LANES:2 * LANES * (hh + 1)] = jnp.ones((vx_s.shape[0], LANES), BF16)

    for hh in range(n_heads):
        q = q_ref[:, head_lanes[hh]]
        lane = lax.broadcasted_iota(jnp.int32, q.shape, 1)
        zero = jnp.zeros_like(q)
        q2_s[streams[2 * hh][1], :] = jnp.where(lane < DA_QK_DIM, q, zero)
        q2_s[streams[2 * hh + 1][1], :] = jnp.where(lane < DA_QK_DIM, zero, q)
    m_s[...] = jnp.full(m_s.shape, -1e30, F32)
    l_s[...] = jnp.zeros_like(l_s)
    acc_s[...] = jnp.zeros_like(acc_s)

    def process(start, width, diag_offset):
        def scores(st):
            hh, rows = streams[st]
            s_s[rows, 0:width] = _dot_nt(q2_s[rows, :], k_ref[pl.ds(start, width), head_lanes[hh]])

        lead = min(ATTN_SCORE_LEAD, len(streams))
        for st in range(lead):
            scores(st)
        for st, (hh, rows) in enumerate(streams):
            s = s_s[rows, 0:width]
            if diag_offset is not None:
                row = lax.broadcasted_iota(jnp.int32, s.shape, 0)
                col = lax.broadcasted_iota(jnp.int32, s.shape, 1)
                s = jnp.where(col <= row + diag_offset, s, -jnp.inf)
            m_prev = m_s[rows, :]
            m_new = jnp.maximum(m_prev, jnp.max(s, axis=1, keepdims=True))
            alpha = jnp.exp2(m_prev - m_new)
            p = jnp.exp2(s - jnp.concatenate([m_new] * (width // LANES), axis=1))
            pv = _dot(p.astype(BF16), vx_s[pl.ds(start, width), vx_lanes[hh]])
            acc_s[rows, :] = acc_s[rows, :] * alpha + pv[:, 0:LANES]
            l_s[rows, :] = l_s[rows, :] * alpha + pv[:, LANES:2 * LANES]
            m_s[rows, :] = m_new
            if st + lead < len(streams):
                scores(st + lead)

    def wide_body(t, carry):
        process(pl.multiple_of(t * (2 * tq), 2 * tq), 2 * tq, None)
        return carry

    n_wide = qi // 2

    def wide_pair(t, carry):
        wide_body(2 * t, carry)
        return wide_body(2 * t + 1, carry)

    lax.fori_loop(0, n_wide // 2, wide_pair, 0)

    @pl.when(n_wide % 2 == 1)
    def _():
        wide_body(n_wide - 1, 0)

    @pl.when(qi % 2 == 0)
    def _():
        process(pl.multiple_of(qi * tq, tq), tq, 0)

    @pl.when(qi % 2 == 1)
    def _():
        process(pl.multiple_of((qi - 1) * tq, tq), 2 * tq, tq)

    lam1 = jnp.exp(jnp.sum(lam_ref[0:1, :] * lam_ref[1:2, :], axis=1, keepdims=True))
    lam2 = jnp.exp(jnp.sum(lam_ref[2:3, :] * lam_ref[3:4, :], axis=1, keepdims=True))
    lam = lam1 - lam2 + lam_init
    for hh in range(n_heads):
        r0, r1 = streams[2 * hh][1], streams[2 * hh + 1][1]
        o = acc_s[r0, :] / l_s[r0, :] - lam * (acc_s[r1, :] / l_s[r1, :])
        o = _rms_rows(o, sub_ref[...]) * (1.0 - lam_init)
        o_ref[:, head_lanes[hh]] = o.astype(o_ref.dtype)


def _attention(q, k, v, lam_vecs, subln, lam_init):
    b, l_pad, _ = q.shape
    tq = ATTN_TILE
    nq = l_pad // tq
    nh = ATTN_HEADS_PER_STEP
    hw = nh * LANES
    n_streams = 2 * nh
    kv_spec = pl.BlockSpec((None, l_pad, hw), lambda bi, h, qi: (bi, 0, h), pipeline_mode=pl.Buffered(1))
    return pl.pallas_call(
        functools.partial(_attn_kernel, tq=tq, lam_init=lam_init),
        grid=(b, DA_HEADS // nh, nq),
        in_specs=[pl.BlockSpec(lam_vecs.shape, lambda bi, h, qi: (0, 0)),
                  pl.BlockSpec((None, tq, hw), lambda bi, h, qi: (bi, qi, h)),
                  kv_spec, kv_spec,
                  pl.BlockSpec((1, DA_V_DIM), lambda bi, h, qi: (0, 0))],
        out_specs=pl.BlockSpec((None, tq, hw), lambda bi, h, qi: (bi, qi, h)),
        out_shape=jax.ShapeDtypeStruct((b, l_pad, DA_WIDTH), BF16),
        scratch_shapes=[pltpu.VMEM((n_streams * tq, LANES), BF16), pltpu.VMEM((l_pad, 2 * hw), BF16),
                        pltpu.VMEM((n_streams * tq, LANES), F32), pltpu.VMEM((n_streams * tq, LANES), F32),
                        pltpu.VMEM((n_streams * tq, LANES), F32),
                        pltpu.VMEM((n_streams * tq, 2 * tq), F32)],
        compiler_params=pltpu.CompilerParams(dimension_semantics=("arbitrary", "arbitrary", "arbitrary"),
                                             vmem_limit_bytes=VMEM_LIMIT_BYTES),
        name="diff_attention",
    )(lam_vecs, q, k, v, subln.reshape(1, DA_V_DIM))


def _expand_heads(x, lo_mask):
    zero = jnp.zeros_like(x)
    return jnp.concatenate([jnp.where(lo_mask, x, zero), jnp.where(lo_mask, zero, x)], axis=0)


def _run(stages):
    try:
        while True:
            next(stages)
    except StopIteration as stop:
        return stop.value


def _run_interleaved(stages_a, stages_b):
    live = [stages_a, stages_b]
    results = [None, None]
    while live[0] is not None or live[1] is not None:
        for n in range(2):
            if live[n] is None:
                continue
            try:
                next(live[n])
            except StopIteration as stop:
                results[n] = stop.value
                live[n] = None
    return results


def _unit_lower_inverse(a_list):
    n = a_list[0].shape[0]
    eye = (lax.broadcasted_iota(jnp.int32, (n, n), 0) == lax.broadcasted_iota(jnp.int32, (n, n), 1)).astype(F32)
    mm = lambda x, y: _dot(x.astype(BF16), y.astype(BF16))
    t_list = [eye + a for a in a_list]
    p_list = [mm(a, a) for a in a_list]
    yield
    levels = int(math.log2(RW_CHUNK))
    for _ in range(levels - 2):
        pt_list = [mm(p, jnp.concatenate([p, t], axis=1)) for p, t in zip(p_list, t_list)]
        t_list = [t + pt[:, n:] for t, pt in zip(t_list, pt_list)]
        p_list = [pt[:, :n] for pt in pt_list]
        yield
    return [t + mm(p, t) for p, t in zip(p_list, t_list)]


_RW_CARRY = ("ar", "bke", "v2", "t_inv", "akv", "arbk", "g_end", "bonus", "gate")


def _rwkv_kernel(u_ref, next_ref, mix_ref, w2a2_ref, w0_ref, a0_ref, g2_ref, kk_ref, ka_ref, rk_ref,
                 lnw_ref, lnb_ref, ones_ref, tri_ref, o_ref, s_ref, *carry_refs):
    i = pl.program_id(1)
    carry = dict(zip(_RW_CARRY, carry_refs))
    c_len = RW_CHUNK
    w = RW_WIDTH
    n2 = 2 * c_len
    n_sub = u_ref.shape[0] // RW_ROWS
    n_chunks = RW_ROWS // c_len
    n_pairs = w // LANES
    pair_lanes = [slice(LANES * p, LANES * (p + 1)) for p in range(n_pairs)]
    seg_lanes = [slice(SEG_LANES * p, SEG_LANES * (p + 1)) for p in range(w // SEG_LANES)]

    @pl.when(i == 0)
    def _():
        s_ref[...] = jnp.zeros_like(s_ref)

    ones_bd = ones_ref[...]
    tri = tri_ref[...]
    r_i = lax.broadcasted_iota(jnp.int32, (n2, n2), 0)
    c_i = lax.broadcasted_iota(jnp.int32, (n2, n2), 1)
    t_row = jnp.where(r_i >= c_len, r_i - c_len, r_i)
    t_col = jnp.where(c_i >= c_len, c_i - c_len, c_i)
    strict = t_col < t_row
    incl = t_col <= t_row
    lo_mask = lax.broadcasted_iota(jnp.int32, (c_len, LANES), 1) < RW_HEAD

    def prepare(src_ref, base, prev_row):
        def shifted(lo, hi):
            u = src_ref[base:base + RW_ROWS, lo:hi]
            last_prev = prev_row(lo, hi)
            rolled = pltpu.roll(u, 1, 0)
            row = lax.broadcasted_iota(jnp.int32, u.shape, 0)
            prev = jnp.where(row == 0, last_prev, rolled)
            return u + (prev - u) * mix_ref[:, lo:hi]

        r = shifted(0, w)
        k = shifted(w, 2 * w)
        v = shifted(2 * w, 3 * w)
        lora_in = shifted(3 * w, 3 * w + LANES)
        g_in = shifted(3 * w + LANES, RW_IN_PAD)

        lane = lax.broadcasted_iota(jnp.int32, lora_in.shape, 1)
        lora_act = jnp.where(lane < D_DECAY_LORA, jnp.tanh(lora_in), lora_in)
        wa = _dot(lora_act.astype(BF16), w2a2_ref[...])
        g = _dot(jax.nn.sigmoid(g_in).astype(BF16), g2_ref[...])
        yield
        w_pre = w0_ref[...] + wa[:, 0:w]
        lw = (-math.exp(-0.5)) * jax.nn.sigmoid(w_pre)
        a = jax.nn.sigmoid(a0_ref[...] + wa[:, w:2 * w])

        cums = []
        for c in range(n_chunks):
            lw_c = lw[c * c_len:(c + 1) * c_len]
            hi, lo = _split2(lw_c)
            cums.append(_dot(tri, hi) + _dot(tri, lo))

        kk_raw = k * kk_ref[...]
        k_mod = k * (1.0 + (a - 1.0) * ka_ref[...])
        rkr = r * k_mod * rk_ref[...]
        ss_l = [_seg_sum(kk_raw[:, sl] * kk_raw[:, sl], ones_bd) for sl in seg_lanes]
        rk_l = [_seg_sum(rkr[:, sl], ones_bd) for sl in seg_lanes]
        yield
        kk = jnp.concatenate([kk_raw[:, sl] * lax.rsqrt(jnp.maximum(ss, 1e-24))
                              for sl, ss in zip(seg_lanes, ss_l)], axis=1)
        bonus = jnp.concatenate([s * v[:, sl] for sl, s in zip(seg_lanes, rk_l)], axis=1)
        a_neg = -kk
        b_vec = kk * a

        ar_l, bk_l, bke_l, v2_l, g_end_l = [], [], [], [], []
        for c in range(n_chunks):
            rs = slice(c * c_len, (c + 1) * c_len)
            cum = cums[c]
            cum_end = cum[c_len - 1:c_len, :]
            e_in = jnp.exp(cum)
            e_out = jnp.exp(-cum)
            e_end = jnp.exp(cum_end - cum)
            r_t = r[rs] * e_in
            a_t = a_neg[rs] * jnp.exp(cum - lw[rs])
            k_t = k_mod[rs] * e_out
            b_t = b_vec[rs] * e_out
            k_e = k_mod[rs] * e_end
            b_e = b_vec[rs] * e_end
            g_end_l.append(jnp.exp(cum_end))
            v_c = v[rs]
            for sl in pair_lanes:
                ar_l.append(jnp.concatenate([_expand_heads(a_t[:, sl], lo_mask), _expand_heads(r_t[:, sl], lo_mask)],
                                            axis=0).astype(BF16))
                bk_l.append(jnp.concatenate([_expand_heads(b_t[:, sl], lo_mask), _expand_heads(k_t[:, sl], lo_mask)],
                                            axis=0).astype(BF16))
                bke_l.append(jnp.concatenate([_expand_heads(b_e[:, sl], lo_mask), _expand_heads(k_e[:, sl], lo_mask)],
                                             axis=0).astype(BF16))
                v2_l.append(_expand_heads(v_c[:, sl], lo_mask).astype(BF16))
        aa_l = [_dot_nt(ar, bk) for ar, bk in zip(ar_l, bk_l)]
        yield
        akv_l = [_dot(jnp.where(strict, aa[0:n2, n2:2 * n2], 0.0).astype(BF16), v2) for aa, v2 in zip(aa_l, v2_l)]
        arbk_l = [jnp.concatenate([jnp.where(incl, aa[n2:2 * n2, 0:n2], 0.0),
                                   jnp.where(incl, aa[n2:2 * n2, n2:2 * n2], 0.0)], axis=1).astype(BF16)
                  for aa in aa_l]
        t_inv_l = yield from _unit_lower_inverse([jnp.where(strict, aa[0:n2, 0:n2], 0.0) for aa in aa_l])
        t_inv_l = [t.astype(BF16) for t in t_inv_l]
        yield
        return dict(ar=ar_l, bke=bke_l, v2=v2_l, g_end=g_end_l, t_inv=t_inv_l, akv=akv_l, arbk=arbk_l,
                    bonus=bonus, gate=g)

    def advance(j, pre):
        base = j * RW_ROWS
        y_chunks = []
        for c in range(n_chunks):
            idx = [c * n_pairs + p for p in range(n_pairs)]
            states = [s_ref[p] for p in range(n_pairs)]
            arh = [_dot_nt(pre["ar"][q], s.astype(BF16)) for q, s in zip(idx, states)]
            yield
            u_l = [_dot(pre["t_inv"][q], (h[0:n2] + pre["akv"][q]).astype(BF16)) for q, h in zip(idx, arh)]
            yield
            uv = [jnp.concatenate([u.astype(BF16), pre["v2"][q]], axis=0) for q, u in zip(idx, u_l)]
            y2 = [h[n2:2 * n2] + _dot(pre["arbk"][q], x) for q, h, x in zip(idx, arh, uv)]
            for p in range(n_pairs):
                s_ref[p] = states[p] * pre["g_end"][c][:, pair_lanes[p]] + _dot_tn(uv[p], pre["bke"][idx[p]])
            yield
            y_chunks.append(jnp.concatenate([y[0:c_len] + y[c_len:n2] for y in y2], axis=1))
        y = jnp.concatenate(y_chunks, axis=0)

        mu_l = [_seg_sum(y[:, sl], ones_bd) * (1.0 / RW_HEAD) for sl in seg_lanes]
        yield
        d_l = [y[:, sl] - mu for sl, mu in zip(seg_lanes, mu_l)]
        var_l = [_seg_sum(d * d, ones_bd) * (1.0 / RW_HEAD) for d in d_l]
        yield
        yn = jnp.concatenate([d * lax.rsqrt(var + GN_EPS) for d, var in zip(d_l, var_l)], axis=1)
        yn = yn * lnw_ref[...] + lnb_ref[...]
        o_ref[base:base + RW_ROWS, :] = ((yn + pre["bonus"]) * pre["gate"]).astype(o_ref.dtype)

    def save(pre):
        for name in _RW_CARRY:
            value, ref = pre[name], carry[name]
            if isinstance(value, list):
                for q, item in enumerate(value):
                    ref[q] = item
            else:
                ref[...] = value

    def load():
        return {name: ([ref[q] for q in range(ref.shape[0])] if len(ref.shape) == 3 else ref[...])
                for name, ref in carry.items()}

    def row_before(src_ref, row):
        return lambda lo, hi: src_ref[row:row + 1, lo:hi]

    @pl.when(i == 0)
    def _():
        save(_run(prepare(u_ref, 0, lambda lo, hi: jnp.zeros((1, hi - lo), F32))))

    pre = load()
    for j in range(n_sub):
        last_row = (j + 1) * RW_ROWS - 1
        if j + 1 < n_sub:
            upcoming = prepare(u_ref, (j + 1) * RW_ROWS, row_before(u_ref, last_row))
        else:
            upcoming = prepare(next_ref, 0, row_before(u_ref, last_row))
        pre, _ = _run_interleaved(upcoming, advance(j, pre))
    save(pre)


def _rwkv(u_rw, shift_mix_p, w2a2, w0, a0, g2_p, k_k, k_a, r_k, ln_w, ln_b, ones_bd, tri):
    b, l_pad, _ = u_rw.shape
    rows = RW_BLOCK_ROWS
    sub_per_block = rows // RW_ROWS
    last_sub = l_pad // RW_ROWS - 1
    n_prob = (RW_ROWS // RW_CHUNK) * (RW_WIDTH // LANES)
    n2 = 2 * RW_CHUNK
    vec = lambda x: x.reshape(1, -1).astype(F32)
    const = lambda bi, i: (0, 0)
    vec_spec = pl.BlockSpec((1, RW_WIDTH), const)
    carry_shapes = dict(
        ar=pltpu.VMEM((n_prob, 2 * n2, LANES), BF16), bke=pltpu.VMEM((n_prob, 2 * n2, LANES), BF16),
        v2=pltpu.VMEM((n_prob, n2, LANES), BF16), t_inv=pltpu.VMEM((n_prob, n2, n2), BF16),
        akv=pltpu.VMEM((n_prob, n2, LANES), F32), arbk=pltpu.VMEM((n_prob, n2, 2 * n2), BF16),
        g_end=pltpu.VMEM((RW_ROWS // RW_CHUNK, 1, RW_WIDTH), F32),
        bonus=pltpu.VMEM((RW_ROWS, RW_WIDTH), F32), gate=pltpu.VMEM((RW_ROWS, RW_WIDTH), F32))
    return pl.pallas_call(
        _rwkv_kernel,
        grid=(b, l_pad // rows),
        in_specs=[pl.BlockSpec((None, rows, RW_IN_PAD), lambda bi, i: (bi, i, 0)),
                  pl.BlockSpec((None, RW_ROWS, RW_IN_PAD),
                               lambda bi, i: (bi, jnp.minimum((i + 1) * sub_per_block, last_sub), 0)),
                  pl.BlockSpec((1, RW_IN_PAD), const),
                  pl.BlockSpec(w2a2.shape, const),
                  vec_spec, vec_spec,
                  pl.BlockSpec(g2_p.shape, const),
                  vec_spec, vec_spec, vec_spec, vec_spec, vec_spec,
                  pl.BlockSpec((SEG_LANES, SEG_LANES), const),
                  pl.BlockSpec((RW_CHUNK, RW_CHUNK), const)],
        out_specs=pl.BlockSpec((None, rows, RW_WIDTH), lambda bi, i: (bi, i, 0)),
        out_shape=jax.ShapeDtypeStruct((b, l_pad, RW_WIDTH), BF16),
        scratch_shapes=[pltpu.VMEM((RW_WIDTH // LANES, LANES, LANES), F32)] + [carry_shapes[n] for n in _RW_CARRY],
        compiler_params=pltpu.CompilerParams(dimension_semantics=("arbitrary", "arbitrary"),
                                             vmem_limit_bytes=VMEM_LIMIT_BYTES),
        name="rwkv7",
    )(u_rw, u_rw, vec(shift_mix_p), w2a2, vec(w0), vec(a0), g2_p, vec(k_k), vec(k_a), vec(r_k),
      vec(ln_w), vec(ln_b), ones_bd, tri)


def _rope_lane_tables(length):
    pos = jnp.arange(length, dtype=F32)
    inv_freq = ROPE_THETA ** (-jnp.arange(0, ROPE_DIM, 2, dtype=F32) / ROPE_DIM)
    ang = pos[:, None] * inv_freq[None, :]
    cos, sin = jnp.cos(ang), jnp.sin(ang)
    half = ROPE_DIM // 2
    rest = DA_QK_DIM - ROPE_DIM
    ones = jnp.ones((length, rest), F32)
    zeros = lambda n: jnp.zeros((length, n), F32)
    c = jnp.concatenate([cos, cos, ones], axis=1)
    s1 = jnp.concatenate([-sin, zeros(half + rest)], axis=1)
    s2 = jnp.concatenate([zeros(half), sin, zeros(rest)], axis=1)
    rep = LANES // DA_QK_DIM
    return jnp.tile(c, (1, rep)), jnp.tile(s1, (1, rep)), jnp.tile(s2, (1, rep))


def kernel(x, meta_tokens, ffn1_norm, ffn1_w_gate, ffn1_w_up, ffn1_w_down, mix_norm, w_in, da_q_norm, da_k_norm, da_lambda_q1, da_lambda_k1, da_lambda_q2, da_lambda_k2, da_subln, rw_shift_mix, rw_w0, rw_w2, rw_a0, rw_a2, rw_g2, rw_k_k, rw_k_a, rw_r_k, rw_ln_w, rw_ln_b, w_out, ffn2_norm, ffn2_w_gate, ffn2_w_up, ffn2_w_down):
    bsz, t, d = x.shape
    depth = w_in.shape[0]
    l = N_META + t
    l_pad = -(-l // Q_BLOCK) * Q_BLOCK
    assert d == D_MODEL and l_pad % ROW_TILE == 0 and l_pad % ATTN_TILE == 0 and l_pad % RW_BLOCK_ROWS == 0
    h = x.reshape(bsz * t, d)
    meta = meta_tokens.astype(x.dtype)

    rope_c, rope_s1, rope_s2 = _rope_lane_tables(l_pad)
    lane_head = jnp.arange(SEG_LANES) // RW_HEAD
    ones_bd =(lane_head[:, None] == lane_head[None, :]).astype(BF16)
    tri = (jnp.arange(RW_CHUNK)[:, None] >= jnp.arange(RW_CHUNK)[None, :]).astype(BF16)

    for layer in range(depth):
        lam_init = 0.8 - 0.6 * math.exp(-0.3 * layer)
        shift_mix_p = jnp.pad(rw_shift_mix[layer], (0, RW_IN_PAD - RW_IN))
        zeros_lora = jnp.zeros((D_DECAY_LORA, RW_WIDTH), F32)
        w2a2 = jnp.concatenate([jnp.concatenate([rw_w2[layer], zeros_lora], axis=1),
                                jnp.concatenate([zeros_lora, rw_a2[layer]], axis=1)], axis=0).astype(BF16)
        g2_p = jnp.pad(rw_g2[layer], ((0, RW_LORA_PAD - D_GATE_LORA), (0, 0))).astype(BF16)
        q_gain = jnp.tile(da_q_norm[layer], DA_QK_WIDTH // DA_QK_DIM).reshape(1, DA_QK_WIDTH)
        k_gain = jnp.tile(da_k_norm[layer], DA_QK_WIDTH // DA_QK_DIM).reshape(1, DA_QK_WIDTH)
        lam_vecs = jnp.stack([da_lambda_q1[layer], da_lambda_k1[layer],
                              da_lambda_q2[layer], da_lambda_k2[layer]]).astype(F32)

        h = _ffn(h, ffn1_norm[layer], ffn1_w_gate, ffn1_w_up, ffn1_w_down, layer,
                 assemble=(meta, l_pad, t) if layer == 0 else None)
        q, k, v, u_rw = _mix_in(h, mix_norm[layer], w_in, layer, q_gain, k_gain,
                                rope_c, rope_s1, rope_s2, ones_bd, l_pad)
        shape3 = lambda a: a.reshape(bsz, l_pad, a.shape[-1])
        o_da = _attention(shape3(q), shape3(k), shape3(v), lam_vecs, da_subln[layer], lam_init)
        o_rw = _rwkv(shape3(u_rw), shift_mix_p, w2a2, rw_w0[layer], rw_a0[layer], g2_p,
                     rw_k_k[layer], rw_k_a[layer], rw_r_k[layer], rw_ln_w[layer], rw_ln_b[layer],
                     ones_bd, tri)
        last = layer == depth - 1
        h = _ffn(h, ffn2_norm[layer], ffn2_w_gate, ffn2_w_up, ffn2_w_down, layer,
                 mix=(o_da.reshape(bsz * l_pad, DA_WIDTH), o_rw.reshape(bsz * l_pad, RW_WIDTH), w_out),
                 compact=(l_pad, N_META, t) if last else None)
    return h.reshape(bsz, t, d)
```

```python
import functools
import math

import jax
import jax.numpy as jnp
from jax import lax
from jax.experimental import pallas as pl
from jax.experimental.pallas import tpu as pltpu

F32 = jnp.float32
BF16 = jnp.bfloat16

D_MODEL = 1024
N_META = 16
Q_BLOCK = 128
ROPE_THETA = 500000.0
NORM_EPS = 1e-6
DA_HEADS = 4
DA_QK_DIM = 64
DA_V_DIM = 2 * DA_QK_DIM
DA_WIDTH = DA_HEADS * DA_V_DIM
DA_QK_WIDTH = DA_HEADS * 2 * DA_QK_DIM
ROPE_DIM = DA_QK_DIM // 4
RW_HEAD = 64
RW_WIDTH = D_MODEL - DA_WIDTH
RW_HEADS = RW_WIDTH // RW_HEAD
D_DECAY_LORA = 64
D_AAA_LORA = 64
D_GATE_LORA = 160
GN_EPS = 64e-5
DA_IN = 2 * DA_QK_WIDTH + DA_WIDTH
RW_IN = 3 * RW_WIDTH + D_DECAY_LORA + D_AAA_LORA + D_GATE_LORA
D_FF = 2816

LANES = 128
SUBLANES = 8
SEG_LANES = 256
VMEM_LIMIT_BYTES = 56 * 1024 * 1024

ROW_TILE = 640
COMPACT_ROW_TILE = 512
FF_CHUNK = 256
WEIGHT_SLAB_ROWS = 256
ATTN_TILE = 640
ATTN_HEADS_PER_STEP = 2
ATTN_SCORE_LEAD = 2
RW_CHUNK = 64
RW_ROWS = 2 * RW_CHUNK
RW_BLOCK_ROWS = 5 * RW_ROWS
RW_LORA_PAD = 2 * LANES
RW_IN_PAD = 3 * RW_WIDTH + LANES + RW_LORA_PAD
N_IN_PAD = DA_IN + RW_IN_PAD

_NT = (((1,), (1,)), ((), ()))
_TN = (((0,), (0,)), ((), ()))


def _dot(a, b):
    return jnp.dot(a, b, preferred_element_type=F32)


def _dot_nt(a, b):
    return lax.dot_general(a, b, _NT, preferred_element_type=F32)


def _dot_tn(a, b):
    return lax.dot_general(a, b, _TN, preferred_element_type=F32)


def _split2(x):
    hi = x.astype(BF16)
    lo = (x - hi.astype(F32)).astype(BF16)
    return hi, lo


def _seg_sum(x, ones_bd):
    return _dot(x.astype(BF16), ones_bd)


def _rms_rows(h, gain):
    ms = jnp.mean(h * h, axis=-1, keepdims=True)
    return h * lax.rsqrt(ms + NORM_EPS) * gain


def _fetch_weights_bf16(jobs):
    slabs = []
    used = {}
    for hbm, vmem, stage, sem in jobs:
        rows, cols = hbm.shape
        assert rows % WEIGHT_SLAB_ROWS == 0 and stage.shape == (2, WEIGHT_SLAB_ROWS, cols), (hbm.shape, stage.shape)
        for r0 in range(0, rows, WEIGHT_SLAB_ROWS):
            slot = used.get(id(stage), 0) % 2
            used[id(stage)] = used.get(id(stage), 0) + 1
            slabs.append((hbm, vmem, stage, sem, r0, cols, slot))

    def copy(k):
        hbm, _, stage, sem, r0, _, slot = slabs[k]
        return pltpu.make_async_copy(hbm.at[r0:r0 + WEIGHT_SLAB_ROWS, :], stage.at[slot], sem.at[slot])

    copy(0).start()
    for k, (_, vmem, stage, _, r0, cols, slot) in enumerate(slabs):
        if k + 1 < len(slabs):
            copy(k + 1).start()
        copy(k).wait()
        vmem[r0:r0 + WEIGHT_SLAB_ROWS, 0:cols] = stage[slot].astype(BF16)
    for hbm, vmem, _, _ in jobs:
        if vmem.shape[1] > hbm.shape[1]:
            vmem[:, hbm.shape[1]:] = jnp.zeros((vmem.shape[0], vmem.shape[1] - hbm.shape[1]), BF16)


def _ffn_kernel(*refs, has_mix, assemble, layer):
    n_w = 4 if has_mix else 3
    n_in = len(refs) - (1 + 2 + n_w + 4)
    w_hbm = [w.at[layer] for w in refs[n_in - n_w:n_in]]
    o_ref, xn_ref, act_ref = refs[n_in:n_in + 3]
    w_vmem = refs[n_in + 3:n_in + 3 + n_w]
    stage_wide, stage_narrow, sem_wide, sem_narrow = refs[n_in + 3 + n_w:]
    g_ref = refs[n_in - n_w - 1]
    wg_ref, wu_ref, wd_ref = w_vmem[0:3]

    @pl.when(pl.program_id(0) == 0)
    def _():
        stages = [(stage_wide, sem_wide), (stage_wide, sem_wide), (stage_narrow, sem_narrow),
                  (stage_narrow, sem_narrow)]
        _fetch_weights_bf16([(h, v) + s for h, v, s in zip(w_hbm, w_vmem, stages)])

    if has_mix:
        h_ref, oda_ref, orw_ref = refs[0:3]
        wout_ref = w_vmem[3]
        h = (h_ref[...] + _dot(oda_ref[...], wout_ref[0:DA_WIDTH, :])
             + _dot(orw_ref[...], wout_ref[DA_WIDTH:DA_WIDTH + RW_WIDTH, :]))
        o_ref[...] = h
        res_ref = o_ref
    elif assemble is not None:
        x_ref, meta_ref = refs[0:2]
        tiles, n_prefix, last_shift, last_valid = assemble
        tm, d = x_ref.shape
        il = pl.program_id(0) % tiles
        x = x_ref[...]
        first, last = il == 0, il == tiles - 1
        h = jnp.where(first, pltpu.roll(x, n_prefix, 0), jnp.where(last, pltpu.roll(x, tm - last_shift, 0), x))
        row = lax.broadcasted_iota(jnp.int32, (tm, d), 0)
        prefix = jnp.concatenate([meta_ref[...], jnp.zeros((tm - n_prefix, d), F32)], axis=0)
        h = jnp.where(first & (row < n_prefix), prefix, h)
        h = jnp.where(last & (row >= last_valid), 0.0, h)
        o_ref[...] = h
        res_ref = o_ref
    else:
        h_ref = refs[0]
        h = h_ref[...]
        res_ref = h_ref
    xn_ref[...] = _rms_rows(h, g_ref[...]).astype(BF16)
    d_ff = wg_ref.shape[1]
    for c in range(d_ff // FF_CHUNK):
        cols = slice(c * FF_CHUNK, (c + 1) * FF_CHUNK)
        g = _dot(xn_ref[...], wg_ref[:, cols])
        u = _dot(xn_ref[...], wu_ref[:, cols])
        act_ref[:, cols] = ((g * jax.nn.sigmoid(g)) * u).astype(BF16)
    o_ref[...] = res_ref[...] + 0.5 * _dot(act_ref[...], wd_ref[...])


def _ffn(h, norm_g, wg, wu, wd, layer, mix=None, compact=None, assemble=None):
    m, d = h.shape
    d_ff = wg.shape[-1]
    row = lambda i: (i, 0)
    const2 = lambda i: (0, 0)
    kernel_assemble = None
    if assemble is not None:
        prefix, l_pad, n_rows = assemble
        n_prefix = prefix.shape[0]
        tm = ROW_TILE
        tiles = l_pad // tm
        n_seq = m // n_rows
        last_start = (tiles - 1) * tm - n_prefix
        assert tiles >= 2 and n_rows >= tm and n_prefix % SUBLANES == 0 and n_rows % SUBLANES == 0
        assert 0 <= last_start - (n_rows - tm) < tm
        kernel_assemble = (tiles, n_prefix, last_start - (n_rows - tm), n_prefix + n_rows - (tiles - 1) * tm)
        m_out = n_seq * l_pad
        window = lambda i: (pl.multiple_of(
            (i // tiles) * n_rows + jnp.clip((i % tiles) * tm - n_prefix, 0, n_rows - tm), SUBLANES), 0)
        row_spec = lambda width: pl.BlockSpec((pl.Element(tm), pl.Element(width)), window)
    elif compact is None:
        tm = ROW_TILE
        m_out = m
        row_spec = lambda width: pl.BlockSpec((tm, width), row)
    else:
        l_pad, first_row, n_rows = compact
        tm = COMPACT_ROW_TILE
        align = 2 * SUBLANES
        assert n_rows % tm == 0 and first_row % align == 0 and l_pad % align == 0 and tm % align == 0
        tiles = n_rows // tm
        m_out = (m // l_pad) * n_rows
        window = lambda i: (pl.multiple_of((i // tiles) * l_pad + first_row + (i % tiles) * tm, align), 0)
        row_spec = lambda width: pl.BlockSpec((pl.Element(tm), pl.Element(width)), window)
    in_specs = [row_spec(d)]
    args = [h]
    if assemble is not None:
        in_specs.append(pl.BlockSpec(prefix.shape, const2))
        args.append(prefix)
    weights = [wg, wu, wd]
    if mix is not None:
        o_da, o_rw, w_out = mix
        in_specs += [row_spec(DA_WIDTH), row_spec(RW_WIDTH)]
        args += [o_da, o_rw]
        weights.append(w_out)
    in_specs += [pl.BlockSpec((1, d), const2)] + [pl.BlockSpec(memory_space=pl.ANY)] * len(weights)
    args += [norm_g.reshape(1, d)] + weights
    return pl.pallas_call(
        functools.partial(_ffn_kernel, has_mix=mix is not None, assemble=kernel_assemble, layer=layer),
        grid=(m_out // tm,),
        in_specs=in_specs,
        out_specs=pl.BlockSpec((tm, d), row),
        out_shape=jax.ShapeDtypeStruct((m_out, d), F32),
        scratch_shapes=([pltpu.VMEM((tm, d), BF16), pltpu.VMEM((tm, d_ff), BF16)]
                        + [pltpu.VMEM(w.shape[-2:], BF16) for w in weights]
                        + [pltpu.VMEM((2, WEIGHT_SLAB_ROWS, d_ff), F32), pltpu.VMEM((2, WEIGHT_SLAB_ROWS, d), F32),
                           pltpu.SemaphoreType.DMA((2,)), pltpu.SemaphoreType.DMA((2,))]),
        compiler_params=pltpu.CompilerParams(dimension_semantics=("arbitrary",),
                                             vmem_limit_bytes=VMEM_LIMIT_BYTES),
        name="ffn_mix" if mix is not None else "ffn",
    )(*args)


def _qk_prep(u, gain_ref, c, s1, s2, ones_bd, scale, out_ref):
    rep = SEG_LANES // LANES
    c, s1, s2 = (jnp.concatenate([t] * rep, axis=1) for t in (c, s1, s2))
    for j in range(DA_QK_WIDTH // SEG_LANES):
        sl = slice(SEG_LANES * j, SEG_LANES * (j + 1))
        x = u[:, sl]
        ss = _seg_sum(x * x, ones_bd)
        xn = x * lax.rsqrt(ss * (1.0 / DA_QK_DIM) + NORM_EPS) * gain_ref[:, sl]
        half = ROPE_DIM // 2
        xr = xn * c + pltpu.roll(xn, SEG_LANES - half, 1) * s1 + pltpu.roll(xn, half, 1) * s2
        out_ref[:, sl] = (xr * scale).astype(BF16)


def _mixin_kernel(h_ref, g_ref, win_hbm, qg_ref, kg_ref, c_ref, s1_ref, s2_ref, ones_ref,
                  q_ref, k_ref, v_ref, urw_ref, win_ref, stage_ref, sem_ref, *, q_scale, layer):
    @pl.when(pl.program_id(0) == 0)
    def _():
        _fetch_weights_bf16([(win_hbm.at[layer], win_ref, stage_ref, sem_ref)])

    xn = _rms_rows(h_ref[...], g_ref[...]).astype(BF16)
    qw = DA_QK_WIDTH
    c, s1, s2, ones_bd = c_ref[...], s1_ref[...], s2_ref[...], ones_ref[...]
    uq = _dot(xn, win_ref[:, 0:qw])
    uk = _dot(xn, win_ref[:, qw:2 * qw])
    _qk_prep(uq, qg_ref, c, s1, s2, ones_bd, q_scale, q_ref)
    _qk_prep(uk, kg_ref, c, s1, s2, ones_bd, 1.0, k_ref)
    v_ref[...] = _dot(xn, win_ref[:, 2 * qw:DA_IN]).astype(BF16)
    urw_ref[...] = _dot(xn, win_ref[:, DA_IN:N_IN_PAD])


def _mix_in(h, norm_g, w_in, layer, q_gain, k_gain, rope_c, rope_s1, rope_s2, ones_bd, l_pad):
    m, d = h.shape
    tm = ROW_TILE
    tiles_per_seq = l_pad // tm
    row = lambda i: (i, 0)
    pos = lambda i: (i % tiles_per_seq, 0)
    const2 = lambda i: (0, 0)
    q_scale = DA_QK_DIM ** -0.5 * math.log2(math.e)
    return pl.pallas_call(
        functools.partial(_mixin_kernel, q_scale=q_scale, layer=layer),
        grid=(m // tm,),
        in_specs=[pl.BlockSpec((tm, d), row), pl.BlockSpec((1, d), const2),
                  pl.BlockSpec(memory_space=pl.ANY),
                  pl.BlockSpec((1, DA_QK_WIDTH), const2), pl.BlockSpec((1, DA_QK_WIDTH), const2),
                  pl.BlockSpec((tm, LANES), pos), pl.BlockSpec((tm, LANES), pos),
                  pl.BlockSpec((tm, LANES), pos), pl.BlockSpec((SEG_LANES, SEG_LANES), const2)],
        out_specs=[pl.BlockSpec((tm, DA_QK_WIDTH), row), pl.BlockSpec((tm, DA_QK_WIDTH), row),
                   pl.BlockSpec((tm, DA_WIDTH), row), pl.BlockSpec((tm, RW_IN_PAD), row)],
        out_shape=[jax.ShapeDtypeStruct((m, DA_QK_WIDTH), BF16), jax.ShapeDtypeStruct((m, DA_QK_WIDTH), BF16),
                   jax.ShapeDtypeStruct((m, DA_WIDTH), BF16), jax.ShapeDtypeStruct((m, RW_IN_PAD), F32)],
        scratch_shapes=[pltpu.VMEM((d, N_IN_PAD), BF16), pltpu.VMEM((2, WEIGHT_SLAB_ROWS, w_in.shape[-1]), F32),
                        pltpu.SemaphoreType.DMA((2,))],
        compiler_params=pltpu.CompilerParams(dimension_semantics=("arbitrary",),
                                             vmem_limit_bytes=VMEM_LIMIT_BYTES),
        name="mix_in",
    )(h, norm_g.reshape(1, d), w_in, q_gain, k_gain, rope_c, rope_s1, rope_s2, ones_bd)


def _attn_kernel(lam_ref, q_ref, k_ref, v_ref, sub_ref, o_ref, q2_s, vx_s, m_s, l_s, acc_s, s_s, *, tq, lam_init):
    qi = pl.program_id(2)
    n_heads = q_ref.shape[1] // LANES
    head_lanes = [slice(LANES * hh, LANES * (hh + 1)) for hh in range(n_heads)]
    vx_lanes = [slice(2 * LANES * hh, 2 * LANES * (hh + 1)) for hh in range(n_heads)]
    streams = [(hh, slice((2 * hh + c) * tq, (2 * hh + c + 1) * tq)) for hh in range(n_heads) for c in range(2)]

    @pl.when(qi == 0)
    def _():
        for hh in range(n_heads):
            vx_s[:, 2 * LANES * hh:2 * LANES * hh + LANES] = v_ref[:, head_lanes[hh]]
            vx_s[:, 2 * LANES * hh + LANES:2 * LANES * (hh + 1)] = jnp.ones((vx_s.shape[0], LANES), BF16)

    for hh in range(n_heads):
        q = q_ref[:, head_lanes[hh]]
        lane = lax.broadcasted_iota(jnp.int32, q.shape, 1)
        zero = jnp.zeros_like(q)
        q2_s[streams[2 * hh][1], :] = jnp.where(lane < DA_QK_DIM, q, zero)
        q2_s[streams[2 * hh + 1][1], :] = jnp.where(lane < DA_QK_DIM, zero, q)
    m_s[...] = jnp.full(m_s.shape, -1e30, F32)
    l_s[...] = jnp.zeros_like(l_s)
    acc_s[...] = jnp.zeros_like(acc_s)

    def process(start, width, diag_offset):
        def scores(st):
            hh, rows = streams[st]
            s_s[rows, 0:width] = _dot_nt(q2_s[rows, :], k_ref[pl.ds(start, width), head_lanes[hh]])

        lead = min(ATTN_SCORE_LEAD, len(streams))
        for st in range(lead):
            scores(st)
        for st, (hh, rows) in enumerate(streams):
            s = s_s[rows, 0:width]
            if diag_offset is not None:
                row = lax.broadcasted_iota(jnp.int32, s.shape, 0)
                col = lax.broadcasted_iota(jnp.int32, s.shape, 1)
                s = jnp.where(col <= row + diag_offset, s, -jnp.inf)
            m_prev = m_s[rows, :]
            m_new = jnp.maximum(m_prev, jnp.max(s, axis=1, keepdims=True))
            alpha = jnp.exp2(m_prev - m_new)
            p = jnp.exp2(s - jnp.concatenate([m_new] * (width // LANES), axis=1))
            pv = _dot(p.astype(BF16), vx_s[pl.ds(start, width), vx_lanes[hh]])
            acc_s[rows, :] = acc_s[rows, :] * alpha + pv[:, 0:LANES]
            l_s[rows, :] = l_s[rows, :] * alpha + pv[:, LANES:2 * LANES]
            m_s[rows, :] = m_new
            if st + lead < len(streams):
                scores(st + lead)

    def wide_body(t, carry):
        process(pl.multiple_of(t * (2 * tq), 2 * tq), 2 * tq, None)
        return carry

    n_wide = qi // 2

    def wide_pair(t, carry):
        wide_body(2 * t, carry)
        return wide_body(2 * t + 1, carry)

    lax.fori_loop(0, n_wide // 2, wide_pair, 0)

    @pl.when(n_wide % 2 == 1)
    def _():
        wide_body(n_wide - 1, 0)

    @pl.when(qi % 2 == 0)
    def _():
        process(pl.multiple_of(qi * tq, tq), tq, 0)

    @pl.when(qi % 2 == 1)
    def _():
        process(pl.multiple_of((qi - 1) * tq, tq), 2 * tq, tq)

    lam1 = jnp.exp(jnp.sum(lam_ref[0:1, :] * lam_ref[1:2, :], axis=1, keepdims=True))
    lam2 = jnp.exp(jnp.sum(lam_ref[2:3, :] * lam_ref[3:4, :], axis=1, keepdims=True))
    lam = lam1 - lam2 + lam_init
    for hh in range(n_heads):
        r0, r1 = streams[2 * hh][1], streams[2 * hh + 1][1]
        o = acc_s[r0, :] / l_s[r0, :] - lam * (acc_s[r1, :] / l_s[r1, :])
        o = _rms_rows(o, sub_ref[...]) * (1.0 - lam_init)
        o_ref[:, head_lanes[hh]] = o.astype(o_ref.dtype)


def _attention(q, k, v, lam_vecs, subln, lam_init):
    b, l_pad, _ = q.shape
    tq = ATTN_TILE
    nq = l_pad // tq
    nh = ATTN_HEADS_PER_STEP
    hw = nh * LANES
    n_streams = 2 * nh
    kv_spec = pl.BlockSpec((None, l_pad, hw), lambda bi, h, qi: (bi, 0, h), pipeline_mode=pl.Buffered(1))
    return pl.pallas_call(
        functools.partial(_attn_kernel, tq=tq, lam_init=lam_init),
        grid=(b, DA_HEADS // nh, nq),
        in_specs=[pl.BlockSpec(lam_vecs.shape, lambda bi, h, qi: (0, 0)),
                  pl.BlockSpec((None, tq, hw), lambda bi, h, qi: (bi, qi, h)),
                  kv_spec, kv_spec,
                  pl.BlockSpec((1, DA_V_DIM), lambda bi, h, qi: (0, 0))],
        out_specs=pl.BlockSpec((None, tq, hw), lambda bi, h, qi: (bi, qi, h)),
        out_shape=jax.ShapeDtypeStruct((b, l_pad, DA_WIDTH), BF16),
        scratch_shapes=[pltpu.VMEM((n_streams * tq, LANES), BF16), pltpu.VMEM((l_pad, 2 * hw), BF16),
                        pltpu.VMEM((n_streams * tq, LANES), F32), pltpu.VMEM((n_streams * tq, LANES), F32),
                        pltpu.VMEM((n_streams * tq, LANES), F32),
                        pltpu.VMEM((n_streams * tq, 2 * tq), F32)],
        compiler_params=pltpu.CompilerParams(dimension_semantics=("arbitrary", "arbitrary", "arbitrary"),
                                             vmem_limit_bytes=VMEM_LIMIT_BYTES),
        name="diff_attention",
    )(lam_vecs, q, k, v, subln.reshape(1, DA_V_DIM))


def _expand_heads(x, lo_mask):
    zero = jnp.zeros_like(x)
    return jnp.concatenate([jnp.where(lo_mask, x, zero), jnp.where(lo_mask, zero, x)], axis=0)


def _run(stages):
    try:
        while True:
            next(stages)
    except StopIteration as stop:
        return stop.value


def _run_interleaved(stages_a, stages_b):
    live = [stages_a, stages_b]
    results = [None, None]
    while live[0] is not None or live[1] is not None:
        for n in range(2):
            if live[n] is None:
                continue
            try:
                next(live[n])
            except StopIteration as stop:
                results[n] = stop.value
                live[n] = None
    return results


def _unit_lower_inverse(a_list):
    n = a_list[0].shape[0]
    eye = (lax.broadcasted_iota(jnp.int32, (n, n), 0) == lax.broadcasted_iota(jnp.int32, (n, n), 1)).astype(F32)
    mm = lambda x, y: _dot(x.astype(BF16), y.astype(BF16))
    t_list = [eye + a for a in a_list]
    p_list = [mm(a, a) for a in a_list]
    yield
    levels = int(math.log2(RW_CHUNK))
    for _ in range(levels - 2):
        pt_list = [mm(p, jnp.concatenate([p, t], axis=1)) for p, t in zip(p_list, t_list)]
        t_list = [t + pt[:, n:] for t, pt in zip(t_list, pt_list)]
        p_list = [pt[:, :n] for pt in pt_list]
        yield
    return [t + mm(p, t) for p, t in zip(p_list, t_list)]


_RW_CARRY = ("ar", "bke", "v2", "t_inv", "akv", "arbk", "g_end", "bonus", "gate")


def _rwkv_kernel(u_ref, next_ref, mix_ref, w2a2_ref, w0_ref, a0_ref, g2_ref, kk_ref, ka_ref, rk_ref,
                 lnw_ref, lnb_ref, ones_ref, tri_ref, o_ref, s_ref, *carry_refs):
    i = pl.program_id(1)
    carry = dict(zip(_RW_CARRY, carry_refs))
    c_len = RW_CHUNK
    w = RW_WIDTH
    n2 = 2 * c_len
    n_sub = u_ref.shape[0] // RW_ROWS
    n_chunks = RW_ROWS // c_len
    n_pairs = w // LANES
    pair_lanes = [slice(LANES * p, LANES * (p + 1)) for p in range(n_pairs)]
    seg_lanes = [slice(SEG_LANES * p, SEG_LANES * (p + 1)) for p in range(w // SEG_LANES)]

    @pl.when(i == 0)
    def _():
        s_ref[...] = jnp.zeros_like(s_ref)

    ones_bd = ones_ref[...]
    tri = tri_ref[...]
    r_i = lax.broadcasted_iota(jnp.int32, (n2, n2), 0)
    c_i = lax.broadcasted_iota(jnp.int32, (n2, n2), 1)
    t_row = jnp.where(r_i >= c_len, r_i - c_len, r_i)
    t_col = jnp.where(c_i >= c_len, c_i - c_len, c_i)
    strict = t_col < t_row
    incl = t_col <= t_row
    lo_mask = lax.broadcasted_iota(jnp.int32, (c_len, LANES), 1) < RW_HEAD

    def prepare(src_ref, base, prev_row):
        def shifted(lo, hi):
            u = src_ref[base:base + RW_ROWS, lo:hi]
            last_prev = prev_row(lo, hi)
            rolled = pltpu.roll(u, 1, 0)
            row = lax.broadcasted_iota(jnp.int32, u.shape, 0)
            prev = jnp.where(row == 0, last_prev, rolled)
            return u + (prev - u) * mix_ref[:, lo:hi]

        r = shifted(0, w)
        k = shifted(w, 2 * w)
        v = shifted(2 * w, 3 * w)
        lora_in = shifted(3 * w, 3 * w + LANES)
        g_in = shifted(3 * w + LANES, RW_IN_PAD)

        lane = lax.broadcasted_iota(jnp.int32, lora_in.shape, 1)
        lora_act = jnp.where(lane < D_DECAY_LORA, jnp.tanh(lora_in), lora_in)
        wa = _dot(lora_act.astype(BF16), w2a2_ref[...])
        g = _dot(jax.nn.sigmoid(g_in).astype(BF16), g2_ref[...])
        yield
        w_pre = w0_ref[...] + wa[:, 0:w]
        lw = (-math.exp(-0.5)) * jax.nn.sigmoid(w_pre)
        a = jax.nn.sigmoid(a0_ref[...] + wa[:, w:2 * w])

        cums = []
        for c in range(n_chunks):
            lw_c = lw[c * c_len:(c + 1) * c_len]
            hi, lo = _split2(lw_c)
            cums.append(_dot(tri, hi) + _dot(tri, lo))

        kk_raw = k * kk_ref[...]
        k_mod = k * (1.0 + (a - 1.0) * ka_ref[...])
        rkr = r * k_mod * rk_ref[...]
        ss_l = [_seg_sum(kk_raw[:, sl] * kk_raw[:, sl], ones_bd) for sl in seg_lanes]
        rk_l = [_seg_sum(rkr[:, sl], ones_bd) for sl in seg_lanes]
        yield
        kk = jnp.concatenate([kk_raw[:, sl] * lax.rsqrt(jnp.maximum(ss, 1e-24))
                              for sl, ss in zip(seg_lanes, ss_l)], axis=1)
        bonus = jnp.concatenate([s * v[:, sl] for sl, s in zip(seg_lanes, rk_l)], axis=1)
        a_neg = -kk
        b_vec = kk * a

        ar_l, bk_l, bke_l, v2_l, g_end_l = [], [], [], [], []
        for c in range(n_chunks):
            rs = slice(c * c_len, (c + 1) * c_len)
            cum = cums[c]
            cum_end = cum[c_len - 1:c_len, :]
            e_in = jnp.exp(cum)
            e_out = jnp.exp(-cum)
            e_end = jnp.exp(cum_end - cum)
            r_t = r[rs] * e_in
            a_t = a_neg[rs] * jnp.exp(cum - lw[rs])
            k_t = k_mod[rs] * e_out
            b_t = b_vec[rs] * e_out
            k_e = k_mod[rs] * e_end
            b_e = b_vec[rs] * e_end
            g_end_l.append(jnp.exp(cum_end))
            v_c = v[rs]
            for sl in pair_lanes:
                ar_l.append(jnp.concatenate([_expand_heads(a_t[:, sl], lo_mask), _expand_heads(r_t[:, sl], lo_mask)],
                                            axis=0).astype(BF16))
                bk_l.append(jnp.concatenate([_expand_heads(b_t[:, sl], lo_mask), _expand_heads(k_t[:, sl], lo_mask)],
                                            axis=0).astype(BF16))
                bke_l.append(jnp.concatenate([_expand_heads(b_e[:, sl], lo_mask), _expand_heads(k_e[:, sl], lo_mask)],
                                             axis=0).astype(BF16))
                v2_l.append(_expand_heads(v_c[:, sl], lo_mask).astype(BF16))
        aa_l = [_dot_nt(ar, bk) for ar, bk in zip(ar_l, bk_l)]
        yield
        akv_l = [_dot(jnp.where(strict, aa[0:n2, n2:2 * n2], 0.0).astype(BF16), v2) for aa, v2 in zip(aa_l, v2_l)]
        arbk_l = [jnp.concatenate([jnp.where(incl, aa[n2:2 * n2, 0:n2], 0.0),
                                   jnp.where(incl, aa[n2:2 * n2, n2:2 * n2], 0.0)], axis=1).astype(BF16)
                  for aa in aa_l]
        t_inv_l = yield from _unit_lower_inverse([jnp.where(strict, aa[0:n2, 0:n2], 0.0) for aa in aa_l])
        t_inv_l = [t.astype(BF16) for t in t_inv_l]
        yield
        return dict(ar=ar_l, bke=bke_l, v2=v2_l, g_end=g_end_l, t_inv=t_inv_l, akv=akv_l, arbk=arbk_l,
                    bonus=bonus, gate=g)

    def advance(j, pre):
        base = j * RW_ROWS
        y_chunks = []
        for c in range(n_chunks):
            idx = [c * n_pairs + p for p in range(n_pairs)]
            states = [s_ref[p] for p in range(n_pairs)]
            arh = [_dot_nt(pre["ar"][q], s.astype(BF16)) for q, s in zip(idx, states)]
            yield
            u_l = [_dot(pre["t_inv"][q], (h[0:n2] + pre["akv"][q]).astype(BF16)) for q, h in zip(idx, arh)]
            yield
            uv = [jnp.concatenate([u.astype(BF16), pre["v2"][q]], axis=0) for q, u in zip(idx, u_l)]
            for p in range(n_pairs):
                s_ref[p] = states[p] * pre["g_end"][c][:, pair_lanes[p]] + _dot_tn(uv[p], pre["bke"][idx[p]])
            y2 = [h[n2:2 * n2] + _dot(pre["arbk"][q], x) for q, h, x in zip(idx, arh, uv)]
            yield
            y_chunks.append(jnp.concatenate([y[0:c_len] + y[c_len:n2] for y in y2], axis=1))
        y = jnp.concatenate(y_chunks, axis=0)

        mu_l = [_seg_sum(y[:, sl], ones_bd) * (1.0 / RW_HEAD) for sl in seg_lanes]
        yield
        d_l = [y[:, sl] - mu for sl, mu in zip(seg_lanes, mu_l)]
        var_l = [_seg_sum(d * d, ones_bd) * (1.0 / RW_HEAD) for d in d_l]
        yield
        yn = jnp.concatenate([d * lax.rsqrt(var + GN_EPS) for d, var in zip(d_l, var_l)], axis=1)
        yn = yn * lnw_ref[...] + lnb_ref[...]
        o_ref[base:base + RW_ROWS, :] = ((yn + pre["bonus"]) * pre["gate"]).astype(o_ref.dtype)

    def save(pre):
        for name in _RW_CARRY:
            value, ref = pre[name], carry[name]
            if isinstance(value, list):
                for q, item in enumerate(value):
                    ref[q] = item
            else:
                ref[...] = value

    def load():
        return {name: ([ref[q] for q in range(ref.shape[0])] if len(ref.shape) == 3 else ref[...])
                for name, ref in carry.items()}

    def row_before(src_ref, row):
        return lambda lo, hi: src_ref[row:row + 1, lo:hi]

    @pl.when(i == 0)
    def _():
        save(_run(prepare(u_ref, 0, lambda lo, hi: jnp.zeros((1, hi - lo), F32))))

    pre = load()
    for j in range(n_sub):
        last_row = (j + 1) * RW_ROWS - 1
        if j + 1 < n_sub:
            upcoming = prepare(u_ref, (j + 1) * RW_ROWS, row_before(u_ref, last_row))
        else:
            upcoming = prepare(next_ref, 0, row_before(u_ref, last_row))
        pre, _ = _run_interleaved(upcoming, advance(j, pre))
    save(pre)


def _rwkv(u_rw, shift_mix_p, w2a2, w0, a0, g2_p, k_k, k_a, r_k, ln_w, ln_b, ones_bd, tri):
    b, l_pad, _ = u_rw.shape
    rows = RW_BLOCK_ROWS
    sub_per_block = rows // RW_ROWS
    last_sub = l_pad // RW_ROWS - 1
    n_prob = (RW_ROWS // RW_CHUNK) * (RW_WIDTH // LANES)
    n2 = 2 * RW_CHUNK
    vec = lambda x: x.reshape(1, -1).astype(F32)
    const = lambda bi, i: (0, 0)
    vec_spec = pl.BlockSpec((1, RW_WIDTH), const)
    carry_shapes = dict(
        ar=pltpu.VMEM((n_prob, 2 * n2, LANES), BF16), bke=pltpu.VMEM((n_prob, 2 * n2, LANES), BF16),
        v2=pltpu.VMEM((n_prob, n2, LANES), BF16), t_inv=pltpu.VMEM((n_prob, n2, n2), BF16),
        akv=pltpu.VMEM((n_prob, n2, LANES), F32), arbk=pltpu.VMEM((n_prob, n2, 2 * n2), BF16),
        g_end=pltpu.VMEM((RW_ROWS // RW_CHUNK, 1, RW_WIDTH), F32),
        bonus=pltpu.VMEM((RW_ROWS, RW_WIDTH), F32), gate=pltpu.VMEM((RW_ROWS, RW_WIDTH), F32))
    return pl.pallas_call(
        _rwkv_kernel,
        grid=(b, l_pad // rows),
        in_specs=[pl.BlockSpec((None, rows, RW_IN_PAD), lambda bi, i: (bi, i, 0)),
                  pl.BlockSpec((None, RW_ROWS, RW_IN_PAD),
                               lambda bi, i: (bi, jnp.minimum((i + 1) * sub_per_block, last_sub), 0)),
                  pl.BlockSpec((1, RW_IN_PAD), const),
                  pl.BlockSpec(w2a2.shape, const),
                  vec_spec, vec_spec,
                  pl.BlockSpec(g2_p.shape, const),
                  vec_spec, vec_spec, vec_spec, vec_spec, vec_spec,
                  pl.BlockSpec((SEG_LANES, SEG_LANES), const),
                  pl.BlockSpec((RW_CHUNK, RW_CHUNK), const)],
        out_specs=pl.BlockSpec((None, rows, RW_WIDTH), lambda bi, i: (bi, i, 0)),
        out_shape=jax.ShapeDtypeStruct((b, l_pad, RW_WIDTH), BF16),
        scratch_shapes=[pltpu.VMEM((RW_WIDTH // LANES, LANES, LANES), F32)] + [carry_shapes[n] for n in _RW_CARRY],
        compiler_params=pltpu.CompilerParams(dimension_semantics=("arbitrary", "arbitrary"),
                                             vmem_limit_bytes=VMEM_LIMIT_BYTES),
        name="rwkv7",
    )(u_rw, u_rw, vec(shift_mix_p), w2a2, vec(w0), vec(a0), g2_p, vec(k_k), vec(k_a), vec(r_k),
      vec(ln_w), vec(ln_b), ones_bd, tri)


def _rope_lane_tables(length):
    pos = jnp.arange(length, dtype=F32)
    inv_freq = ROPE_THETA ** (-jnp.arange(0, ROPE_DIM, 2, dtype=F32) / ROPE_DIM)
    ang = pos[:, None] * inv_freq[None, :]
    cos, sin = jnp.cos(ang), jnp.sin(ang)
    half = ROPE_DIM // 2
    rest = DA_QK_DIM - ROPE_DIM
    ones = jnp.ones((length, rest), F32)
    zeros = lambda n: jnp.zeros((length, n), F32)
    c = jnp.concatenate([cos, cos, ones], axis=1)
    s1 = jnp.concatenate([-sin, zeros(half + rest)], axis=1)
    s2 = jnp.concatenate([zeros(half), sin, zeros(rest)], axis=1)
    rep = LANES // DA_QK_DIM
    return jnp.tile(c, (1, rep)), jnp.tile(s1, (1, rep)), jnp.tile(s2, (1, rep))


def kernel(x, meta_tokens, ffn1_norm, ffn1_w_gate, ffn1_w_up, ffn1_w_down, mix_norm, w_in, da_q_norm, da_k_norm, da_lambda_q1, da_lambda_k1, da_lambda_q2, da_lambda_k2, da_subln, rw_shift_mix, rw_w0, rw_w2, rw_a0, rw_a2, rw_g2, rw_k_k, rw_k_a, rw_r_k, rw_ln_w, rw_ln_b, w_out, ffn2_norm, ffn2_w_gate, ffn2_w_up, ffn2_w_down):
    bsz, t, d = x.shape
    depth = w_in.shape[0]
    l = N_META + t
    l_pad = -(-l // Q_BLOCK) * Q_BLOCK
    assert d == D_MODEL and l_pad % ROW_TILE == 0 and l_pad % ATTN_TILE == 0 and l_pad % RW_BLOCK_ROWS == 0
    h = x.reshape(bsz * t, d)
    meta = meta_tokens.astype(x.dtype)

    rope_c, rope_s1, rope_s2 = _rope_lane_tables(l_pad)
    lane_head = jnp.arange(SEG_LANES) // RW_HEAD
    ones_bd = (lane_head[:, None] == lane_head[None, :]).astype(BF16)
    tri = (jnp.arange(RW_CHUNK)[:, None] >= jnp.arange(RW_CHUNK)[None, :]).astype(BF16)

    for layer in range(depth):
        lam_init = 0.8 - 0.6 * math.exp(-0.3 * layer)
        shift_mix_p = jnp.pad(rw_shift_mix[layer], (0, RW_IN_PAD - RW_IN))
        zeros_lora = jnp.zeros((D_DECAY_LORA, RW_WIDTH), F32)
        w2a2 = jnp.concatenate([jnp.concatenate([rw_w2[layer], zeros_lora], axis=1),
                                jnp.concatenate([zeros_lora, rw_a2[layer]], axis=1)], axis=0).astype(BF16)
        g2_p = jnp.pad(rw_g2[layer], ((0, RW_LORA_PAD - D_GATE_LORA), (0, 0))).astype(BF16)
        q_gain = jnp.tile(da_q_norm[layer], DA_QK_WIDTH // DA_QK_DIM).reshape(1, DA_QK_WIDTH)
        k_gain = jnp.tile(da_k_norm[layer], DA_QK_WIDTH // DA_QK_DIM).reshape(1, DA_QK_WIDTH)
        lam_vecs = jnp.stack([da_lambda_q1[layer], da_lambda_k1[layer],
                              da_lambda_q2[layer], da_lambda_k2[layer]]).astype(F32)

        h = _ffn(h, ffn1_norm[layer], ffn1_w_gate, ffn1_w_up, ffn1_w_down, layer,
                 assemble=(meta, l_pad, t) if layer == 0 else None)
        q, k, v, u_rw = _mix_in(h, mix_norm[layer], w_in, layer, q_gain, k_gain,
                                rope_c, rope_s1, rope_s2, ones_bd, l_pad)
        shape3 = lambda a: a.reshape(bsz, l_pad, a.shape[-1])
        o_da = _attention(shape3(q), shape3(k), shape3(v), lam_vecs, da_subln[layer], lam_init)
        o_rw = _rwkv(shape3(u_rw), shift_mix_p, w2a2, rw_w0[layer], rw_a0[layer], g2_p,
                     rw_k_k[layer], rw_k_a[layer], rw_r_k[layer], rw_ln_w[layer], rw_ln_b[layer],
                     ones_bd, tri)
        last = layer == depth - 1
        h = _ffn(h, ffn2_norm[layer], ffn2_w_gate, ffn2_w_up, ffn2_w_down, layer,
                 mix=(o_da.reshape(bsz * l_pad, DA_WIDTH), o_rw.reshape(bsz * l_pad, RW_WIDTH), w_out),
                 compact=(l_pad, N_META, t) if last else None)
    return h.reshape(bsz, t, d)
```

```python
import functools
import math

import jax
import jax.numpy as jnp
from jax import lax
from jax.experimental import pallas as pl
from jax.experimental.pallas import tpu as pltpu

F32 = jnp.float32
BF16 = jnp.bfloat16

D_MODEL = 1024
N_META = 16
Q_BLOCK = 128
ROPE_THETA = 500000.0
NORM_EPS = 1e-6
DA_HEADS = 4
DA_QK_DIM = 64
DA_V_DIM = 2 * DA_QK_DIM
DA_WIDTH = DA_HEADS * DA_V_DIM
DA_QK_WIDTH = DA_HEADS * 2 * DA_QK_DIM
ROPE_DIM = DA_QK_DIM // 4
RW_HEAD = 64
RW_WIDTH = D_MODEL - DA_WIDTH
RW_HEADS = RW_WIDTH // RW_HEAD
D_DECAY_LORA = 64
D_AAA_LORA = 64
D_GATE_LORA = 160
GN_EPS = 64e-5
DA_IN = 2 * DA_QK_WIDTH + DA_WIDTH
RW_IN = 3 * RW_WIDTH + D_DECAY_LORA + D_AAA_LORA + D_GATE_LORA
D_FF = 2816

LANES = 128
SUBLANES = 8
SEG_LANES = 256
VMEM_LIMIT_BYTES = 56 * 1024 * 1024

ROW_TILE = 640
COMPACT_ROW_TILE = 512
FF_CHUNK = 256
WEIGHT_SLAB_ROWS = 256
ATTN_TILE = 640
ATTN_HEADS_PER_STEP = 2
ATTN_SCORE_LEAD = 2
RW_CHUNK = 64
RW_ROWS = 2 * RW_CHUNK
RW_BLOCK_ROWS = 5 * RW_ROWS
RW_LORA_PAD = 2 * LANES
RW_IN_PAD = 3 * RW_WIDTH + LANES + RW_LORA_PAD
N_IN_PAD = DA_IN + RW_IN_PAD

_NT = (((1,), (1,)), ((), ()))
_TN = (((0,), (0,)), ((), ()))


def _dot(a, b):
    return jnp.dot(a, b, preferred_element_type=F32)


def _dot_nt(a, b):
    return lax.dot_general(a, b, _NT, preferred_element_type=F32)


def _dot_tn(a, b):
    return lax.dot_general(a, b, _TN, preferred_element_type=F32)


def _split2(x):
    hi = x.astype(BF16)
    lo = (x - hi.astype(F32)).astype(BF16)
    return hi, lo


def _seg_sum(x, ones_bd):
    return _dot(x.astype(BF16), ones_bd)


def _rms_rows(h, gain):
    ms = jnp.mean(h * h, axis=-1, keepdims=True)
    return h * lax.rsqrt(ms + NORM_EPS) * gain


def _fetch_weights_bf16(jobs):
    slabs = []
    used = {}
    for hbm, vmem, stage, sem in jobs:
        rows, cols = hbm.shape
        assert rows % WEIGHT_SLAB_ROWS == 0 and stage.shape == (2, WEIGHT_SLAB_ROWS, cols), (hbm.shape, stage.shape)
        for r0 in range(0, rows, WEIGHT_SLAB_ROWS):
            slot = used.get(id(stage), 0) % 2
            used[id(stage)] = used.get(id(stage), 0) + 1
            slabs.append((hbm, vmem, stage, sem, r0, cols, slot))

    def copy(k):
        hbm, _, stage, sem, r0, _, slot = slabs[k]
        return pltpu.make_async_copy(hbm.at[r0:r0 + WEIGHT_SLAB_ROWS, :], stage.at[slot], sem.at[slot])

    copy(0).start()
    for k, (_, vmem, stage, _, r0, cols, slot) in enumerate(slabs):
        if k + 1 < len(slabs):
            copy(k + 1).start()
        copy(k).wait()
        vmem[r0:r0 + WEIGHT_SLAB_ROWS, 0:cols] = stage[slot].astype(BF16)
    for hbm, vmem, _, _ in jobs:
        if vmem.shape[1] > hbm.shape[1]:
            vmem[:, hbm.shape[1]:] = jnp.zeros((vmem.shape[0], vmem.shape[1] - hbm.shape[1]), BF16)


def _ffn_kernel(*refs, has_mix, assemble, layer):
    n_w = 4 if has_mix else 3
    n_in = len(refs) - (1 + 2 + n_w + 4)
    w_hbm = [w.at[layer] for w in refs[n_in - n_w:n_in]]
    o_ref, xn_ref, act_ref = refs[n_in:n_in + 3]
    w_vmem = refs[n_in + 3:n_in + 3 + n_w]
    stage_wide, stage_narrow, sem_wide, sem_narrow = refs[n_in + 3 + n_w:]
    g_ref = refs[n_in - n_w - 1]
    wg_ref, wu_ref, wd_ref = w_vmem[0:3]

    @pl.when(pl.program_id(0) == 0)
    def _():
        stages = [(stage_wide, sem_wide), (stage_wide, sem_wide), (stage_narrow, sem_narrow),
                  (stage_narrow, sem_narrow)]
        _fetch_weights_bf16([(h, v) + s for h, v, s in zip(w_hbm, w_vmem, stages)])

    if has_mix:
        h_ref, oda_ref, orw_ref = refs[0:3]
        wout_ref = w_vmem[3]
        h = (h_ref[...] + _dot(oda_ref[...], wout_ref[0:DA_WIDTH, :])
             + _dot(orw_ref[...], wout_ref[DA_WIDTH:DA_WIDTH + RW_WIDTH, :]))
        o_ref[...] = h
        res_ref = o_ref
    elif assemble is not None:
        x_ref, meta_ref = refs[0:2]
        tiles, n_prefix, last_shift, last_valid = assemble
        tm, d = x_ref.shape
        il = pl.program_id(0) % tiles
        x = x_ref[...]
        first, last = il == 0, il == tiles - 1
        h = jnp.where(first, pltpu.roll(x, n_prefix, 0), jnp.where(last, pltpu.roll(x, tm - last_shift, 0), x))
        row = lax.broadcasted_iota(jnp.int32, (tm, d), 0)
        prefix = jnp.concatenate([meta_ref[...], jnp.zeros((tm - n_prefix, d), F32)], axis=0)
        h = jnp.where(first & (row < n_prefix), prefix, h)
        h = jnp.where(last & (row >= last_valid), 0.0, h)
        o_ref[...] = h
        res_ref = o_ref
    else:
        h_ref = refs[0]
        h = h_ref[...]
        res_ref = h_ref
    xn_ref[...] = _rms_rows(h, g_ref[...]).astype(BF16)
    d_ff = wg_ref.shape[1]
    for c in range(d_ff // FF_CHUNK):
        cols = slice(c * FF_CHUNK, (c + 1) * FF_CHUNK)
        g = _dot(xn_ref[...], wg_ref[:, cols])
        u = _dot(xn_ref[...], wu_ref[:, cols])
        act_ref[:, cols] = ((g * jax.nn.sigmoid(g)) * u).astype(BF16)
    o_ref[...] = res_ref[...] + 0.5 * _dot(act_ref[...], wd_ref[...])


def _ffn(h, norm_g, wg, wu, wd, layer, mix=None, compact=None, assemble=None):
    m, d = h.shape
    d_ff = wg.shape[-1]
    row = lambda i: (i, 0)
    const2 = lambda i: (0, 0)
    kernel_assemble = None
    if assemble is not None:
        prefix, l_pad, n_rows = assemble
        n_prefix = prefix.shape[0]
        tm = ROW_TILE
        tiles = l_pad // tm
        n_seq = m // n_rows
        last_start = (tiles - 1) * tm - n_prefix
        assert tiles >= 2 and n_rows >= tm and n_prefix % SUBLANES == 0 and n_rows % SUBLANES == 0
        assert 0 <= last_start - (n_rows - tm) < tm
        kernel_assemble = (tiles, n_prefix, last_start - (n_rows - tm), n_prefix + n_rows - (tiles - 1) * tm)
        m_out = n_seq * l_pad
        window = lambda i: (pl.multiple_of(
            (i // tiles) * n_rows + jnp.clip((i % tiles) * tm - n_prefix, 0, n_rows - tm), SUBLANES), 0)
        row_spec = lambda width: pl.BlockSpec((pl.Element(tm), pl.Element(width)), window)
    elif compact is None:
        tm = ROW_TILE
        m_out = m
        row_spec = lambda width: pl.BlockSpec((tm, width), row)
    else:
        l_pad, first_row, n_rows = compact
        tm = COMPACT_ROW_TILE
        align = 2 * SUBLANES
        assert n_rows % tm == 0 and first_row % align == 0 and l_pad % align == 0 and tm % align == 0
        tiles = n_rows // tm
        m_out = (m // l_pad) * n_rows
        window = lambda i: (pl.multiple_of((i // tiles) * l_pad + first_row + (i % tiles) * tm, align), 0)
        row_spec = lambda width: pl.BlockSpec((pl.Element(tm), pl.Element(width)), window)
    in_specs = [row_spec(d)]
    args = [h]
    if assemble is not None:
        in_specs.append(pl.BlockSpec(prefix.shape, const2))
        args.append(prefix)
    weights = [wg, wu, wd]
    if mix is not None:
        o_da, o_rw, w_out = mix
        in_specs += [row_spec(DA_WIDTH), row_spec(RW_WIDTH)]
        args += [o_da, o_rw]
        weights.append(w_out)
    in_specs += [pl.BlockSpec((1, d), const2)] + [pl.BlockSpec(memory_space=pl.ANY)] * len(weights)
    args += [norm_g.reshape(1, d)] + weights
    return pl.pallas_call(
        functools.partial(_ffn_kernel, has_mix=mix is not None, assemble=kernel_assemble, layer=layer),
        grid=(m_out // tm,),
        in_specs=in_specs,
        out_specs=pl.BlockSpec((tm, d), row),
        out_shape=jax.ShapeDtypeStruct((m_out, d), F32),
        scratch_shapes=([pltpu.VMEM((tm, d), BF16), pltpu.VMEM((tm, d_ff), BF16)]
                        + [pltpu.VMEM(w.shape[-2:], BF16) for w in weights]
                        + [pltpu.VMEM((2, WEIGHT_SLAB_ROWS, d_ff), F32), pltpu.VMEM((2, WEIGHT_SLAB_ROWS, d), F32),
                           pltpu.SemaphoreType.DMA((2,)), pltpu.SemaphoreType.DMA((2,))]),
        compiler_params=pltpu.CompilerParams(dimension_semantics=("arbitrary",),
                                             vmem_limit_bytes=VMEM_LIMIT_BYTES),
        name="ffn_mix" if mix is not None else "ffn",
    )(*args)


def _qk_prep(u, gain_ref, c, s1, s2, ones_bd, scale, out_ref):
    rep = SEG_LANES // LANES
    c, s1, s2 = (jnp.concatenate([t] * rep, axis=1) for t in (c, s1, s2))
    for j in range(DA_QK_WIDTH // SEG_LANES):
        sl = slice(SEG_LANES * j, SEG_LANES * (j + 1))
        x = u[:, sl]
        ss = _seg_sum(x * x, ones_bd)
        xn = x * lax.rsqrt(ss * (1.0 / DA_QK_DIM) + NORM_EPS) * gain_ref[:, sl]
        half = ROPE_DIM // 2
        xr = xn * c + pltpu.roll(xn, SEG_LANES - half, 1) * s1 + pltpu.roll(xn, half, 1) * s2
        out_ref[:, sl] = (xr * scale).astype(BF16)


def _mixin_kernel(h_ref, g_ref, win_hbm, qg_ref, kg_ref, c_ref, s1_ref, s2_ref, ones_ref,
                  q_ref, k_ref, v_ref, urw_ref, win_ref, stage_ref, sem_ref, *, q_scale, layer):
    @pl.when(pl.program_id(0) == 0)
    def _():
        _fetch_weights_bf16([(win_hbm.at[layer], win_ref, stage_ref, sem_ref)])

    xn = _rms_rows(h_ref[...], g_ref[...]).astype(BF16)
    qw = DA_QK_WIDTH
    c, s1, s2, ones_bd = c_ref[...], s1_ref[...], s2_ref[...], ones_ref[...]
    uq = _dot(xn, win_ref[:, 0:qw])
    uk = _dot(xn, win_ref[:, qw:2 * qw])
    _qk_prep(uq, qg_ref, c, s1, s2, ones_bd, q_scale, q_ref)
    _qk_prep(uk, kg_ref, c, s1, s2, ones_bd, 1.0, k_ref)
    v_ref[...] = _dot(xn, win_ref[:, 2 * qw:DA_IN]).astype(BF16)
    urw_ref[...] = _dot(xn, win_ref[:, DA_IN:N_IN_PAD])


def _mix_in(h, norm_g, w_in, layer, q_gain, k_gain, rope_c, rope_s1, rope_s2, ones_bd, l_pad):
    m, d = h.shape
    tm = ROW_TILE
    tiles_per_seq = l_pad // tm
    row = lambda i: (i, 0)
    pos = lambda i: (i % tiles_per_seq, 0)
    const2 = lambda i: (0, 0)
    q_scale = DA_QK_DIM ** -0.5 * math.log2(math.e)
    return pl.pallas_call(
        functools.partial(_mixin_kernel, q_scale=q_scale, layer=layer),
        grid=(m // tm,),
        in_specs=[pl.BlockSpec((tm, d), row), pl.BlockSpec((1, d), const2),
                  pl.BlockSpec(memory_space=pl.ANY),
                  pl.BlockSpec((1, DA_QK_WIDTH), const2), pl.BlockSpec((1, DA_QK_WIDTH), const2),
                  pl.BlockSpec((tm, LANES), pos), pl.BlockSpec((tm, LANES), pos),
                  pl.BlockSpec((tm, LANES), pos), pl.BlockSpec((SEG_LANES, SEG_LANES), const2)],
        out_specs=[pl.BlockSpec((tm, DA_QK_WIDTH), row), pl.BlockSpec((tm, DA_QK_WIDTH), row),
                   pl.BlockSpec((tm, DA_WIDTH), row), pl.BlockSpec((tm, RW_IN_PAD), row)],
        out_shape=[jax.ShapeDtypeStruct((m, DA_QK_WIDTH), BF16), jax.ShapeDtypeStruct((m, DA_QK_WIDTH), BF16),
                   jax.ShapeDtypeStruct((m, DA_WIDTH), BF16), jax.ShapeDtypeStruct((m, RW_IN_PAD), F32)],
        scratch_shapes=[pltpu.VMEM((d, N_IN_PAD), BF16), pltpu.VMEM((2, WEIGHT_SLAB_ROWS, w_in.shape[-1]), F32),
                        pltpu.SemaphoreType.DMA((2,))],
        compiler_params=pltpu.CompilerParams(dimension_semantics=("arbitrary",),
                                             vmem_limit_bytes=VMEM_LIMIT_BYTES),
        name="mix_in",
    )(h, norm_g.reshape(1, d), w_in, q_gain, k_gain, rope_c, rope_s1, rope_s2, ones_bd)


def _attn_kernel(lam_ref, q_ref, k_ref, v_ref, sub_ref, o_ref, q2_s, vx_s, m_s, l_s, acc_s, s_s, *, tq, lam_init):
    qi = pl.program_id(2)
    n_heads = q_ref.shape[1] // LANES
    head_lanes = [slice(LANES * hh, LANES * (hh + 1)) for hh in range(n_heads)]
    vx_lanes = [slice(2 * LANES * hh, 2 * LANES * (hh + 1)) for hh in range(n_heads)]
    streams = [(hh, slice((2 * hh + c) * tq, (2 * hh + c + 1) * tq)) for hh in range(n_heads) for c in range(2)]

    @pl.when(qi == 0)
    def _():
        for hh in range(n_heads):
            vx_s[:, 2 * LANES * hh:2 * LANES * hh + LANES] = v_ref[:, head_lanes[hh]]
            vx_s[:, 2 * LANES * hh + LANES:2 * LANES * (hh + 1)] = jnp.ones((vx_s.shape[0], LANES), BF16)

    for hh in range(n_heads):
        q = q_ref[:, head_lanes[hh]]
        lane = lax.broadcasted_iota(jnp.int32, q.shape, 1)
        zero = jnp.zeros_like(q)
        q2_s[streams[2 * hh][1], :] = jnp.where(lane < DA_QK_DIM, q, zero)
        q2_s[streams[2 * hh + 1][1], :] = jnp.where(lane < DA_QK_DIM, zero, q)
    m_s[...] = jnp.full(m_s.shape, -1e30, F32)
    l_s[...] = jnp.zeros_like(l_s)
    acc_s[...] = jnp.zeros_like(acc_s)

    def process(start, width, diag_offset):
        def scores(st):
            hh, rows = streams[st]
            s_s[rows, 0:width] = _dot_nt(q2_s[rows, :], k_ref[pl.ds(start, width), head_lanes[hh]])

        lead = min(ATTN_SCORE_LEAD, len(streams))
        for st in range(lead):
            scores(st)
        for st, (hh, rows) in enumerate(streams):
            s = s_s[rows, 0:width]
            if diag_offset is not None:
                row = lax.broadcasted_iota(jnp.int32, s.shape, 0)
                col = lax.broadcasted_iota(jnp.int32, s.shape, 1)
                s = jnp.where(col <= row + diag_offset, s, -jnp.inf)
            m_prev = m_s[rows, :]
            m_new = jnp.maximum(m_prev, jnp.max(s, axis=1, keepdims=True))
            alpha = jnp.exp2(m_prev - m_new)
            p = jnp.exp2(s - jnp.concatenate([m_new] * (width // LANES), axis=1))
            pv = _dot(p.astype(BF16), vx_s[pl.ds(start, width), vx_lanes[hh]])
            acc_s[rows, :] = acc_s[rows, :] * alpha + pv[:, 0:LANES]
            l_s[rows, :] = l_s[rows, :] * alpha + pv[:, LANES:2 * LANES]
            m_s[rows, :] = m_new
            if st + lead < len(streams):
                scores(st + lead)

    def wide_body(t, carry):
        process(pl.multiple_of(t * (2 * tq), 2 * tq), 2 * tq, None)
        return carry

    n_wide = qi // 2

    def wide_pair(t, carry):
        wide_body(2 * t, carry)
        return wide_body(2 * t + 1, carry)

    lax.fori_loop(0, n_wide // 2, wide_pair, 0)

    @pl.when(n_wide % 2 == 1)
    def _():
        wide_body(n_wide - 1, 0)

    @pl.when(qi % 2 == 0)
    def _():
        process(pl.multiple_of(qi * tq, tq), tq, 0)

    @pl.when(qi % 2 == 1)
    def _():
        process(pl.multiple_of((qi - 1) * tq, tq), 2 * tq, tq)

    lam1 = jnp.exp(jnp.sum(lam_ref[0:1, :] * lam_ref[1:2, :], axis=1, keepdims=True))
    lam2 = jnp.exp(jnp.sum(lam_ref[2:3, :] * lam_ref[3:4, :], axis=1, keepdims=True))
    lam = lam1 - lam2 + lam_init
    for hh in range(n_heads):
        r0, r1 = streams[2 * hh][1], streams[2 * hh + 1][1]
        o = acc_s[r0, :] / l_s[r0, :] - lam * (acc_s[r1, :] / l_s[r1, :])
        o = _rms_rows(o, sub_ref[...]) * (1.0 - lam_init)
        o_ref[:, head_lanes[hh]] = o.astype(o_ref.dtype)


def _attention(q, k, v, lam_vecs, subln, lam_init):
    b, l_pad, _ = q.shape
    tq = ATTN_TILE
    nq = l_pad // tq
    nh = ATTN_HEADS_PER_STEP
    hw = nh * LANES
    n_streams = 2 * nh
    kv_spec = pl.BlockSpec((None, l_pad, hw), lambda bi, h, qi: (bi, 0, h))
    return pl.pallas_call(
        functools.partial(_attn_kernel, tq=tq, lam_init=lam_init),
        grid=(b, DA_HEADS // nh, nq),
        in_specs=[pl.BlockSpec(lam_vecs.shape, lambda bi, h, qi: (0, 0)),
                  pl.BlockSpec((None, tq, hw), lambda bi, h, qi: (bi, qi, h)),
                  kv_spec, kv_spec,
                  pl.BlockSpec((1, DA_V_DIM), lambda bi, h, qi: (0, 0))],
        out_specs=pl.BlockSpec((None, tq, hw), lambda bi, h, qi: (bi, qi, h)),
        out_shape=jax.ShapeDtypeStruct((b, l_pad, DA_WIDTH), BF16),
        scratch_shapes=[pltpu.VMEM((n_streams * tq, LANES), BF16), pltpu.VMEM((l_pad, 2 * hw), BF16),
                        pltpu.VMEM((n_streams * tq, LANES), F32), pltpu.VMEM((n_streams * tq, LANES), F32),
                        pltpu.VMEM((n_streams * tq, LANES), F32),
                        pltpu.VMEM((n_streams * tq, 2 * tq), F32)],
        compiler_params=pltpu.CompilerParams(dimension_semantics=("arbitrary", "arbitrary", "arbitrary"),
                                             vmem_limit_bytes=VMEM_LIMIT_BYTES),
        name="diff_attention",
    )(lam_vecs, q, k, v, subln.reshape(1, DA_V_DIM))


def _expand_heads(x, lo_mask):
    zero = jnp.zeros_like(x)
    return jnp.concatenate([jnp.where(lo_mask, x, zero), jnp.where(lo_mask, zero, x)], axis=0)


def _run(stages):
    try:
        while True:
            next(stages)
    except StopIteration as stop:
        return stop.value


def _run_interleaved(stages_a, stages_b):
    live = [stages_a, stages_b]
    results = [None, None]
    while live[0] is not None or live[1] is not None:
        for n in range(2):
            if live[n] is None:
                continue
            try:
                next(live[n])
            except StopIteration as stop:
                results[n] = stop.value
                live[n] = None
    return results


def _unit_lower_inverse(a_list):
    n = a_list[0].shape[0]
    eye = (lax.broadcasted_iota(jnp.int32, (n, n), 0) == lax.broadcasted_iota(jnp.int32, (n, n), 1)).astype(F32)
    mm = lambda x, y: _dot(x.astype(BF16), y.astype(BF16))
    t_list = [eye + a for a in a_list]
    p_list = [mm(a, a) for a in a_list]
    yield
    levels = int(math.log2(RW_CHUNK))
    for _ in range(levels - 2):
        pt_list = [mm(p, jnp.concatenate([p, t], axis=1)) for p, t in zip(p_list, t_list)]
        t_list = [t + pt[:, n:] for t, pt in zip(t_list, pt_list)]
        p_list = [pt[:, :n] for pt in pt_list]
        yield
    return [t + mm(p, t) for p, t in zip(p_list, t_list)]


_RW_CARRY = ("ar", "bke", "v2", "t_inv", "akv", "arbk", "g_end", "bonus", "gate")


def _rwkv_kernel(u_ref, next_ref, mix_ref, w2a2_ref, w0_ref, a0_ref, g2_ref, kk_ref, ka_ref, rk_ref,
                 lnw_ref, lnb_ref, ones_ref, tri_ref, o_ref, s_ref, *carry_refs):
    i = pl.program_id(1)
    carry = dict(zip(_RW_CARRY, carry_refs))
    c_len = RW_CHUNK
    w = RW_WIDTH
    n2 = 2 * c_len
    n_sub = u_ref.shape[0] // RW_ROWS
    n_chunks = RW_ROWS // c_len
    n_pairs = w // LANES
    pair_lanes = [slice(LANES * p, LANES * (p + 1)) for p in range(n_pairs)]
    seg_lanes = [slice(SEG_LANES * p, SEG_LANES * (p + 1)) for p in range(w // SEG_LANES)]

    @pl.when(i == 0)
    def _():
        s_ref[...] = jnp.zeros_like(s_ref)

    ones_bd = ones_ref[...]
    tri = tri_ref[...]
    r_i = lax.broadcasted_iota(jnp.int32, (n2, n2), 0)
    c_i = lax.broadcasted_iota(jnp.int32, (n2, n2), 1)
    t_row = jnp.where(r_i >= c_len, r_i - c_len, r_i)
    t_col = jnp.where(c_i >= c_len, c_i - c_len, c_i)
    strict = t_col < t_row
    incl = t_col <= t_row
    lo_mask = lax.broadcasted_iota(jnp.int32, (c_len, LANES), 1) < RW_HEAD

    def prepare(src_ref, base, prev_row):
        def shifted(lo, hi):
            u = src_ref[base:base + RW_ROWS, lo:hi]
            last_prev = prev_row(lo, hi)
            rolled = pltpu.roll(u, 1, 0)
            row = lax.broadcasted_iota(jnp.int32, u.shape, 0)
            prev = jnp.where(row == 0, last_prev, rolled)
            return u + (prev - u) * mix_ref[:, lo:hi]

        r = shifted(0, w)
        k = shifted(w, 2 * w)
        v = shifted(2 * w, 3 * w)
        lora_in = shifted(3 * w, 3 * w + LANES)
        g_in = shifted(3 * w + LANES, RW_IN_PAD)

        lane = lax.broadcasted_iota(jnp.int32, lora_in.shape, 1)
        lora_act = jnp.where(lane < D_DECAY_LORA, jnp.tanh(lora_in), lora_in)
        wa = _dot(lora_act.astype(BF16), w2a2_ref[...])
        g = _dot(jax.nn.sigmoid(g_in).astype(BF16), g2_ref[...])
        yield
        w_pre = w0_ref[...] + wa[:, 0:w]
        lw = (-math.exp(-0.5)) * jax.nn.sigmoid(w_pre)
        a = jax.nn.sigmoid(a0_ref[...] + wa[:, w:2 * w])

        cums = []
        for c in range(n_chunks):
            lw_c = lw[c * c_len:(c + 1) * c_len]
            hi, lo = _split2(lw_c)
            cums.append(_dot(tri, hi) + _dot(tri, lo))

        kk_raw = k * kk_ref[...]
        k_mod = k * (1.0 + (a - 1.0) * ka_ref[...])
        rkr = r * k_mod * rk_ref[...]
        ss_l = [_seg_sum(kk_raw[:, sl] * kk_raw[:, sl], ones_bd) for sl in seg_lanes]
        rk_l = [_seg_sum(rkr[:, sl], ones_bd) for sl in seg_lanes]
        yield
        kk = jnp.concatenate([kk_raw[:, sl] * lax.rsqrt(jnp.maximum(ss, 1e-24))
                              for sl, ss in zip(seg_lanes, ss_l)], axis=1)
        bonus = jnp.concatenate([s * v[:, sl] for sl, s in zip(seg_lanes, rk_l)], axis=1)
        a_neg = -kk
        b_vec = kk * a

        ar_l, bk_l, bke_l, v2_l, g_end_l = [], [], [], [], []
        for c in range(n_chunks):
            rs = slice(c * c_len, (c + 1) * c_len)
            cum = cums[c]
            cum_end = cum[c_len - 1:c_len, :]
            e_in = jnp.exp(cum)
            e_out = jnp.exp(-cum)
            e_end = jnp.exp(cum_end - cum)
            r_t = r[rs] * e_in
            a_t = a_neg[rs] * jnp.exp(cum - lw[rs])
            k_t = k_mod[rs] * e_out
            b_t = b_vec[rs] * e_out
            k_e = k_mod[rs] * e_end
            b_e = b_vec[rs] * e_end
            g_end_l.append(jnp.exp(cum_end))
            v_c = v[rs]
            for sl in pair_lanes:
                ar_l.append(jnp.concatenate([_expand_heads(a_t[:, sl], lo_mask), _expand_heads(r_t[:, sl], lo_mask)],
                                            axis=0).astype(BF16))
                bk_l.append(jnp.concatenate([_expand_heads(b_t[:, sl], lo_mask), _expand_heads(k_t[:, sl], lo_mask)],
                                            axis=0).astype(BF16))
                bke_l.append(jnp.concatenate([_expand_heads(b_e[:, sl], lo_mask), _expand_heads(k_e[:, sl], lo_mask)],
                                             axis=0).astype(BF16))
                v2_l.append(_expand_heads(v_c[:, sl], lo_mask).astype(BF16))
        aa_l = [_dot_nt(ar, bk) for ar, bk in zip(ar_l, bk_l)]
        yield
        akv_l = [_dot(jnp.where(strict, aa[0:n2, n2:2 * n2], 0.0).astype(BF16), v2) for aa, v2 in zip(aa_l, v2_l)]
        arbk_l = [jnp.concatenate([jnp.where(incl, aa[n2:2 * n2, 0:n2], 0.0),
                                   jnp.where(incl, aa[n2:2 * n2, n2:2 * n2], 0.0)], axis=1).astype(BF16)
                  for aa in aa_l]
        t_inv_l = yield from _unit_lower_inverse([jnp.where(strict, aa[0:n2, 0:n2], 0.0) for aa in aa_l])
        t_inv_l = [t.astype(BF16) for t in t_inv_l]
        yield
        return dict(ar=ar_l, bke=bke_l, v2=v2_l, g_end=g_end_l, t_inv=t_inv_l, akv=akv_l, arbk=arbk_l,
                    bonus=bonus, gate=g)

    def advance(j, pre):
        base = j * RW_ROWS
        y_chunks = []
        for c in range(n_chunks):
            idx = [c * n_pairs + p for p in range(n_pairs)]
            states = [s_ref[p] for p in range(n_pairs)]
            arh = [_dot_nt(pre["ar"][q], s.astype(BF16)) for q, s in zip(idx, states)]
            yield
            u_l = [_dot(pre["t_inv"][q], (h[0:n2] + pre["akv"][q]).astype(BF16)) for q, h in zip(idx, arh)]
            yield
            uv = [jnp.concatenate([u.astype(BF16), pre["v2"][q]], axis=0) for q, u in zip(idx, u_l)]
            for p in range(n_pairs):
                s_ref[p] = states[p] * pre["g_end"][c][:, pair_lanes[p]] + _dot_tn(uv[p], pre["bke"][idx[p]])
            y2 = [h[n2:2 * n2] + _dot(pre["arbk"][q], x) for q, h, x in zip(idx, arh, uv)]
            yield
            y_chunks.append(jnp.concatenate([y[0:c_len] + y[c_len:n2] for y in y2], axis=1))
        y = jnp.concatenate(y_chunks, axis=0)

        mu_l = [_seg_sum(y[:, sl], ones_bd) * (1.0 / RW_HEAD) for sl in seg_lanes]
        yield
        d_l = [y[:, sl] - mu for sl, mu in zip(seg_lanes, mu_l)]
        var_l = [_seg_sum(d * d, ones_bd) * (1.0 / RW_HEAD) for d in d_l]
        yield
        yn = jnp.concatenate([d * lax.rsqrt(var + GN_EPS) for d, var in zip(d_l, var_l)], axis=1)
        yn = yn * lnw_ref[...] + lnb_ref[...]
        o_ref[base:base + RW_ROWS, :] = ((yn + pre["bonus"]) * pre["gate"]).astype(o_ref.dtype)

    def save(pre):
        for name in _RW_CARRY:
            value, ref = pre[name], carry[name]
            if isinstance(value, list):
                for q, item in enumerate(value):
                    ref[q] = item
            else:
                ref[...] = value

    def load():
        return {name: ([ref[q] for q in range(ref.shape[0])] if len(ref.shape) == 3 else ref[...])
                for name, ref in carry.items()}

    def row_before(src_ref, row):
        return lambda lo, hi: src_ref[row:row + 1, lo:hi]

    @pl.when(i == 0)
    def _():
        save(_run(prepare(u_ref, 0, lambda lo, hi: jnp.zeros((1, hi - lo), F32))))

    pre = load()
    for j in range(n_sub):
        last_row = (j + 1) * RW_ROWS - 1
        if j + 1 < n_sub:
            upcoming = prepare(u_ref, (j + 1) * RW_ROWS, row_before(u_ref, last_row))
        else:
            upcoming = prepare(next_ref, 0, row_before(u_ref, last_row))
        pre, _ = _run_interleaved(upcoming, advance(j, pre))
    save(pre)


def _rwkv(u_rw, shift_mix_p, w2a2, w0, a0, g2_p, k_k, k_a, r_k, ln_w, ln_b, ones_bd, tri):
    b, l_pad, _ = u_rw.shape
    rows = RW_BLOCK_ROWS
    sub_per_block = rows // RW_ROWS
    last_sub = l_pad // RW_ROWS - 1
    n_prob = (RW_ROWS // RW_CHUNK) * (RW_WIDTH // LANES)
    n2 = 2 * RW_CHUNK
    vec = lambda x: x.reshape(1, -1).astype(F32)
    const = lambda bi, i: (0, 0)
    vec_spec = pl.BlockSpec((1, RW_WIDTH), const)
    carry_shapes = dict(
        ar=pltpu.VMEM((n_prob, 2 * n2, LANES), BF16), bke=pltpu.VMEM((n_prob, 2 * n2, LANES), BF16),
        v2=pltpu.VMEM((n_prob, n2, LANES), BF16), t_inv=pltpu.VMEM((n_prob, n2, n2), BF16),
        akv=pltpu.VMEM((n_prob, n2, LANES), F32), arbk=pltpu.VMEM((n_prob, n2, 2 * n2), BF16),
        g_end=pltpu.VMEM((RW_ROWS // RW_CHUNK, 1, RW_WIDTH), F32),
        bonus=pltpu.VMEM((RW_ROWS, RW_WIDTH), F32), gate=pltpu.VMEM((RW_ROWS, RW_WIDTH), F32))
    return pl.pallas_call(
        _rwkv_kernel,
        grid=(b, l_pad // rows),
        in_specs=[pl.BlockSpec((None, rows, RW_IN_PAD), lambda bi, i: (bi, i, 0)),
                  pl.BlockSpec((None, RW_ROWS, RW_IN_PAD),
                               lambda bi, i: (bi, jnp.minimum((i + 1) * sub_per_block, last_sub), 0)),
                  pl.BlockSpec((1, RW_IN_PAD), const),
                  pl.BlockSpec(w2a2.shape, const),
                  vec_spec, vec_spec,
                  pl.BlockSpec(g2_p.shape, const),
                  vec_spec, vec_spec, vec_spec, vec_spec, vec_spec,
                  pl.BlockSpec((SEG_LANES, SEG_LANES), const),
                  pl.BlockSpec((RW_CHUNK, RW_CHUNK), const)],
        out_specs=pl.BlockSpec((None, rows, RW_WIDTH), lambda bi, i: (bi, i, 0)),
        out_shape=jax.ShapeDtypeStruct((b, l_pad, RW_WIDTH), BF16),
        scratch_shapes=[pltpu.VMEM((RW_WIDTH // LANES, LANES, LANES), F32)] + [carry_shapes[n] for n in _RW_CARRY],
        compiler_params=pltpu.CompilerParams(dimension_semantics=("arbitrary", "arbitrary"),
                                             vmem_limit_bytes=VMEM_LIMIT_BYTES),
        name="rwkv7",
    )(u_rw, u_rw, vec(shift_mix_p), w2a2, vec(w0), vec(a0), g2_p, vec(k_k), vec(k_a), vec(r_k),
      vec(ln_w), vec(ln_b), ones_bd, tri)


def _rope_lane_tables(length):
    pos = jnp.arange(length, dtype=F32)
    inv_freq = ROPE_THETA ** (-jnp.arange(0, ROPE_DIM, 2, dtype=F32) / ROPE_DIM)
    ang = pos[:, None] * inv_freq[None, :]
    cos, sin = jnp.cos(ang), jnp.sin(ang)
    half = ROPE_DIM // 2
    rest = DA_QK_DIM - ROPE_DIM
    ones = jnp.ones((length, rest), F32)
    zeros = lambda n: jnp.zeros((length, n), F32)
    c = jnp.concatenate([cos, cos, ones], axis=1)
    s1 = jnp.concatenate([-sin, zeros(half + rest)], axis=1)
    s2 = jnp.concatenate([zeros(half), sin, zeros(rest)], axis=1)
    rep = LANES // DA_QK_DIM
    return jnp.tile(c, (1, rep)), jnp.tile(s1, (1, rep)), jnp.tile(s2, (1, rep))


def kernel(x, meta_tokens, ffn1_norm, ffn1_w_gate, ffn1_w_up, ffn1_w_down, mix_norm, w_in, da_q_norm, da_k_norm, da_lambda_q1, da_lambda_k1, da_lambda_q2, da_lambda_k2, da_subln, rw_shift_mix, rw_w0, rw_w2, rw_a0, rw_a2, rw_g2, rw_k_k, rw_k_a, rw_r_k, rw_ln_w, rw_ln_b, w_out, ffn2_norm, ffn2_w_gate, ffn2_w_up, ffn2_w_down):
    bsz, t, d = x.shape
    depth = w_in.shape[0]
    l = N_META + t
    l_pad = -(-l // Q_BLOCK) * Q_BLOCK
    assert d == D_MODEL and l_pad % ROW_TILE == 0 and l_pad % ATTN_TILE == 0 and l_pad % RW_BLOCK_ROWS == 0
    h = x.reshape(bsz * t, d)
    meta = meta_tokens.astype(x.dtype)

    rope_c, rope_s1, rope_s2 = _rope_lane_tables(l_pad)
    lane_head = jnp.arange(SEG_LANES) // RW_HEAD
    ones_bd = (lane_head[:, None] == lane_head[None, :]).astype(BF16)
    tri = (jnp.arange(RW_CHUNK)[:, None] >= jnp.arange(RW_CHUNK)[None, :]).astype(BF16)

    for layer in range(depth):
        lam_init = 0.8 - 0.6 * math.exp(-0.3 * layer)
        shift_mix_p = jnp.pad(rw_shift_mix[layer], (0, RW_IN_PAD - RW_IN))
        zeros_lora = jnp.zeros((D_DECAY_LORA, RW_WIDTH), F32)
        w2a2 = jnp.concatenate([jnp.concatenate([rw_w2[layer], zeros_lora], axis=1),
                                jnp.concatenate([zeros_lora, rw_a2[layer]], axis=1)], axis=0).astype(BF16)
        g2_p = jnp.pad(rw_g2[layer], ((0, RW_LORA_PAD - D_GATE_LORA), (0, 0))).astype(BF16)
        q_gain = jnp.tile(da_q_norm[layer], DA_QK_WIDTH // DA_QK_DIM).reshape(1, DA_QK_WIDTH)
        k_gain = jnp.tile(da_k_norm[layer], DA_QK_WIDTH // DA_QK_DIM).reshape(1, DA_QK_WIDTH)
        lam_vecs = jnp.stack([da_lambda_q1[layer], da_lambda_k1[layer],
                              da_lambda_q2[layer], da_lambda_k2[layer]]).astype(F32)

        h = _ffn(h, ffn1_norm[layer], ffn1_w_gate, ffn1_w_up, ffn1_w_down, layer,
                 assemble=(meta, l_pad, t) if layer == 0 else None)
        q, k, v, u_rw = _mix_in(h, mix_norm[layer], w_in, layer, q_gain, k_gain,
                                rope_c, rope_s1, rope_s2, ones_bd, l_pad)
        shape3 = lambda a: a.reshape(bsz, l_pad, a.shape[-1])
        o_da = _attention(shape3(q), shape3(k), shape3(v), lam_vecs, da_subln[layer], lam_init)
        o_rw = _rwkv(shape3(u_rw), shift_mix_p, w2a2, rw_w0[layer], rw_a0[layer], g2_p,
                     rw_k_k[layer], rw_k_a[layer], rw_r_k[layer], rw_ln_w[layer], rw_ln_b[layer],
                     ones_bd, tri)
        last = layer == depth - 1
        h = _ffn(h, ffn2_norm[layer], ffn2_w_gate, ffn2_w_up, ffn2_w_down, layer,
                 mix=(o_da.reshape(bsz * l_pad, DA_WIDTH), o_rw.reshape(bsz * l_pad, RW_WIDTH), w_out),
                 compact=(l_pad, N_META, t) if last else None)
    return h.reshape(bsz, t, d)
```

```python
import functools
import math

import jax
import jax.numpy as jnp
from jax import lax
from jax.experimental import pallas as pl
from jax.experimental.pallas import tpu as pltpu

F32 = jnp.float32
BF16 = jnp.bfloat16

D_MODEL = 1024
N_META = 16
Q_BLOCK = 128
ROPE_THETA = 500000.0
NORM_EPS = 1e-6
DA_HEADS = 4
DA_QK_DIM = 64
DA_V_DIM = 2 * DA_QK_DIM
DA_WIDTH = DA_HEADS * DA_V_DIM
DA_QK_WIDTH = DA_HEADS * 2 * DA_QK_DIM
ROPE_DIM = DA_QK_DIM // 4
RW_HEAD = 64
RW_WIDTH = D_MODEL - DA_WIDTH
RW_HEADS = RW_WIDTH // RW_HEAD
D_DECAY_LORA = 64
D_AAA_LORA = 64
D_GATE_LORA = 160
GN_EPS = 64e-5
DA_IN = 2 * DA_QK_WIDTH + DA_WIDTH
RW_IN = 3 * RW_WIDTH + D_DECAY_LORA + D_AAA_LORA + D_GATE_LORA
D_FF = 2816

LANES = 128
SUBLANES = 8
SEG_LANES = 256
VMEM_LIMIT_BYTES = 56 * 1024 * 1024

ROW_TILE = 640
COMPACT_ROW_TILE = 512
FF_CHUNK = 256
WEIGHT_SLAB_ROWS = 256
ATTN_TILE = 640
ATTN_HEADS_PER_STEP = 2
ATTN_SCORE_LEAD = 2
RW_CHUNK = 64
RW_ROWS = 2 * RW_CHUNK
RW_BLOCK_ROWS = 5 * RW_ROWS
RW_LORA_PAD = 2 * LANES
RW_IN_PAD = 3 * RW_WIDTH + LANES + RW_LORA_PAD
N_IN_PAD = DA_IN + RW_IN_PAD

_NT = (((1,), (1,)), ((), ()))
_TN = (((0,), (0,)), ((), ()))


def _dot(a, b):
    return jnp.dot(a, b, preferred_element_type=F32)


def _dot_nt(a, b):
    return lax.dot_general(a, b, _NT, preferred_element_type=F32)


def _dot_tn(a, b):
    return lax.dot_general(a, b, _TN, preferred_element_type=F32)


def _split2(x):
    hi = x.astype(BF16)
    lo = (x - hi.astype(F32)).astype(BF16)
    return hi, lo


def _seg_sum(x, ones_bd):
    return _dot(x.astype(BF16), ones_bd)


def _rms_rows(h, gain):
    ms = jnp.mean(h * h, axis=-1, keepdims=True)
    return h * lax.rsqrt(ms + NORM_EPS) * gain


def _fetch_weights_bf16(jobs):
    slabs = []
    used = {}
    for hbm, vmem, stage, sem in jobs:
        rows, cols = hbm.shape
        assert rows % WEIGHT_SLAB_ROWS == 0 and stage.shape == (2, WEIGHT_SLAB_ROWS, cols), (hbm.shape, stage.shape)
        for r0 in range(0, rows, WEIGHT_SLAB_ROWS):
            slot = used.get(id(stage), 0) % 2
            used[id(stage)] = used.get(id(stage), 0) + 1
            slabs.append((hbm, vmem, stage, sem, r0, cols, slot))

    def copy(k):
        hbm, _, stage, sem, r0, _, slot = slabs[k]
        return pltpu.make_async_copy(hbm.at[r0:r0 + WEIGHT_SLAB_ROWS, :], stage.at[slot], sem.at[slot])

    copy(0).start()
    for k, (_, vmem, stage, _, r0, cols, slot) in enumerate(slabs):
        if k + 1 < len(slabs):
            copy(k + 1).start()
        copy(k).wait()
        vmem[r0:r0 + WEIGHT_SLAB_ROWS, 0:cols] = stage[slot].astype(BF16)
    for hbm, vmem, _, _ in jobs:
        if vmem.shape[1] > hbm.shape[1]:
            vmem[:, hbm.shape[1]:] = jnp.zeros((vmem.shape[0], vmem.shape[1] - hbm.shape[1]), BF16)


def _ffn_kernel(*refs, has_mix, assemble, layer):
    n_w = 4 if has_mix else 3
    n_in = len(refs) - (1 + 2 + n_w + 4)
    w_hbm = [w.at[layer] for w in refs[n_in - n_w:n_in]]
    o_ref, xn_ref, act_ref = refs[n_in:n_in + 3]
    w_vmem = refs[n_in + 3:n_in + 3 + n_w]
    stage_wide, stage_narrow, sem_wide, sem_narrow = refs[n_in + 3 + n_w:]
    g_ref = refs[n_in - n_w - 1]
    wg_ref, wu_ref, wd_ref = w_vmem[0:3]

    @pl.when(pl.program_id(0) == 0)
    def _():
        stages = [(stage_wide, sem_wide), (stage_wide, sem_wide), (stage_narrow, sem_narrow),
                  (stage_narrow, sem_narrow)]
        _fetch_weights_bf16([(h, v) + s for h, v, s in zip(w_hbm, w_vmem, stages)])

    if has_mix:
        h_ref, oda_ref, orw_ref = refs[0:3]
        wout_ref = w_vmem[3]
        h = (h_ref[...] + _dot(oda_ref[...], wout_ref[0:DA_WIDTH, :])
             + _dot(orw_ref[...], wout_ref[DA_WIDTH:DA_WIDTH + RW_WIDTH, :]))
        o_ref[...] = h
        res_ref = o_ref
    elif assemble is not None:
        x_ref, meta_ref = refs[0:2]
        tiles, n_prefix, last_shift, last_valid = assemble
        tm, d = x_ref.shape
        il = pl.program_id(0) % tiles
        x = x_ref[...]
        first, last = il == 0, il == tiles - 1
        h = jnp.where(first, pltpu.roll(x, n_prefix, 0), jnp.where(last, pltpu.roll(x, tm - last_shift, 0), x))
        row = lax.broadcasted_iota(jnp.int32, (tm, d), 0)
        prefix = jnp.concatenate([meta_ref[...], jnp.zeros((tm - n_prefix, d), F32)], axis=0)
        h = jnp.where(first & (row < n_prefix), prefix, h)
        h = jnp.where(last & (row >= last_valid), 0.0, h)
        o_ref[...] = h
        res_ref = o_ref
    else:
        h_ref = refs[0]
        h = h_ref[...]
        res_ref = h_ref
    xn_ref[...] = _rms_rows(h, g_ref[...]).astype(BF16)
    d_ff = wg_ref.shape[1]
    for c in range(d_ff // FF_CHUNK):
        cols = slice(c * FF_CHUNK, (c + 1) * FF_CHUNK)
        g = _dot(xn_ref[...], wg_ref[:, cols])
        u = _dot(xn_ref[...], wu_ref[:, cols])
        act_ref[:, cols] = ((g * jax.nn.sigmoid(g)) * u).astype(BF16)
    o_ref[...] = res_ref[...] + 0.5 * _dot(act_ref[...], wd_ref[...])


def _ffn(h, norm_g, wg, wu, wd, layer, mix=None, compact=None, assemble=None):
    m, d = h.shape
    d_ff = wg.shape[-1]
    row = lambda i: (i, 0)
    const2 = lambda i: (0, 0)
    kernel_assemble = None
    if assemble is not None:
        prefix, l_pad, n_rows = assemble
        n_prefix = prefix.shape[0]
        tm = ROW_TILE
        tiles = l_pad // tm
        n_seq = m // n_rows
        last_start = (tiles - 1) * tm - n_prefix
        assert tiles >= 2 and n_rows >= tm and n_prefix % SUBLANES == 0 and n_rows % SUBLANES == 0
        assert 0 <= last_start - (n_rows - tm) < tm
        kernel_assemble = (tiles, n_prefix, last_start - (n_rows - tm), n_prefix + n_rows - (tiles - 1) * tm)
        m_out = n_seq * l_pad
        window = lambda i: (pl.multiple_of(
            (i // tiles) * n_rows + jnp.clip((i % tiles) * tm - n_prefix, 0, n_rows - tm), SUBLANES), 0)
        row_spec = lambda width: pl.BlockSpec((pl.Element(tm), pl.Element(width)), window)
    elif compact is None:
        tm = ROW_TILE
        m_out = m
        row_spec = lambda width: pl.BlockSpec((tm, width), row)
    else:
        l_pad, first_row, n_rows = compact
        tm = COMPACT_ROW_TILE
        align = 2 * SUBLANES
        assert n_rows % tm == 0 and first_row % align == 0 and l_pad % align == 0 and tm % align == 0
        tiles = n_rows // tm
        m_out = (m // l_pad) * n_rows
        window = lambda i: (pl.multiple_of((i // tiles) * l_pad + first_row + (i % tiles) * tm, align), 0)
        row_spec = lambda width: pl.BlockSpec((pl.Element(tm), pl.Element(width)), window)
    in_specs = [row_spec(d)]
    args = [h]
    if assemble is not None:
        in_specs.append(pl.BlockSpec(prefix.shape, const2))
        args.append(prefix)
    weights = [wg, wu, wd]
    if mix is not None:
        o_da, o_rw, w_out = mix
        in_specs += [row_spec(DA_WIDTH), row_spec(RW_WIDTH)]
        args += [o_da, o_rw]
        weights.append(w_out)
    in_specs += [pl.BlockSpec((1, d), const2)] + [pl.BlockSpec(memory_space=pl.ANY)] * len(weights)
    args += [norm_g.reshape(1, d)] + weights
    return pl.pallas_call(
        functools.partial(_ffn_kernel, has_mix=mix is not None, assemble=kernel_assemble, layer=layer),
        grid=(m_out // tm,),
        in_specs=in_specs,
        out_specs=pl.BlockSpec((tm, d), row),
        out_shape=jax.ShapeDtypeStruct((m_out, d), F32),
        scratch_shapes=([pltpu.VMEM((tm, d), BF16), pltpu.VMEM((tm, d_ff), BF16)]
                        + [pltpu.VMEM(w.shape[-2:], BF16) for w in weights]
                        + [pltpu.VMEM((2, WEIGHT_SLAB_ROWS, d_ff), F32), pltpu.VMEM((2, WEIGHT_SLAB_ROWS, d), F32),
                           pltpu.SemaphoreType.DMA((2,)), pltpu.SemaphoreType.DMA((2,))]),
        compiler_params=pltpu.CompilerParams(dimension_semantics=("arbitrary",),
                                             vmem_limit_bytes=VMEM_LIMIT_BYTES),
        name="ffn_mix" if mix is not None else "ffn",
    )(*args)


def _qk_prep(u, gain_ref, c, s1, s2, ones_bd, scale, out_ref):
    rep = SEG_LANES // LANES
    c, s1, s2 = (jnp.concatenate([t] * rep, axis=1) for t in (c, s1, s2))
    for j in range(DA_QK_WIDTH // SEG_LANES):
        sl = slice(SEG_LANES * j, SEG_LANES * (j + 1))
        x = u[:, sl]
        ss = _seg_sum(x * x, ones_bd)
        xn = x * lax.rsqrt(ss * (1.0 / DA_QK_DIM) + NORM_EPS) * gain_ref[:, sl]
        half = ROPE_DIM // 2
        xr = xn * c + pltpu.roll(xn, SEG_LANES - half, 1) * s1 + pltpu.roll(xn, half, 1) * s2
        out_ref[:, sl] = (xr * scale).astype(BF16)


def _mixin_kernel(h_ref, g_ref, win_hbm, qg_ref, kg_ref, c_ref, s1_ref, s2_ref, ones_ref,
                  q_ref, k_ref, v_ref, urw_ref, win_ref, stage_ref, sem_ref, *, q_scale, layer):
    @pl.when(pl.program_id(0) == 0)
    def _():
        _fetch_weights_bf16([(win_hbm.at[layer], win_ref, stage_ref, sem_ref)])

    xn = _rms_rows(h_ref[...], g_ref[...]).astype(BF16)
    qw = DA_QK_WIDTH
    c, s1, s2, ones_bd = c_ref[...], s1_ref[...], s2_ref[...], ones_ref[...]
    uq = _dot(xn, win_ref[:, 0:qw])
    uk = _dot(xn, win_ref[:, qw:2 * qw])
    _qk_prep(uq, qg_ref, c, s1, s2, ones_bd, q_scale, q_ref)
    _qk_prep(uk, kg_ref, c, s1, s2, ones_bd, 1.0, k_ref)
    v_ref[...] = _dot(xn, win_ref[:, 2 * qw:DA_IN]).astype(BF16)
    urw_ref[...] = _dot(xn, win_ref[:, DA_IN:N_IN_PAD])


def _mix_in(h, norm_g, w_in, layer, q_gain, k_gain, rope_c, rope_s1, rope_s2, ones_bd, l_pad):
    m, d = h.shape
    tm = ROW_TILE
    tiles_per_seq = l_pad // tm
    row = lambda i: (i, 0)
    pos = lambda i: (i % tiles_per_seq, 0)
    const2 = lambda i: (0, 0)
    q_scale = DA_QK_DIM ** -0.5 * math.log2(math.e)
    return pl.pallas_call(
        functools.partial(_mixin_kernel, q_scale=q_scale, layer=layer),
        grid=(m // tm,),
        in_specs=[pl.BlockSpec((tm, d), row), pl.BlockSpec((1, d), const2),
                  pl.BlockSpec(memory_space=pl.ANY),
                  pl.BlockSpec((1, DA_QK_WIDTH), const2), pl.BlockSpec((1, DA_QK_WIDTH), const2),
                  pl.BlockSpec((tm, LANES), pos), pl.BlockSpec((tm, LANES), pos),
                  pl.BlockSpec((tm, LANES), pos), pl.BlockSpec((SEG_LANES, SEG_LANES), const2)],
        out_specs=[pl.BlockSpec((tm, DA_QK_WIDTH), row), pl.BlockSpec((tm, DA_QK_WIDTH), row),
                   pl.BlockSpec((tm, DA_WIDTH), row), pl.BlockSpec((tm, RW_IN_PAD), row)],
        out_shape=[jax.ShapeDtypeStruct((m, DA_QK_WIDTH), BF16), jax.ShapeDtypeStruct((m, DA_QK_WIDTH), BF16),
                   jax.ShapeDtypeStruct((m, DA_WIDTH), BF16), jax.ShapeDtypeStruct((m, RW_IN_PAD), F32)],
        scratch_shapes=[pltpu.VMEM((d, N_IN_PAD), BF16), pltpu.VMEM((2, WEIGHT_SLAB_ROWS, w_in.shape[-1]), F32),
                        pltpu.SemaphoreType.DMA((2,))],
        compiler_params=pltpu.CompilerParams(dimension_semantics=("arbitrary",),
                                             vmem_limit_bytes=VMEM_LIMIT_BYTES),
        name="mix_in",
    )(h, norm_g.reshape(1, d), w_in, q_gain, k_gain, rope_c, rope_s1, rope_s2, ones_bd)


def _attn_kernel(lam_ref, q_ref, k_ref, v_ref, sub_ref, o_ref, q2_s, vx_s, m_s, l_s, acc_s, s_s, *, tq, lam_init):
    qi = pl.program_id(2)
    n_heads = q_ref.shape[1] // LANES
    head_lanes = [slice(LANES * hh, LANES * (hh + 1)) for hh in range(n_heads)]
    vx_lanes = [slice(2 * LANES * hh, 2 * LANES * (hh + 1)) for hh in range(n_heads)]
    streams = [(hh, slice((2 * hh + c) * tq, (2 * hh + c + 1) * tq)) for hh in range(n_heads) for c in range(2)]

    @pl.when(qi == 0)
    def _():
        for hh in range(n_heads):
            vx_s[:, 2 * LANES * hh:2 * LANES * hh + LANES] = v_ref[:, head_lanes[hh]]
            vx_s[:, 2 * LANES * hh + LANES:2 * LANES * (hh + 1)] = jnp.ones((vx_s.shape[0], LANES), BF16)

    for hh in range(n_heads):
        q = q_ref[:, head_lanes[hh]]
        lane = lax.broadcasted_iota(jnp.int32, q.shape, 1)
        zero = jnp.zeros_like(q)
        q2_s[streams[2 * hh][1], :] = jnp.where(lane < DA_QK_DIM, q, zero)
        q2_s[streams[2 * hh + 1][1], :] = jnp.where(lane < DA_QK_DIM, zero, q)
    m_s[...] = jnp.full(m_s.shape, -1e30, F32)
    l_s[...] = jnp.zeros_like(l_s)
    acc_s[...] = jnp.zeros_like(acc_s)

    def process(start, width, diag_offset):
        def scores(st):
            hh, rows = streams[st]
            s_s[rows, 0:width] = _dot_nt(q2_s[rows, :], k_ref[pl.ds(start, width), head_lanes[hh]])

        lead = min(ATTN_SCORE_LEAD, len(streams))
        for st in range(lead):
            scores(st)
        for st, (hh, rows) in enumerate(streams):
            s = s_s[rows, 0:width]
            if diag_offset is not None:
                row = lax.broadcasted_iota(jnp.int32, s.shape, 0)
                col = lax.broadcasted_iota(jnp.int32, s.shape, 1)
                s = jnp.where(col <= row + diag_offset, s, -jnp.inf)
            m_prev = m_s[rows, :]
            m_new = jnp.maximum(m_prev, jnp.max(s, axis=1, keepdims=True))
            alpha = jnp.exp2(m_prev - m_new)
            p = jnp.exp2(s - jnp.concatenate([m_new] * (width // LANES), axis=1))
            pv = _dot(p.astype(BF16), vx_s[pl.ds(start, width), vx_lanes[hh]])
            acc_s[rows, :] = acc_s[rows, :] * alpha + pv[:, 0:LANES]
            l_s[rows, :] = l_s[rows, :] * alpha + pv[:, LANES:2 * LANES]
            m_s[rows, :] = m_new
            if st + lead < len(streams):
                scores(st + lead)

    def wide_body(t, carry):
        process(pl.multiple_of(t * (2 * tq), 2 * tq), 2 * tq, None)
        return carry

    n_wide = qi // 2

    def wide_pair(t, carry):
        wide_body(2 * t, carry)
        return wide_body(2 * t + 1, carry)

    lax.fori_loop(0, n_wide // 2, wide_pair, 0)

    @pl.when(n_wide % 2 == 1)
    def _():
        wide_body(n_wide - 1, 0)

    @pl.when(qi % 2 == 0)
    def _():
        process(pl.multiple_of(qi * tq, tq), tq, 0)

    @pl.when(qi % 2 == 1)
    def _():
        process(pl.multiple_of((qi - 1) * tq, tq), 2 * tq, tq)

    lam1 = jnp.exp(jnp.sum(lam_ref[0:1, :] * lam_ref[1:2, :], axis=1, keepdims=True))
    lam2 = jnp.exp(jnp.sum(lam_ref[2:3, :] * lam_ref[3:4, :], axis=1, keepdims=True))
    lam = lam1 - lam2 + lam_init
    for hh in range(n_heads):
        r0, r1 = streams[2 * hh][1], streams[2 * hh + 1][1]
        o = acc_s[r0, :] / l_s[r0, :] - lam * (acc_s[r1, :] / l_s[r1, :])
        o = _rms_rows(o, sub_ref[...]) * (1.0 - lam_init)
        o_ref[:, head_lanes[hh]] = o.astype(o_ref.dtype)


def _attention(q, k, v, lam_vecs, subln, lam_init):
    b, l_pad, _ = q.shape
    tq = ATTN_TILE
    nq = l_pad // tq
    nh = ATTN_HEADS_PER_STEP
    hw = nh * LANES
    n_streams = 2 * nh
    kv_spec = pl.BlockSpec((None, l_pad, hw), lambda bi, h, qi: (bi, 0, h))
    return pl.pallas_call(
        functools.partial(_attn_kernel, tq=tq, lam_init=lam_init),
        grid=(b, DA_HEADS // nh, nq),
        in_specs=[pl.BlockSpec(lam_vecs.shape, lambda bi, h, qi: (0, 0)),
                  pl.BlockSpec((None, tq, hw), lambda bi, h, qi: (bi, qi, h)),
                  kv_spec, kv_spec,
                  pl.BlockSpec((1, DA_V_DIM), lambda bi, h, qi: (0, 0))],
        out_specs=pl.BlockSpec((None, tq, hw), lambda bi, h, qi: (bi, qi, h)),
        out_shape=jax.ShapeDtypeStruct((b, l_pad, DA_WIDTH), BF16),
        scratch_shapes=[pltpu.VMEM((n_streams * tq, LANES), BF16), pltpu.VMEM((l_pad, 2 * hw), BF16),
                        pltpu.VMEM((n_streams * tq, LANES), F32), pltpu.VMEM((n_streams * tq, LANES), F32),
                        pltpu.VMEM((n_streams * tq, LANES), F32),
                        pltpu.VMEM((n_streams * tq, 2 * tq), F32)],
        compiler_params=pltpu.CompilerParams(dimension_semantics=("arbitrary", "arbitrary", "arbitrary"),
                                             vmem_limit_bytes=VMEM_LIMIT_BYTES),
        name="diff_attention",
    )(lam_vecs, q, k, v, subln.reshape(1, DA_V_DIM))


def _expand_heads(x, lo_mask):
    zero = jnp.zeros_like(x)
    return jnp.concatenate([jnp.where(lo_mask, x, zero), jnp.where(lo_mask, zero, x)], axis=0)


def _run(stages):
    try:
        while True:
            next(stages)
    except StopIteration as stop:
        return stop.value


def _run_interleaved(stages_a, stages_b):
    live = [stages_a, stages_b]
    results = [None, None]
    while live[0] is not None or live[1] is not None:
        for n in range(2):
            if live[n] is None:
                continue
            try:
                next(live[n])
            except StopIteration as stop:
                results[n] = stop.value
                live[n] = None
    return results


def _unit_lower_inverse(a_list):
    n = a_list[0].shape[0]
    eye = (lax.broadcasted_iota(jnp.int32, (n, n), 0) == lax.broadcasted_iota(jnp.int32, (n, n), 1)).astype(F32)
    mm = lambda x, y: _dot(x.astype(BF16), y.astype(BF16))
    t_list = [eye + a for a in a_list]
    p_list = [mm(a, a) for a in a_list]
    yield
    levels = int(math.log2(RW_CHUNK))
    for _ in range(levels - 2):
        pt_list = [mm(p, jnp.concatenate([p, t], axis=1)) for p, t in zip(p_list, t_list)]
        t_list = [t + pt[:, n:] for t, pt in zip(t_list, pt_list)]
        p_list = [pt[:, :n] for pt in pt_list]
        yield
    return [t + mm(p, t) for p, t in zip(p_list, t_list)]


_RW_CARRY = ("wr", "bke", "v2", "u0", "arbk", "g_end", "bonus", "gate")


def _rwkv_kernel(u_ref, next_ref, mix_ref, w2a2_ref, w0_ref, a0_ref, g2_ref, kk_ref, ka_ref, rk_ref,
                 lnw_ref, lnb_ref, ones_ref, tri_ref, o_ref, s_ref, *carry_refs):
    i = pl.program_id(1)
    carry = dict(zip(_RW_CARRY, carry_refs))
    c_len = RW_CHUNK
    w = RW_WIDTH
    n2 = 2 * c_len
    n_sub = u_ref.shape[0] // RW_ROWS
    n_chunks = RW_ROWS // c_len
    n_pairs = w // LANES
    pair_lanes = [slice(LANES * p, LANES * (p + 1)) for p in range(n_pairs)]
    seg_lanes = [slice(SEG_LANES * p, SEG_LANES * (p + 1)) for p in range(w // SEG_LANES)]

    @pl.when(i == 0)
    def _():
        s_ref[...] = jnp.zeros_like(s_ref)

    ones_bd = ones_ref[...]
    tri = tri_ref[...]
    r_i = lax.broadcasted_iota(jnp.int32, (n2, n2), 0)
    c_i = lax.broadcasted_iota(jnp.int32, (n2, n2), 1)
    t_row = jnp.where(r_i >= c_len, r_i - c_len, r_i)
    t_col = jnp.where(c_i >= c_len, c_i - c_len, c_i)
    strict = t_col < t_row
    incl = t_col <= t_row
    lo_mask = lax.broadcasted_iota(jnp.int32, (c_len, LANES), 1) < RW_HEAD

    def prepare(src_ref, base, prev_row):
        def shifted(lo, hi):
            u = src_ref[base:base + RW_ROWS, lo:hi]
            last_prev = prev_row(lo, hi)
            rolled = pltpu.roll(u, 1, 0)
            row = lax.broadcasted_iota(jnp.int32, u.shape, 0)
            prev = jnp.where(row == 0, last_prev, rolled)
            return u + (prev - u) * mix_ref[:, lo:hi]

        r = shifted(0, w)
        k = shifted(w, 2 * w)
        v = shifted(2 * w, 3 * w)
        lora_in = shifted(3 * w, 3 * w + LANES)
        g_in = shifted(3 * w + LANES, RW_IN_PAD)

        lane = lax.broadcasted_iota(jnp.int32, lora_in.shape, 1)
        lora_act = jnp.where(lane < D_DECAY_LORA, jnp.tanh(lora_in), lora_in)
        wa = _dot(lora_act.astype(BF16), w2a2_ref[...])
        g = _dot(jax.nn.sigmoid(g_in).astype(BF16), g2_ref[...])
        yield
        w_pre = w0_ref[...] + wa[:, 0:w]
        lw = (-math.exp(-0.5)) * jax.nn.sigmoid(w_pre)
        a = jax.nn.sigmoid(a0_ref[...] + wa[:, w:2 * w])

        cums = []
        for c in range(n_chunks):
            lw_c = lw[c * c_len:(c + 1) * c_len]
            hi, lo = _split2(lw_c)
            cums.append(_dot(tri, hi) + _dot(tri, lo))

        kk_raw = k * kk_ref[...]
        k_mod = k * (1.0 + (a - 1.0) * ka_ref[...])
        rkr = r * k_mod * rk_ref[...]
        ss_l = [_seg_sum(kk_raw[:, sl] * kk_raw[:, sl], ones_bd) for sl in seg_lanes]
        rk_l = [_seg_sum(rkr[:, sl], ones_bd) for sl in seg_lanes]
        yield
        kk = jnp.concatenate([kk_raw[:, sl] * lax.rsqrt(jnp.maximum(ss, 1e-24))
                              for sl, ss in zip(seg_lanes, ss_l)], axis=1)
        bonus = jnp.concatenate([s * v[:, sl] for sl, s in zip(seg_lanes, rk_l)], axis=1)
        a_neg = -kk
        b_vec = kk * a

        ar_l, bk_l, bke_l, v2_l, g_end_l, a2_l, r2_l = [], [], [], [], [], [], []
        for c in range(n_chunks):
            rs = slice(c * c_len, (c + 1) * c_len)
            cum = cums[c]
            cum_end = cum[c_len - 1:c_len, :]
            e_in = jnp.exp(cum)
            e_out = jnp.exp(-cum)
            e_end = jnp.exp(cum_end - cum)
            r_t = r[rs] * e_in
            a_t = a_neg[rs] * jnp.exp(cum - lw[rs])
            k_t = k_mod[rs] * e_out
            b_t = b_vec[rs] * e_out
            k_e = k_mod[rs] * e_end
            b_e = b_vec[rs] * e_end
            g_end_l.append(jnp.exp(cum_end))
            v_c = v[rs]
            for sl in pair_lanes:
                a2_l.append(_expand_heads(a_t[:, sl], lo_mask).astype(BF16))
                r2_l.append(_expand_heads(r_t[:, sl], lo_mask).astype(BF16))
                ar_l.append(jnp.concatenate([a2_l[-1], r2_l[-1]], axis=0))
                bk_l.append(jnp.concatenate([_expand_heads(b_t[:, sl], lo_mask), _expand_heads(k_t[:, sl], lo_mask)],
                                            axis=0).astype(BF16))
                bke_l.append(jnp.concatenate([_expand_heads(b_e[:, sl], lo_mask), _expand_heads(k_e[:, sl], lo_mask)],
                                             axis=0).astype(BF16))
                v2_l.append(_expand_heads(v_c[:, sl], lo_mask).astype(BF16))
        aa_l = [_dot_nt(ar, bk) for ar, bk in zip(ar_l, bk_l)]
        yield
        akv_l = [_dot(jnp.where(strict, aa[0:n2, n2:2 * n2], 0.0).astype(BF16), v2) for aa, v2 in zip(aa_l, v2_l)]
        arbk_l = [jnp.concatenate([jnp.where(incl, aa[n2:2 * n2, 0:n2], 0.0),
                                   jnp.where(incl, aa[n2:2 * n2, n2:2 * n2], 0.0)], axis=1).astype(BF16)
                  for aa in aa_l]
        t_inv_l = yield from _unit_lower_inverse([jnp.where(strict, aa[0:n2, 0:n2], 0.0) for aa in aa_l])
        t_inv_l = [t.astype(BF16) for t in t_inv_l]
        yield
        w_l = [_dot(t, a2) for t, a2 in zip(t_inv_l, a2_l)]
        u0_l = [_dot(t, akv.astype(BF16)) for t, akv in zip(t_inv_l, akv_l)]
        wr_l = [jnp.concatenate([w_.astype(BF16), r2], axis=0) for w_, r2 in zip(w_l, r2_l)]
        yield
        return dict(wr=wr_l, bke=bke_l, v2=v2_l, g_end=g_end_l, u0=u0_l, arbk=arbk_l, bonus=bonus, gate=g)

    def advance(j, pre):
        base = j * RW_ROWS
        y_chunks = []
        for c in range(n_chunks):
            idx = [c * n_pairs + p for p in range(n_pairs)]
            states = [s_ref[p] for p in range(n_pairs)]
            arh = [_dot_nt(pre["wr"][q], s.astype(BF16)) for q, s in zip(idx, states)]
            yield
            uv = [jnp.concatenate([(h[0:n2] + pre["u0"][q]).astype(BF16), pre["v2"][q]], axis=0)
                  for q, h in zip(idx, arh)]
            for p in range(n_pairs):
                s_ref[p] = states[p] * pre["g_end"][c][:, pair_lanes[p]] + _dot_tn(uv[p], pre["bke"][idx[p]])
            y2 = [h[n2:2 * n2] + _dot(pre["arbk"][q], x) for q, h, x in zip(idx, arh, uv)]
            yield
            y_chunks.append(jnp.concatenate([y[0:c_len] + y[c_len:n2] for y in y2], axis=1))
        y = jnp.concatenate(y_chunks, axis=0)

        mu_l = [_seg_sum(y[:, sl], ones_bd) * (1.0 / RW_HEAD) for sl in seg_lanes]
        yield
        d_l = [y[:, sl] - mu for sl, mu in zip(seg_lanes, mu_l)]
        var_l = [_seg_sum(d * d, ones_bd) * (1.0 / RW_HEAD) for d in d_l]
        yield
        yn = jnp.concatenate([d * lax.rsqrt(var + GN_EPS) for d, var in zip(d_l, var_l)], axis=1)
        yn = yn * lnw_ref[...] + lnb_ref[...]
        o_ref[base:base + RW_ROWS, :] = ((yn + pre["bonus"]) * pre["gate"]).astype(o_ref.dtype)

    def save(pre):
        for name in _RW_CARRY:
            value, ref = pre[name], carry[name]
            if isinstance(value, list):
                for q, item in enumerate(value):
                    ref[q] = item
            else:
                ref[...] = value

    def load():
        return {name: ([ref[q] for q in range(ref.shape[0])] if len(ref.shape) == 3 else ref[...])
                for name, ref in carry.items()}

    def row_before(src_ref, row):
        return lambda lo, hi: src_ref[row:row + 1, lo:hi]

    @pl.when(i == 0)
    def _():
        save(_run(prepare(u_ref, 0, lambda lo, hi: jnp.zeros((1, hi - lo), F32))))

    pre = load()
    for j in range(n_sub):
        last_row = (j + 1) * RW_ROWS - 1
        if j + 1 < n_sub:
            upcoming = prepare(u_ref, (j + 1) * RW_ROWS, row_before(u_ref, last_row))
        else:
            upcoming = prepare(next_ref, 0, row_before(u_ref, last_row))
        pre, _ = _run_interleaved(upcoming, advance(j, pre))
    save(pre)


def _rwkv(u_rw, shift_mix_p, w2a2, w0, a0, g2_p, k_k, k_a, r_k, ln_w, ln_b, ones_bd, tri):
    b, l_pad, _ = u_rw.shape
    rows = RW_BLOCK_ROWS
    sub_per_block = rows // RW_ROWS
    last_sub = l_pad // RW_ROWS - 1
    n_prob = (RW_ROWS // RW_CHUNK) * (RW_WIDTH // LANES)
    n2 = 2 * RW_CHUNK
    vec = lambda x: x.reshape(1, -1).astype(F32)
    const = lambda bi, i: (0, 0)
    vec_spec = pl.BlockSpec((1, RW_WIDTH), const)
    carry_shapes = dict(
        wr=pltpu.VMEM((n_prob, 2 * n2, LANES), BF16), bke=pltpu.VMEM((n_prob, 2 * n2, LANES), BF16),
        v2=pltpu.VMEM((n_prob, n2, LANES), BF16),
        u0=pltpu.VMEM((n_prob, n2, LANES), F32), arbk=pltpu.VMEM((n_prob, n2, 2 * n2), BF16),
        g_end=pltpu.VMEM((RW_ROWS // RW_CHUNK, 1, RW_WIDTH), F32),
        bonus=pltpu.VMEM((RW_ROWS, RW_WIDTH), F32), gate=pltpu.VMEM((RW_ROWS, RW_WIDTH), F32))
    return pl.pallas_call(
        _rwkv_kernel,
        grid=(b, l_pad // rows),
        in_specs=[pl.BlockSpec((None, rows, RW_IN_PAD), lambda bi, i: (bi, i, 0)),
                  pl.BlockSpec((None, RW_ROWS, RW_IN_PAD),
                               lambda bi, i: (bi, jnp.minimum((i + 1) * sub_per_block, last_sub), 0)),
                  pl.BlockSpec((1, RW_IN_PAD), const),
                  pl.BlockSpec(w2a2.shape, const),
                  vec_spec, vec_spec,
                  pl.BlockSpec(g2_p.shape, const),
                  vec_spec, vec_spec, vec_spec, vec_spec, vec_spec,
                  pl.BlockSpec((SEG_LANES, SEG_LANES), const),
                  pl.BlockSpec((RW_CHUNK, RW_CHUNK), const)],
        out_specs=pl.BlockSpec((None, rows, RW_WIDTH), lambda bi, i: (bi, i, 0)),
        out_shape=jax.ShapeDtypeStruct((b, l_pad, RW_WIDTH), BF16),
        scratch_shapes=[pltpu.VMEM((RW_WIDTH // LANES, LANES, LANES), F32)] + [carry_shapes[n] for n in _RW_CARRY],
        compiler_params=pltpu.CompilerParams(dimension_semantics=("arbitrary", "arbitrary"),
                                             vmem_limit_bytes=VMEM_LIMIT_BYTES),
        name="rwkv7",
    )(u_rw, u_rw, vec(shift_mix_p), w2a2, vec(w0), vec(a0), g2_p, vec(k_k), vec(k_a), vec(r_k),
      vec(ln_w), vec(ln_b), ones_bd, tri)


def _rope_lane_tables(length):
    pos = jnp.arange(length, dtype=F32)
    inv_freq = ROPE_THETA ** (-jnp.arange(0, ROPE_DIM, 2, dtype=F32) / ROPE_DIM)
    ang = pos[:, None] * inv_freq[None, :]
    cos, sin = jnp.cos(ang), jnp.sin(ang)
    half = ROPE_DIM // 2
    rest = DA_QK_DIM - ROPE_DIM
    ones = jnp.ones((length, rest), F32)
    zeros = lambda n: jnp.zeros((length, n), F32)
    c = jnp.concatenate([cos, cos, ones], axis=1)
    s1 = jnp.concatenate([-sin, zeros(half + rest)], axis=1)
    s2 = jnp.concatenate([zeros(half), sin, zeros(rest)], axis=1)
    rep = LANES // DA_QK_DIM
    return jnp.tile(c, (1, rep)), jnp.tile(s1, (1, rep)), jnp.tile(s2, (1, rep))


def kernel(x, meta_tokens, ffn1_norm, ffn1_w_gate, ffn1_w_up, ffn1_w_down, mix_norm, w_in, da_q_norm, da_k_norm, da_lambda_q1, da_lambda_k1, da_lambda_q2, da_lambda_k2, da_subln, rw_shift_mix, rw_w0, rw_w2, rw_a0, rw_a2, rw_g2, rw_k_k, rw_k_a, rw_r_k, rw_ln_w, rw_ln_b, w_out, ffn2_norm, ffn2_w_gate, ffn2_w_up, ffn2_w_down):
    bsz, t, d = x.shape
    depth = w_in.shape[0]
    l = N_META + t
    l_pad = -(-l // Q_BLOCK) * Q_BLOCK
    assert d == D_MODEL and l_pad % ROW_TILE == 0 and l_pad % ATTN_TILE == 0 and l_pad % RW_BLOCK_ROWS == 0
    h = x.reshape(bsz * t, d)
    meta = meta_tokens.astype(x.dtype)

    rope_c, rope_s1, rope_s2 = _rope_lane_tables(l_pad)
    lane_head = jnp.arange(SEG_LANES) // RW_HEAD
    ones_bd = (lane_head[:, None] == lane_head[None, :]).astype(BF16)
    tri = (jnp.arange(RW_CHUNK)[:, None] >= jnp.arange(RW_CHUNK)[None, :]).astype(BF16)

    for layer in range(depth):
        lam_init = 0.8 - 0.6 * math.exp(-0.3 * layer)
        shift_mix_p = jnp.pad(rw_shift_mix[layer], (0, RW_IN_PAD - RW_IN))
        zeros_lora = jnp.zeros((D_DECAY_LORA, RW_WIDTH), F32)
        w2a2 = jnp.concatenate([jnp.concatenate([rw_w2[layer], zeros_lora], axis=1),
                                jnp.concatenate([zeros_lora, rw_a2[layer]], axis=1)], axis=0).astype(BF16)
        g2_p = jnp.pad(rw_g2[layer], ((0, RW_LORA_PAD - D_GATE_LORA), (0, 0))).astype(BF16)
        q_gain = jnp.tile(da_q_norm[layer], DA_QK_WIDTH // DA_QK_DIM).reshape(1, DA_QK_WIDTH)
        k_gain = jnp.tile(da_k_norm[layer], DA_QK_WIDTH // DA_QK_DIM).reshape(1, DA_QK_WIDTH)
        lam_vecs = jnp.stack([da_lambda_q1[layer], da_lambda_k1[layer],
                              da_lambda_q2[layer], da_lambda_k2[layer]]).astype(F32)

        h = _ffn(h, ffn1_norm[layer], ffn1_w_gate, ffn1_w_up, ffn1_w_down, layer,
                 assemble=(meta, l_pad, t) if layer == 0 else None)
        q, k, v, u_rw = _mix_in(h, mix_norm[layer], w_in, layer, q_gain, k_gain,
                                rope_c, rope_s1, rope_s2, ones_bd, l_pad)
        shape3 = lambda a: a.reshape(bsz, l_pad, a.shape[-1])
        o_da = _attention(shape3(q), shape3(k), shape3(v), lam_vecs, da_subln[layer], lam_init)
        o_rw = _rwkv(shape3(u_rw), shift_mix_p, w2a2, rw_w0[layer], rw_a0[layer], g2_p,
                     rw_k_k[layer], rw_k_a[layer], rw_r_k[layer], rw_ln_w[layer], rw_ln_b[layer],
                     ones_bd, tri)
        last = layer == depth - 1
        h = _ffn(h, ffn2_norm[layer], ffn2_w_gate, ffn2_w_up, ffn2_w_down, layer,
                 mix=(o_da.reshape(bsz * l_pad, DA_WIDTH), o_rw.reshape(bsz * l_pad, RW_WIDTH), w_out),
                 compact=(l_pad, N_META, t) if last else None)
    return h.reshape(bsz, t, d)
```
